```python
import math
import jax, jax.numpy as jnp
from jax import lax
import numpy as np

D_MODEL = 1024
BATCH = 8
SEQ = 4096
DEPTH = 1

SSM_WIDTH = D_MODEL // 2
SSM_GROUP = 16
SSM_GROUPS = SSM_WIDTH // SSM_GROUP
SSM_STATE = 64
DT_MIN = 1e-3
DT_MAX = 1e-1
ATTN_HEADS = 8
HEAD_DIM = 64
ATTN_WIDTH = ATTN_HEADS * HEAD_DIM
MOBA_BLOCK = 256
MOBA_TOPK = 3
Q_CHUNK = 16
D_FF = 4 * D_MODEL
RMS_EPS = 1e-6
NEG_INF = -1e30
IN_COLS = SSM_WIDTH + 3 * ATTN_WIDTH + 2 * D_MODEL

kernel_name = "hybrid_s5_moba_gated_block"


def rms_norm(x, g):
    xf = x.astype(jnp.float32)
    var = jnp.mean(xf * xf, axis=-1, keepdims=True)
    return (xf * lax.rsqrt(var + RMS_EPS)).astype(x.dtype) * g


def _ssm_combine(left, right):
    ar1, ai1, br1, bi1 = left
    ar2, ai2, br2, bi2 = right
    ar = ar2 * ar1 - ai2 * ai1
    ai = ar2 * ai1 + ai2 * ar1
    br = ar2 * br1 - ai2 * bi1 + br2
    bi = ar2 * bi1 + ai2 * br1 + bi2
    return ar, ai, br, bi


def s5_mixer(u, lam_re, lam_im, log_dt, b_re, b_im, c_re, c_im, d_skip, w_glu, b_glu):
    f32 = jnp.float32
    bsz, seq, _ = u.shape
    uf = u.astype(f32)
    ug = uf.reshape(bsz, seq, SSM_GROUPS, SSM_GROUP)
    lr = lam_re.astype(f32)
    li = lam_im.astype(f32)
    dt = jnp.exp(log_dt.astype(f32))[:, None]
    mag = jnp.exp(lr * dt)
    ab_re = mag * jnp.cos(li * dt)
    ab_im = mag * jnp.sin(li * dt)
    nr = ab_re - 1.0
    ni = ab_im
    den = lr * lr + li * li
    coef_re = (nr * lr + ni * li) / den
    coef_im = (ni * lr - nr * li) / den
    br = b_re.astype(f32)
    bi = b_im.astype(f32)
    bb_re = coef_re[..., None] * br - coef_im[..., None] * bi
    bb_im = coef_re[..., None] * bi + coef_im[..., None] * br
    bu_re = jnp.einsum('blgh,gph->lbgp', ug, bb_re)
    bu_im = jnp.einsum('blgh,gph->lbgp', ug, bb_im)
    a_re = jnp.broadcast_to(ab_re[None, None], (seq, 1, SSM_GROUPS, SSM_STATE))
    a_im = jnp.broadcast_to(ab_im[None, None], (seq, 1, SSM_GROUPS, SSM_STATE))
    _, _, x_re, x_im = lax.associative_scan(_ssm_combine, (a_re, a_im, bu_re, bu_im), axis=0)
    y = (jnp.einsum('lbgp,ghp->blgh', x_re, c_re.astype(f32))
         - jnp.einsum('lbgp,ghp->blgh', x_im, c_im.astype(f32)))
    y = y.reshape(bsz, seq, SSM_WIDTH) + d_skip.astype(f32) * uf
    v = jax.nn.gelu(y)
    out = v * jax.nn.sigmoid(v @ w_glu.astype(f32) + b_glu.astype(f32))
    return out.astype(u.dtype)


def moba_attention(q, k, v):
    f32 = jnp.float32
    bsz, seq, nh, dh = q.shape
    nb = -(-seq // MOBA_BLOCK)
    lp = nb * MOBA_BLOCK
    pad = lp - seq
    def prep(t):
        t = jnp.transpose(t, (0, 2, 1, 3))
        return jnp.pad(t, ((0, 0), (0, 0), (0, pad), (0, 0)))
    q, k, v = prep(q), prep(k), prep(v)
    kb = k.reshape(bsz, nh, nb, MOBA_BLOCK, dh)
    vb = v.reshape(bsz, nh, nb, MOBA_BLOCK, dh)
    k_mean = jnp.mean(kb.astype(f32), axis=3)
    gate = jnp.einsum('bhtd,bhnd->bhtn', q.astype(f32), k_mean)
    pos = jnp.arange(lp, dtype=jnp.int32)
    qblk = pos // MOBA_BLOCK
    past = jnp.arange(nb, dtype=jnp.int32)[None, :] < qblk[:, None]
    gate = jnp.where(past, gate, NEG_INF)
    n_sel = min(MOBA_TOPK, max(nb - 1, 0))
    if n_sel > 0:
        _, sel = lax.top_k(gate, n_sel)
        sel = sel.astype(jnp.int32)
    else:
        sel = jnp.zeros(gate.shape[:-1] + (0,), jnp.int32)
    sel_valid = sel < qblk[:, None]
    own = jnp.broadcast_to(qblk[:, None], sel.shape[:-1] + (1,))
    idx = jnp.concatenate([sel, own], axis=-1)
    valid = jnp.concatenate([sel_valid, jnp.ones(own.shape, bool)], axis=-1)
    n_slots = idx.shape[-1]
    bh = bsz * nh
    nc = lp // Q_CHUNK
    def to_chunks(t):
        t = t.reshape((bh, nc, Q_CHUNK) + t.shape[3:])
        return jnp.moveaxis(t, 1, 0)
    q_ch = to_chunks(q)
    idx_ch = to_chunks(idx)
    val_ch = to_chunks(valid)
    pos_ch = pos.reshape(nc, Q_CHUNK)
    kb_f = kb.reshape(bh, nb, MOBA_BLOCK, dh)
    vb_f = vb.reshape(bh, nb, MOBA_BLOCK, dh)
    scale = 1.0 / math.sqrt(dh)
    offs = jnp.arange(MOBA_BLOCK, dtype=jnp.int32)
    gather = jax.vmap(lambda blocks, ii: blocks[ii])

    def chunk_fn(args):
        qc, ic, vc, pc = args
        kg = gather(kb_f, ic)
        vg = gather(vb_f, ic)
        s = jnp.einsum('nqd,nqjsd->nqjs', qc, kg).astype(f32) * scale
        key_pos = ic[..., None] * MOBA_BLOCK + offs
        mask = vc[..., None] & (key_pos <= pc[None, :, None, None])
        s = jnp.where(mask, s, NEG_INF)
        p = jax.nn.softmax(s.reshape(bh, Q_CHUNK, n_slots * MOBA_BLOCK), axis=-1)
        p = p.reshape(s.shape).astype(vg.dtype)
        return jnp.einsum('nqjs,nqjsd->nqd', p, vg)

    out = lax.map(chunk_fn, (q_ch, idx_ch, val_ch, pos_ch))
    out = jnp.moveaxis(out, 0, 1).reshape(bsz, nh, lp, dh)[:, :, :seq]
    return jnp.transpose(out, (0, 2, 1, 3)).reshape(bsz, seq, nh * dh)


def setup_inputs(seed: int = 0) -> dict:
    key = jax.random.key(seed)
    ks = jax.random.split(key, 24)
    f32 = jnp.float32
    def nrm(k, shape, scale):
        return jax.random.normal(k, shape, f32) * scale
    def gain(k):
        return 1.0 + 0.01 * jax.random.normal(k, (DEPTH, D_MODEL), f32)
    x = jax.random.normal(ks[0], (BATCH, SEQ, D_MODEL), f32)
    lam_re = -0.5 + 0.01 * jax.random.normal(ks[1], (DEPTH, SSM_GROUPS, SSM_STATE), f32)
    lam_im = (math.pi * jnp.arange(SSM_STATE, dtype=f32))[None, None, :] \
        + 0.01 * jax.random.normal(ks[2], (DEPTH, SSM_GROUPS, SSM_STATE), f32)
    log_dt = jax.random.uniform(ks[3], (DEPTH, SSM_GROUPS), f32,
                                minval=math.log(DT_MIN), maxval=math.log(DT_MAX))
    return {
        "x": x,
        "g_pre_mix": gain(ks[4]),
        "w_in": nrm(ks[5], (DEPTH, D_MODEL, IN_COLS), D_MODEL ** -0.5),
        "lam_re": lam_re,
        "lam_im": lam_im,
        "log_dt": log_dt,
        "b_re": nrm(ks[6], (DEPTH, SSM_GROUPS, SSM_STATE, SSM_GROUP), (2 * SSM_GROUP) ** -0.5),
        "b_im": nrm(ks[7], (DEPTH, SSM_GROUPS, SSM_STATE, SSM_GROUP), (2 * SSM_GROUP) ** -0.5),
        "c_re": nrm(ks[8], (DEPTH, SSM_GROUPS, SSM_GROUP, SSM_STATE), (2 * SSM_STATE) ** -0.5),
        "c_im": nrm(ks[9], (DEPTH, SSM_GROUPS, SSM_GROUP, SSM_STATE), (2 * SSM_STATE) ** -0.5),
        "d_skip": nrm(ks[10], (DEPTH, SSM_WIDTH), 1.0),
        "w_glu": nrm(ks[11], (DEPTH, SSM_WIDTH, SSM_WIDTH), SSM_WIDTH ** -0.5),
        "b_glu": nrm(ks[12], (DEPTH, SSM_WIDTH), 0.01),
        "w_branch_a": nrm(ks[13], (DEPTH, SSM_WIDTH, D_MODEL), SSM_WIDTH ** -0.5),
        "w_branch_b": nrm(ks[14], (DEPTH, ATTN_WIDTH, D_MODEL), ATTN_WIDTH ** -0.5),
        "w_out": nrm(ks[15], (DEPTH, D_MODEL, D_MODEL), D_MODEL ** -0.5),
        "g_post_mix": gain(ks[16]),
        "g_pre_ffn": gain(ks[17]),
        "w_ff1": nrm(ks[18], (DEPTH, D_MODEL, D_FF), D_MODEL ** -0.5),
        "w_ff2": nrm(ks[19], (DEPTH, D_FF, D_MODEL), D_FF ** -0.5),
        "g_post_ffn": gain(ks[20]),
    }


def reference(x, g_pre_mix, w_in, lam_re, lam_im, log_dt, b_re, b_im, c_re, c_im, d_skip,
              w_glu, b_glu, w_branch_a, w_branch_b, w_out, g_post_mix, g_pre_ffn,
              w_ff1, w_ff2, g_post_ffn):
    bsz, seq, _ = x.shape
    cuts = [SSM_WIDTH,
            SSM_WIDTH + ATTN_WIDTH,
            SSM_WIDTH + 2 * ATTN_WIDTH,
            SSM_WIDTH + 3 * ATTN_WIDTH,
            SSM_WIDTH + 3 * ATTN_WIDTH + D_MODEL]
    h = x
    for l in range(DEPTH):
        u = rms_norm(h, g_pre_mix[l])
        z = u @ w_in[l]
        z_ssm, z_q, z_k, z_v, z_ga, z_gb = jnp.split(z, cuts, axis=-1)
        y_a = s5_mixer(z_ssm, lam_re[l], lam_im[l], log_dt[l], b_re[l], b_im[l],
                       c_re[l], c_im[l], d_skip[l], w_glu[l], b_glu[l])
        heads = (bsz, seq, ATTN_HEADS, HEAD_DIM)
        y_b = moba_attention(z_q.reshape(heads), z_k.reshape(heads), z_v.reshape(heads))
        merged = (jax.nn.sigmoid(z_ga) * (y_a @ w_branch_a[l])
                  + jax.nn.sigmoid(z_gb) * (y_b @ w_branch_b[l]))
        mix = merged @ w_out[l]
        h = h + rms_norm(mix, g_post_mix[l])
        f = rms_norm(h, g_pre_ffn[l])
        f = jnp.square(jax.nn.relu(f @ w_ff1[l])) @ w_ff2[l]
        h = h + rms_norm(f, g_post_ffn[l])
    return h
```

```python
import functools

import jax
import jax.numpy as jnp
from jax import lax
from jax.experimental import pallas as pl
from jax.experimental.pallas import tpu as pltpu

F32 = jnp.float32
BF16 = jnp.bfloat16

RMS_EPS = 1e-6
NEG_INF = -1e30
TAKEN = -3e38

SSM_GROUP = 16
SSM_STATE = 64
ATTN_HEADS = 8
HEAD_DIM = 64
MOBA_BLOCK = 256
MOBA_TOPK = 3

SUBLANES = 8
LANES = 128
VMEM_LIMIT_BYTES = 56 * 1024 * 1024

TOKEN_TILE = 512
S5_TIME_CHUNK = 64
S5_COL_GROUP = 512
FFN_CHUNK = 1024

_NT = (((1,), (1,)), ((), ()))


def _params(sem):
    return pltpu.CompilerParams(dimension_semantics=sem, vmem_limit_bytes=VMEM_LIMIT_BYTES)


def _resident(shape):
    zeros = (0,) * len(shape)
    return pl.BlockSpec(shape, lambda *_: zeros, pipeline_mode=pl.Buffered(1))


def _rms(x, g):
    var = jnp.mean(x * x, axis=-1, keepdims=True)
    return (x * lax.rsqrt(var + RMS_EPS)) * g


def _inproj_body(x_ref, g_ref, w_ref, *out_refs):
    ub = _rms(x_ref[...], g_ref[...]).astype(BF16)
    col = 0
    for ref in out_refs:
        n = ref.shape[-1]
        ref[...] = jnp.dot(ub, w_ref[:, col:col + n], preferred_element_type=F32).astype(ref.dtype)
        col += n


def _inproj(x2, g, w, widths):
    t, d = x2.shape
    tm = min(TOKEN_TILE, t)
    return pl.pallas_call(
        _inproj_body,
        grid=(t // tm,),
        in_specs=[pl.BlockSpec((tm, d), lambda i: (i, 0)), _resident(g.shape), _resident(w.shape)],
        out_specs=[pl.BlockSpec((tm, n), lambda i: (i, 0)) for n in widths],
        out_shape=[jax.ShapeDtypeStruct((t, n), BF16) for n in widths],
        compiler_params=_params(("parallel",)),
        name="inproj",
    )(x2, g, w)


def _s5_prep_body(lr_ref, li_ref, ldt_ref, br_ref, bi_ref, ar_ref, ai_ref, bbr_ref, bbi_ref):
    lr = lr_ref[...]
    li = li_ref[...]
    dt = jnp.exp(ldt_ref[...])
    mag = jnp.exp(lr * dt)
    ab_re = mag * jnp.cos(li * dt)
    ab_im = mag * jnp.sin(li * dt)
    nr = ab_re - 1.0
    ni = ab_im
    den = lr * lr + li * li
    coef_re = (nr * lr + ni * li) / den
    coef_im = (ni * lr - nr * li) / den
    br = br_ref[...]
    bi = bi_ref[...]
    ar_ref[...] = ab_re
    ai_ref[...] = ab_im
    bbr_ref[...] = coef_re * br - coef_im * bi
    bbi_ref[...] = coef_re * bi + coef_im * br


def _s5_prep(lam_re, lam_im, log_dt, b_re, b_im):
    g, p = lam_re.shape
    h = b_re.shape[-1]
    lr = lam_re.reshape(g, 1, p)
    li = lam_im.reshape(g, 1, p)
    ldt = jnp.broadcast_to(log_dt.reshape(g, 1, 1), (g, 1, p))
    br = jnp.transpose(b_re, (0, 2, 1))
    bi = jnp.transpose(b_im, (0, 2, 1))
    small = jax.ShapeDtypeStruct((g, 1, p), F32)
    big = jax.ShapeDtypeStruct((g, h, p), F32)
    return pl.pallas_call(_s5_prep_body, out_shape=[small, small, big, big], name="s5_prep")(lr, li, ldt, br, bi)


def _block_diag(m):
    g, r, c = m.shape
    idx = jnp.arange(g)
    return jnp.zeros((g, r, g, c), m.dtype).at[idx, :, idx, :].set(m).reshape(g * r, g * c)


def _s5_body(u_ref, b_ref, ar_ref, ai_ref, cre_ref, cim_ref, dsk_ref, wg_ref, bg_ref, y_ref,
             bu_ref, st_ref, *, steps):
    half_in = u_ref.shape[1] // 2
    half_st = st_ref.shape[1] // 4

    @pl.when(pl.program_id(0) == 0)
    def _():
        st_ref[...] = jnp.zeros_like(st_ref)

    for c in range(2):
        bu_ref[:, 2 * c * half_st:2 * (c + 1) * half_st] = jnp.dot(
            u_ref[:, c * half_in:(c + 1) * half_in], b_ref[c], preferred_element_type=F32)

    w = S5_COL_GROUP
    for c in range(2):
        for s in range(half_st // w):
            re0 = 2 * c * half_st + s * w
            im0 = re0 + half_st
            a0 = c * half_st + s * w
            ar = ar_ref[:, a0:a0 + w]
            ai = ai_ref[:, a0:a0 + w]

            def step(t, carry, re0=re0, im0=im0, ar=ar, ai=ai):
                xr, xi = carry
                row = pl.multiple_of(t * SUBLANES, SUBLANES)
                nxr = ar * xr - ai * xi + bu_ref[pl.ds(row, SUBLANES), re0:re0 + w]
                nxi = ar * xi + ai * xr + bu_ref[pl.ds(row, SUBLANES), im0:im0 + w]
                bu_ref[pl.ds(row, SUBLANES), re0:re0 + w] = nxr
                bu_ref[pl.ds(row, SUBLANES), im0:im0 + w] = nxi
                return nxr, nxi

            xr, xi = lax.fori_loop(0, steps, step,
                                   (st_ref[:, re0:re0 + w], st_ref[:, im0:im0 + w]), unroll=8)
            st_ref[:, re0:re0 + w] = xr
            st_ref[:, im0:im0 + w] = xi

    ys = []
    for c in range(2):
        xre = bu_ref[:, 2 * c * half_st:(2 * c + 1) * half_st].astype(BF16)
        xim = bu_ref[:, (2 * c + 1) * half_st:(2 * c + 2) * half_st].astype(BF16)
        ys.append(jnp.dot(xre, cre_ref[c], preferred_element_type=F32)
                  - jnp.dot(xim, cim_ref[c], preferred_element_type=F32))
    y = jnp.concatenate(ys, axis=1) + dsk_ref[...] * u_ref[...].astype(F32)
    v = jax.nn.gelu(y)
    gate = jnp.dot(v.astype(BF16), wg_ref[...], preferred_element_type=F32) + bg_ref[...]
    y_ref[...] = (v * jax.nn.sigmoid(gate)).astype(y_ref.dtype)


def _s5_mixer(u_tm, bsz, bmat, ar, ai, cre, cim, d_skip, w_glu, b_glu):
    rows, width = u_tm.shape
    seq = rows // bsz
    steps = min(S5_TIME_CHUNK, seq)
    r = steps * bsz
    n_state = ar.shape[1] * 2
    return pl.pallas_call(
        functools.partial(_s5_body, steps=steps),
        grid=(seq // steps,),
        in_specs=[pl.BlockSpec((r, width), lambda i: (i, 0)),
                  _resident(bmat.shape), _resident(ar.shape), _resident(ai.shape),
                  _resident(cre.shape), _resident(cim.shape), _resident(d_skip.shape),
                  _resident(w_glu.shape), _resident(b_glu.shape)],
        out_specs=pl.BlockSpec((r, width), lambda i: (i, 0)),
        out_shape=jax.ShapeDtypeStruct((rows, width), BF16),
        scratch_shapes=[pltpu.VMEM((r, n_state), F32), pltpu.VMEM((bsz, n_state), F32)],
        compiler_params=_params(("arbitrary",)),
        name="s5_mixer",
    )(u_tm, bmat, ar, ai, cre, cim, d_skip, w_glu, b_glu)


def _moba_body(q_ref, k_ref, v_ref, o_ref, vt_ref, km_ref, bias_ref, s_ref, *, nb):
    blk = MOBA_BLOCK
    i = pl.program_id(2)

    @pl.when(i == 0)
    def _():
        for j in range(nb):
            vt_ref[j] = v_ref[0, j * blk:(j + 1) * blk, :].astype(F32).T.astype(BF16)
        kf = k_ref[0].astype(F32).reshape(nb, blk, 2 * HEAD_DIM)
        km_ref[...] = jnp.mean(kf, axis=1).astype(BF16)

    q2 = q_ref[0]
    lane = lax.broadcasted_iota(jnp.int32, q2.shape, 1)
    nidx = lax.broadcasted_iota(jnp.int32, (nb, blk), 0)
    km = km_ref[...]
    qs = []
    for h in range(2):
        head = (lane < HEAD_DIM) if h == 0 else (lane >= HEAD_DIM)
        qh = jnp.where(head, q2, jnp.zeros_like(q2))
        g = lax.dot_general(km, qh, _NT, preferred_element_type=F32)
        g = jnp.where(nidx < i, g, NEG_INF)
        picked = jnp.zeros((nb, blk), jnp.int32)
        for _ in range(min(MOBA_TOPK, nb - 1)):
            top = jnp.max(g, axis=0, keepdims=True)
            first = jnp.min(jnp.where(g == top, nidx, nb), axis=0, keepdims=True)
            hit = nidx == first
            picked = jnp.where(hit, 1, picked)
            g = jnp.where(hit, TAKEN, g)
        keep = (picked == 1) & (nidx < i)
        bias_ref[h] = jnp.where(keep, 0.0, NEG_INF)
        qs.append(qh * jnp.asarray(HEAD_DIM ** -0.5, BF16))

    def fold(x, op):
        return op(x.reshape(blk // SUBLANES, SUBLANES, blk), axis=0)

    def scores(j):
        kj = k_ref[0, pl.ds(pl.multiple_of(j * blk, blk), blk), :]
        return [lax.dot_general(kj, qs[h], _NT, preferred_element_type=F32) for h in range(2)]

    def past(j, ms):
        row = pl.multiple_of(j * blk, blk)
        out = []
        for h, s in enumerate(scores(j)):
            s = s + bias_ref[h, pl.ds(j, 1), :]
            s_ref[h, pl.ds(row, blk), :] = s
            out.append(jnp.maximum(ms[h], fold(s, jnp.max)))
        return tuple(out)

    init = tuple(jnp.full((SUBLANES, blk), NEG_INF, F32) for _ in range(2))
    ms = lax.fori_loop(0, i, past, init)

    kpos = lax.broadcasted_iota(jnp.int32, (blk, blk), 0)
    qpos = lax.broadcasted_iota(jnp.int32, (blk, blk), 1)
    own = pl.multiple_of(i * blk, blk)
    mx = []
    for h, s in enumerate(scores(i)):
        s = jnp.where(kpos <= qpos, s, NEG_INF)
        s_ref[h, pl.ds(own, blk), :] = s
        mx.append(jnp.max(jnp.maximum(ms[h], fold(s, jnp.max)), axis=0, keepdims=True))

    def weigh(j, carry):
        row = pl.multiple_of(j * blk, blk)
        out = []
        for h in range(2):
            l, acc = carry[h]
            p = jnp.exp(s_ref[h, pl.ds(row, blk), :] - mx[h])
            vt = vt_ref[j, h * HEAD_DIM:(h + 1) * HEAD_DIM, :]
            out.append((l + fold(p, jnp.sum),
                        acc + jnp.dot(vt, p.astype(BF16), preferred_element_type=F32)))
        return tuple(out)

    zero = (jnp.zeros((SUBLANES, blk), F32), jnp.zeros((HEAD_DIM, blk), F32))
    res = lax.fori_loop(0, i + 1, weigh, (zero, zero))
    o = jnp.concatenate([acc / jnp.sum(l, axis=0, keepdims=True) for l, acc in res], axis=0)
    o_ref[0] = o.T.astype(o_ref.dtype)


def _moba(q, k, v):
    bsz, seq, width = q.shape
    blk = MOBA_BLOCK
    nb = seq // blk
    pair = 2 * HEAD_DIM
    qspec = pl.BlockSpec((1, blk, pair), lambda b, hp, i: (b, i, hp))
    kvspec = pl.BlockSpec((1, seq, pair), lambda b, hp, i: (b, 0, hp))
    return pl.pallas_call(
        functools.partial(_moba_body, nb=nb),
        grid=(bsz, width // pair, nb),
        in_specs=[qspec, kvspec, kvspec],
        out_specs=qspec,
        out_shape=jax.ShapeDtypeStruct((bsz, seq, width), BF16),
        scratch_shapes=[pltpu.VMEM((nb, pair, blk), BF16), pltpu.VMEM((nb, pair), BF16),
                        pltpu.VMEM((2, nb, blk), F32), pltpu.VMEM((2, seq, blk), F32)],
        compiler_params=_params(("parallel", "parallel", "arbitrary")),
        name="moba",
    )(q, k, v)


def _post_body(x_ref, ya_ref, yb_ref, ga_ref, gb_ref, wa_ref, wb_ref, wo_ref, gpm_ref, gpf_ref,
               w1_ref, w2_ref, gpo_ref, o_ref):
    a = jnp.dot(ya_ref[...], wa_ref[...], preferred_element_type=F32)
    b = jnp.dot(yb_ref[...], wb_ref[...], preferred_element_type=F32)
    merged = (jax.nn.sigmoid(ga_ref[...].astype(F32)) * a
              + jax.nn.sigmoid(gb_ref[...].astype(F32)) * b)
    mix = jnp.dot(merged.astype(BF16), wo_ref[...], preferred_element_type=F32)
    h = x_ref[...] + _rms(mix, gpm_ref[...])
    f = _rms(h, gpf_ref[...]).astype(BF16)
    d_ff = w1_ref.shape[1]
    ck = min(FFN_CHUNK, d_ff)
    acc = jnp.zeros(h.shape, F32)
    for c in range(d_ff // ck):
        t = jnp.dot(f, w1_ref[:, c * ck:(c + 1) * ck], preferred_element_type=F32)
        t = jnp.square(jnp.maximum(t, 0.0)).astype(BF16)
        acc = acc + jnp.dot(t, w2_ref[c * ck:(c + 1) * ck, :], preferred_element_type=F32)
    o_ref[...] = h + _rms(acc, gpo_ref[...])


def _post(x2, ya, yb, ga, gb, wa, wb, wo, gpm, gpf, w1, w2, gpo):
    t, d = x2.shape
    tm = min(TOKEN_TILE, t)

    def rows(a):
        return pl.BlockSpec((tm, a.shape[1]), lambda i: (i, 0))

    consts = (wa, wb, wo, gpm, gpf, w1, w2, gpo)
    return pl.pallas_call(
        _post_body,
        grid=(t // tm,),
        in_specs=[rows(a) for a in (x2, ya, yb, ga, gb)] + [_resident(c.shape) for c in consts],
        out_specs=rows(x2),
        out_shape=jax.ShapeDtypeStruct((t, d), x2.dtype),
        compiler_params=_params(("parallel",)),
        name="post",
    )(x2, ya, yb, ga, gb, *consts)


def _layer(h, g_pre_mix, w_in, lam_re, lam_im, log_dt, b_re, b_im, c_re, c_im, d_skip, w_glu, b_glu,
           w_branch_a, w_branch_b, w_out, g_post_mix, g_pre_ffn, w_ff1, w_ff2, g_post_ffn):
    bsz, seq, d = h.shape
    ssm_w = lam_re.shape[0] * SSM_GROUP
    attn_w = ATTN_HEADS * HEAD_DIM
    x2 = h.reshape(bsz * seq, d)

    zs, q, k, v, ga, gb = _inproj(x2, g_pre_mix.reshape(1, d), w_in.astype(BF16),
                                  (ssm_w, attn_w, attn_w, attn_w, d, d))

    ar, ai, bbr, bbi = _s5_prep(lam_re, lam_im, log_dt, b_re, b_im)
    g, p = lam_re.shape
    hg = g // 2
    n_half = hg * p

    def halves(m):
        return jnp.stack([_block_diag(m[:hg]), _block_diag(m[hg:])])

    bmat = jnp.concatenate([halves(bbr), halves(bbi)], axis=2).astype(BF16)
    cre = halves(jnp.transpose(c_re, (0, 2, 1))).astype(BF16)
    cim = halves(jnp.transpose(c_im, (0, 2, 1))).astype(BF16)
    ar8 = jnp.broadcast_to(ar.reshape(1, 2 * n_half), (bsz, 2 * n_half))
    ai8 = jnp.broadcast_to(ai.reshape(1, 2 * n_half), (bsz, 2 * n_half))

    u_tm = jnp.transpose(zs.reshape(bsz, seq, ssm_w), (1, 0, 2)).reshape(seq * bsz, ssm_w)
    ya_tm = _s5_mixer(u_tm, bsz, bmat, ar8, ai8, cre, cim, d_skip.reshape(1, ssm_w),
                      w_glu.astype(BF16), b_glu.reshape(1, ssm_w))
    ya = jnp.transpose(ya_tm.reshape(seq, bsz, ssm_w), (1, 0, 2)).reshape(bsz * seq, ssm_w)

    shape3 = (bsz, seq, attn_w)
    yb = _moba(q.reshape(shape3), k.reshape(shape3), v.reshape(shape3)).reshape(bsz * seq, attn_w)

    out = _post(x2, ya, yb, ga, gb, w_branch_a.astype(BF16), w_branch_b.astype(BF16),
                w_out.astype(BF16), g_post_mix.reshape(1, d), g_pre_ffn.reshape(1, d),
                w_ff1.astype(BF16), w_ff2.astype(BF16), g_post_ffn.reshape(1, d))
    return out.reshape(bsz, seq, d)


def kernel(x, g_pre_mix, w_in, lam_re, lam_im, log_dt, b_re, b_im, c_re, c_im, d_skip, w_glu, b_glu,
           w_branch_a, w_branch_b, w_out, g_post_mix, g_pre_ffn, w_ff1, w_ff2, g_post_ffn):
    per_layer = (g_pre_mix, w_in, lam_re, lam_im, log_dt, b_re, b_im, c_re, c_im, d_skip, w_glu, b_glu,
                 w_branch_a, w_branch_b, w_out, g_post_mix, g_pre_ffn, w_ff1, w_ff2, g_post_ffn)
    h = x
    for l in range(g_pre_mix.shape[0]):
        h = _layer(h, *(p[l] for p in per_layer))
    return h
```

```python
import functools

import jax
import jax.numpy as jnp
import numpy as np
from jax import lax
from jax.experimental import pallas as pl
from jax.experimental.pallas import tpu as pltpu

F32 = jnp.float32
BF16 = jnp.bfloat16

RMS_EPS = 1e-6
NEG_INF = -1e30
TAKEN = -3e38

SSM_GROUP = 16
SSM_STATE = 64
ATTN_HEADS = 8
HEAD_DIM = 64
MOBA_BLOCK = 256
MOBA_TOPK = 3

SUBLANES = 8
LANES = 128
VMEM_LIMIT_BYTES = 56 * 1024 * 1024

TOKEN_TILE = 512
S5_TIME_CHUNK = 64
S5_COL_GROUP = 512
FFN_CHUNK = 1024

_NT = (((1,), (1,)), ((), ()))


def _params(sem):
    return pltpu.CompilerParams(dimension_semantics=sem, vmem_limit_bytes=VMEM_LIMIT_BYTES)


def _resident(shape):
    zeros = (0,) * len(shape)
    return pl.BlockSpec(shape, lambda *_: zeros, pipeline_mode=pl.Buffered(1))


def _rms(x, g):
    var = jnp.mean(x * x, axis=-1, keepdims=True)
    return (x * lax.rsqrt(var + RMS_EPS)) * g


def _inproj_body(x_ref, g_ref, w_ref, *out_refs):
    ub = _rms(x_ref[...], g_ref[...]).astype(BF16)
    col = 0
    for ref in out_refs:
        n = ref.shape[-1]
        ref[...] = jnp.dot(ub, w_ref[:, col:col + n], preferred_element_type=F32).astype(ref.dtype)
        col += n


def _inproj(x2, g, w, widths):
    t, d = x2.shape
    tm = min(TOKEN_TILE, t)
    return pl.pallas_call(
        _inproj_body,
        grid=(t // tm,),
        in_specs=[pl.BlockSpec((tm, d), lambda i: (i, 0)), _resident(g.shape), _resident(w.shape)],
        out_specs=[pl.BlockSpec((tm, n), lambda i: (i, 0)) for n in widths],
        out_shape=[jax.ShapeDtypeStruct((t, n), BF16) for n in widths],
        compiler_params=_params(("parallel",)),
        name="inproj",
    )(x2, g, w)


def _s5_prep_body(lr_ref, li_ref, ldt_ref, br_ref, bi_ref, ar_ref, ai_ref, bbr_ref, bbi_ref):
    lr = lr_ref[...]
    li = li_ref[...]
    dt = jnp.exp(ldt_ref[...])
    mag = jnp.exp(lr * dt)
    ab_re = mag * jnp.cos(li * dt)
    ab_im = mag * jnp.sin(li * dt)
    nr = ab_re - 1.0
    ni = ab_im
    den = lr * lr + li * li
    coef_re = (nr * lr + ni * li) / den
    coef_im = (ni * lr - nr * li) / den
    br = br_ref[...]
    bi = bi_ref[...]
    ar_ref[...] = ab_re
    ai_ref[...] = ab_im
    bbr_ref[...] = coef_re * br - coef_im * bi
    bbi_ref[...] = coef_re * bi + coef_im * br


def _s5_prep(lam_re, lam_im, log_dt, b_re, b_im):
    g, p = lam_re.shape
    h = b_re.shape[-1]
    lr = lam_re.reshape(g, 1, p)
    li = lam_im.reshape(g, 1, p)
    ldt = jnp.broadcast_to(log_dt.reshape(g, 1, 1), (g, 1, p))
    br = jnp.transpose(b_re, (0, 2, 1))
    bi = jnp.transpose(b_im, (0, 2, 1))
    small = jax.ShapeDtypeStruct((g, 1, p), F32)
    big = jax.ShapeDtypeStruct((g, h, p), F32)
    return pl.pallas_call(_s5_prep_body, out_shape=[small, small, big, big], name="s5_prep")(lr, li, ldt, br, bi)


def _block_diag(m):
    g, r, c = m.shape
    idx = jnp.arange(g)
    return jnp.zeros((g, r, g, c), m.dtype).at[idx, :, idx, :].set(m).reshape(g * r, g * c)


def _s5_body(u_ref, b_ref, ar_ref, ai_ref, cre_ref, cim_ref, dsk_ref, wg_ref, bg_ref, y_ref,
             bu_ref, st_ref, *, steps):
    half_in = u_ref.shape[1] // 2
    half_st = st_ref.shape[1] // 4

    @pl.when(pl.program_id(0) == 0)
    def _():
        st_ref[...] = jnp.zeros_like(st_ref)

    for c in range(2):
        bu_ref[:, 2 * c * half_st:2 * (c + 1) * half_st] = jnp.dot(
            u_ref[:, c * half_in:(c + 1) * half_in], b_ref[c], preferred_element_type=F32)

    w = S5_COL_GROUP
    for c in range(2):
        for s in range(half_st // w):
            re0 = 2 * c * half_st + s * w
            im0 = re0 + half_st
            a0 = c * half_st + s * w
            ar = ar_ref[:, a0:a0 + w]
            ai = ai_ref[:, a0:a0 + w]

            def step(t, carry, re0=re0, im0=im0, ar=ar, ai=ai):
                xr, xi = carry
                row = pl.multiple_of(t * SUBLANES, SUBLANES)
                nxr = ar * xr - ai * xi + bu_ref[pl.ds(row, SUBLANES), re0:re0 + w]
                nxi = ar * xi + ai * xr + bu_ref[pl.ds(row, SUBLANES), im0:im0 + w]
                bu_ref[pl.ds(row, SUBLANES), re0:re0 + w] = nxr
                bu_ref[pl.ds(row, SUBLANES), im0:im0 + w] = nxi
                return nxr, nxi

            xr, xi = lax.fori_loop(0, steps, step,
                                   (st_ref[:, re0:re0 + w], st_ref[:, im0:im0 + w]), unroll=8)
            st_ref[:, re0:re0 + w] = xr
            st_ref[:, im0:im0 + w] = xi

    ys = []
    for c in range(2):
        xre = bu_ref[:, 2 * c * half_st:(2 * c + 1) * half_st].astype(BF16)
        xim = bu_ref[:, (2 * c + 1) * half_st:(2 * c + 2) * half_st].astype(BF16)
        ys.append(jnp.dot(xre, cre_ref[c], preferred_element_type=F32)
                  - jnp.dot(xim, cim_ref[c], preferred_element_type=F32))
    y = jnp.concatenate(ys, axis=1) + dsk_ref[...] * u_ref[...].astype(F32)
    v = jax.nn.gelu(y)
    gate = jnp.dot(v.astype(BF16), wg_ref[...], preferred_element_type=F32) + bg_ref[...]
    y_ref[...] = (v * jax.nn.sigmoid(gate)).astype(y_ref.dtype)


def _s5_mixer(u_tm, bsz, bmat, ar, ai, cre, cim, d_skip, w_glu, b_glu):
    rows, width = u_tm.shape
    seq = rows // bsz
    steps = min(S5_TIME_CHUNK, seq)
    r = steps * bsz
    n_state = ar.shape[1] * 2
    return pl.pallas_call(
        functools.partial(_s5_body, steps=steps),
        grid=(seq // steps,),
        in_specs=[pl.BlockSpec((r, width), lambda i: (i, 0)),
                  _resident(bmat.shape), _resident(ar.shape), _resident(ai.shape),
                  _resident(cre.shape), _resident(cim.shape), _resident(d_skip.shape),
                  _resident(w_glu.shape), _resident(b_glu.shape)],
        out_specs=pl.BlockSpec((r, width), lambda i: (i, 0)),
        out_shape=jax.ShapeDtypeStruct((rows, width), BF16),
        scratch_shapes=[pltpu.VMEM((r, n_state), F32), pltpu.VMEM((bsz, n_state), F32)],
        compiler_params=_params(("arbitrary",)),
        name="s5_mixer",
    )(u_tm, bmat, ar, ai, cre, cim, d_skip, w_glu, b_glu)


def _moba_tables(nb):
    items = []
    for i in range(nb):
        last = i // 2
        for jp in range(last + 1):
            items.append((i, jp, 0 if jp < last else (1 if i % 2 else 2)))
    n_iter = (len(items) + 1) // 2 + 2
    tab = np.zeros((6, 2 * n_iter), np.int32)
    for c in range(2 * n_iter):
        k, e = divmod(c, 2)

        def item(t):
            return items[t] if 0 <= t < len(items) else None

        one, two, three = item(2 * k + e), item(2 * (k - 1) + e), item(2 * (k - 2) + e)
        tab[0:3, c] = one if one else (0, 0, 0)
        tab[3, c] = two[0] if two else nb
        tab[4:6, c] = three[0:2] if three else (nb, 0)
    return tab, n_iter


def _moba_body(tab_ref, q_ref, k_ref, v_ref, o_ref, kx_ref, vt_ref, cb_ref, qx_ref,
               m_ref, l_ref, acc_ref, s_ring, cm_ring, p_ring, al_ring, *, nb, n_iter):
    blk = MOBA_BLOCK
    pair = 2 * HEAD_DIM
    seq = nb * blk
    two = 2 * blk

    @pl.when((pl.program_id(0) == 0) & (pl.program_id(1) == 0))
    def _():
        row = lax.broadcasted_iota(jnp.int32, (two, pair), 0)
        lane = lax.broadcasted_iota(jnp.int32, (two, pair), 1)
        for jp in range(nb // 2):
            block = jnp.where(row < blk, 2 * jp, 2 * jp + 1)
            kx_ref[jp, :, pair:] = jnp.where(lane == block, 1.0, 0.0).astype(BF16)
        kpos = lax.broadcasted_iota(jnp.int32, (blk, blk), 0)
        qpos = lax.broadcasted_iota(jnp.int32, (blk, blk), 1)
        tri = jnp.where(kpos <= qpos, 0.0, NEG_INF)
        zero = jnp.zeros((blk, blk), F32)
        cb_ref[0] = jnp.concatenate([zero, zero], axis=0)
        cb_ref[1] = jnp.concatenate([zero, tri], axis=0)
        cb_ref[2] = jnp.concatenate([tri, jnp.full((blk, blk), NEG_INF, F32)], axis=0)
        s_ring[...] = jnp.zeros_like(s_ring)
        cm_ring[...] = jnp.zeros_like(cm_ring)
        p_ring[...] = jnp.zeros_like(p_ring)
        al_ring[...] = jnp.zeros_like(al_ring)

    for jp in range(nb // 2):
        rows = slice(jp * two, (jp + 1) * two)
        kx_ref[jp, :, 0:pair] = k_ref[0, rows, :]
        vt_ref[jp] = v_ref[0, rows, :].astype(F32).T.astype(BF16)
    km = jnp.mean(k_ref[0].astype(F32).reshape(nb, blk, pair), axis=1).astype(BF16)

    q2 = q_ref[0]
    lane = lax.broadcasted_iota(jnp.int32, q2.shape, 1)
    nidx = lax.broadcasted_iota(jnp.int32, (nb, seq), 0)
    qblk = lax.shift_right_logical(lax.broadcasted_iota(jnp.int32, (nb, seq), 1), blk.bit_length() - 1)
    for h in range(2):
        head = (lane < HEAD_DIM) if h == 0 else (lane >= HEAD_DIM)
        qh = jnp.where(head, q2, jnp.zeros_like(q2))
        g = lax.dot_general(km, qh, _NT, preferred_element_type=F32)
        g = jnp.where(nidx < qblk, g, NEG_INF)
        picked = jnp.zeros((nb, seq), jnp.int32)
        for _ in range(min(MOBA_TOPK, nb - 1)):
            top = jnp.max(g, axis=0, keepdims=True)
            first = jnp.min(jnp.where(g == top, nidx, nb), axis=0, keepdims=True)
            hit = nidx == first
            picked = jnp.where(hit, 1, picked)
            g = jnp.where(hit, TAKEN, g)
        allow = ((picked == 1) & (nidx < qblk)) | (nidx == qblk)
        bias = jnp.where(allow, 0.0, NEG_INF)
        bias = jnp.concatenate([bias, jnp.zeros((pair - nb, seq), F32)], axis=0)
        qs = qh * jnp.asarray(HEAD_DIM ** -0.5, BF16)
        for i in range(nb):
            qx_ref[2 * i + h, :, 0:pair] = qs[i * blk:(i + 1) * blk, :]
            qx_ref[2 * i + h, :, pair:] = bias[:, i * blk:(i + 1) * blk].T.astype(BF16)

    m_ref[...] = jnp.full(m_ref.shape, NEG_INF, F32)
    l_ref[...] = jnp.zeros_like(l_ref)
    acc_ref[...] = jnp.zeros_like(acc_ref)

    def step(k, carry):
        for e in range(2):
            c = 2 * k + e
            i1, j1, var = tab_ref[0, c], tab_ref[1, c], tab_ref[2, c]
            i2, i3, j3 = tab_ref[3, c], tab_ref[4, c], tab_ref[5, c]
            for h in range(2):
                r = 2 * e + h
                a3 = 2 * i3 + h
                pv = jnp.dot(vt_ref[j3, h * HEAD_DIM:(h + 1) * HEAD_DIM, :], p_ring[r],
                             preferred_element_type=F32)
                acc_ref[a3] = al_ring[r] * acc_ref[a3] + pv
                a2 = 2 * i2 + h
                m_old = m_ref[a2]
                m_new = jnp.maximum(m_old, cm_ring[r])
                alpha = jnp.exp(m_old - m_new)
                p = jnp.exp(s_ring[r] - m_new)
                l_ref[a2] = alpha * l_ref[a2] + jnp.sum(p, axis=0, keepdims=True)
                m_ref[a2] = m_new
                p_ring[r] = p.astype(BF16)
                al_ring[r] = alpha
                s = lax.dot_general(kx_ref[j1], qx_ref[2 * i1 + h], _NT,
                                    preferred_element_type=F32) + cb_ref[var]
                s_ring[r] = s
                cm_ring[r] = jnp.max(s, axis=0, keepdims=True)
        return carry

    lax.fori_loop(0, n_iter, step, 0)

    for i in range(nb):
        o = jnp.concatenate([acc_ref[2 * i + h] / l_ref[2 * i + h] for h in range(2)], axis=0)
        o_ref[0, i * blk:(i + 1) * blk, :] = o.T.astype(o_ref.dtype)


def _moba(q, k, v):
    bsz, seq, width = q.shape
    blk = MOBA_BLOCK
    nb = seq // blk
    pair = 2 * HEAD_DIM
    tab, n_iter = _moba_tables(nb)
    spec = pl.BlockSpec((1, seq, pair), lambda b, hp, tab_ref: (b, 0, hp))
    n_state = 2 * (nb + 1)
    return pl.pallas_call(
        functools.partial(_moba_body, nb=nb, n_iter=n_iter),
        grid_spec=pltpu.PrefetchScalarGridSpec(
            num_scalar_prefetch=1,
            grid=(bsz, width // pair),
            in_specs=[spec, spec, spec],
            out_specs=spec,
            scratch_shapes=[
                pltpu.VMEM((nb // 2, 2 * blk, 2 * pair), BF16),
                pltpu.VMEM((nb // 2, pair, 2 * blk), BF16),
                pltpu.VMEM((3, 2 * blk, blk), F32),
                pltpu.VMEM((2 * nb, blk, 2 * pair), BF16),
                pltpu.VMEM((n_state, 1, blk), F32),
                pltpu.VMEM((n_state, 1, blk), F32),
                pltpu.VMEM((n_state, HEAD_DIM, blk), F32),
                pltpu.VMEM((4, 2 * blk, blk), F32),
                pltpu.VMEM((4, 1, blk), F32),
                pltpu.VMEM((4, 2 * blk, blk), BF16),
                pltpu.VMEM((4, 1, blk), F32),
            ]),
        out_shape=jax.ShapeDtypeStruct((bsz, seq, width), BF16),
        compiler_params=_params(("arbitrary", "arbitrary")),
        name="moba",
    )(jnp.asarray(tab), q, k, v)


def _post_body(x_ref, ya_ref, yb_ref, ga_ref, gb_ref, wa_ref, wb_ref, wo_ref, gpm_ref, gpf_ref,
               w1_ref, w2_ref, gpo_ref, o_ref):
    a = jnp.dot(ya_ref[...], wa_ref[...], preferred_element_type=F32)
    b = jnp.dot(yb_ref[...], wb_ref[...], preferred_element_type=F32)
    merged = (jax.nn.sigmoid(ga_ref[...].astype(F32)) * a
              + jax.nn.sigmoid(gb_ref[...].astype(F32)) * b)
    mix = jnp.dot(merged.astype(BF16), wo_ref[...], preferred_element_type=F32)
    h = x_ref[...] + _rms(mix, gpm_ref[...])
    f = _rms(h, gpf_ref[...]).astype(BF16)
    d_ff = w1_ref.shape[1]
    ck = min(FFN_CHUNK, d_ff)
    acc = jnp.zeros(h.shape, F32)
    for c in range(d_ff // ck):
        t = jnp.dot(f, w1_ref[:, c * ck:(c + 1) * ck], preferred_element_type=F32)
        t = jnp.square(jnp.maximum(t, 0.0)).astype(BF16)
        acc = acc + jnp.dot(t, w2_ref[c * ck:(c + 1) * ck, :], preferred_element_type=F32)
    o_ref[...] = h + _rms(acc, gpo_ref[...])


def _post(x2, ya, yb, ga, gb, wa, wb, wo, gpm, gpf, w1, w2, gpo):
    t, d = x2.shape
    tm = min(TOKEN_TILE, t)

    def rows(a):
        return pl.BlockSpec((tm, a.shape[1]), lambda i: (i, 0))

    consts = (wa, wb, wo, gpm, gpf, w1, w2, gpo)
    return pl.pallas_call(
        _post_body,
        grid=(t // tm,),
        in_specs=[rows(a) for a in (x2, ya, yb, ga, gb)] + [_resident(c.shape) for c in consts],
        out_specs=rows(x2),
        out_shape=jax.ShapeDtypeStruct((t, d), x2.dtype),
        compiler_params=_params(("parallel",)),
        name="post",
    )(x2, ya, yb, ga, gb, *consts)


def _layer(h, g_pre_mix, w_in, lam_re, lam_im, log_dt, b_re, b_im, c_re, c_im, d_skip, w_glu, b_glu,
           w_branch_a, w_branch_b, w_out, g_post_mix, g_pre_ffn, w_ff1, w_ff2, g_post_ffn):
    bsz, seq, d = h.shape
    ssm_w = lam_re.shape[0] * SSM_GROUP
    attn_w = ATTN_HEADS * HEAD_DIM
    x2 = h.reshape(bsz * seq, d)

    zs, q, k, v, ga, gb = _inproj(x2, g_pre_mix.reshape(1, d), w_in.astype(BF16),
                                  (ssm_w, attn_w, attn_w, attn_w, d, d))

    ar, ai, bbr, bbi = _s5_prep(lam_re, lam_im, log_dt, b_re, b_im)
    g, p = lam_re.shape
    hg = g // 2
    n_half = hg * p

    def halves(m):
        return jnp.stack([_block_diag(m[:hg]), _block_diag(m[hg:])])

    bmat = jnp.concatenate([halves(bbr), halves(bbi)], axis=2).astype(BF16)
    cre = halves(jnp.transpose(c_re, (0, 2, 1))).astype(BF16)
    cim = halves(jnp.transpose(c_im, (0, 2, 1))).astype(BF16)
    ar8 = jnp.broadcast_to(ar.reshape(1, 2 * n_half), (bsz, 2 * n_half))
    ai8 = jnp.broadcast_to(ai.reshape(1, 2 * n_half), (bsz, 2 * n_half))

    u_tm = jnp.transpose(zs.reshape(bsz, seq, ssm_w), (1, 0, 2)).reshape(seq * bsz, ssm_w)
    ya_tm = _s5_mixer(u_tm, bsz, bmat, ar8, ai8, cre, cim, d_skip.reshape(1, ssm_w),
                      w_glu.astype(BF16), b_glu.reshape(1, ssm_w))
    ya = jnp.transpose(ya_tm.reshape(seq, bsz, ssm_w), (1, 0, 2)).reshape(bsz * seq, ssm_w)

    shape3 = (bsz, seq, attn_w)
    yb = _moba(q.reshape(shape3), k.reshape(shape3), v.reshape(shape3)).reshape(bsz * seq, attn_w)

    out = _post(x2, ya, yb, ga, gb, w_branch_a.astype(BF16), w_branch_b.astype(BF16),
                w_out.astype(BF16), g_post_mix.reshape(1, d), g_pre_ffn.reshape(1, d),
                w_ff1.astype(BF16), w_ff2.astype(BF16), g_post_ffn.reshape(1, d))
    return out.reshape(bsz, seq, d)


def kernel(x, g_pre_mix, w_in, lam_re, lam_im, log_dt, b_re, b_im, c_re, c_im, d_skip, w_glu, b_glu,
           w_branch_a, w_branch_b, w_out, g_post_mix, g_pre_ffn, w_ff1, w_ff2, g_post_ffn):
    per_layer = (g_pre_mix, w_in, lam_re, lam_im, log_dt, b_re, b_im, c_re, c_im, d_skip, w_glu, b_glu,
                 w_branch_a, w_branch_b, w_out, g_post_mix, g_pre_ffn, w_ff1, w_ff2, g_post_ffn)
    h = x
    for l in range(g_pre_mix.shape[0]):
        h = _layer(h, *(p[l] for p in per_layer))
    return h
```

```python
import functools

import jax
import jax.numpy as jnp
import numpy as np
from jax import lax
from jax.experimental import pallas as pl
from jax.experimental.pallas import tpu as pltpu

F32 = jnp.float32
BF16 = jnp.bfloat16

RMS_EPS = 1e-6
NEG_INF = -1e30
TAKEN = -3e38

SSM_GROUP = 16
SSM_STATE = 64
ATTN_HEADS = 8
HEAD_DIM = 64
MOBA_BLOCK = 256
MOBA_TOPK = 3

LOG2_E = 1.4426950408889634

SUBLANES = 8
BF16_SUBLANES = 16
LANES = 128
VMEM_LIMIT_BYTES = 56 * 1024 * 1024

TOKEN_TILE = 512
S5_TIME_CHUNK = 64
S5_COL_GROUP = 512
FFN_CHUNK = 1024

_NT = (((1,), (1,)), ((), ()))


def _params(sem):
    return pltpu.CompilerParams(dimension_semantics=sem, vmem_limit_bytes=VMEM_LIMIT_BYTES)


def _resident(shape):
    zeros = (0,) * len(shape)
    return pl.BlockSpec(shape, lambda *_: zeros, pipeline_mode=pl.Buffered(1))


def _rms(x, g):
    var = jnp.mean(x * x, axis=-1, keepdims=True)
    return (x * lax.rsqrt(var + RMS_EPS)) * g


def _inproj_body(x_ref, g_ref, w_ref, *out_refs):
    ub = _rms(x_ref[...], g_ref[...]).astype(BF16)
    col = 0
    for ref in out_refs:
        n = ref.shape[-1]
        ref[...] = jnp.dot(ub, w_ref[:, col:col + n], preferred_element_type=F32).astype(ref.dtype)
        col += n


def _inproj(x2, g, w, widths):
    t, d = x2.shape
    tm = min(TOKEN_TILE, t)
    return pl.pallas_call(
        _inproj_body,
        grid=(t // tm,),
        in_specs=[pl.BlockSpec((tm, d), lambda i: (i, 0)), _resident(g.shape), _resident(w.shape)],
        out_specs=[pl.BlockSpec((tm, n), lambda i: (i, 0)) for n in widths],
        out_shape=[jax.ShapeDtypeStruct((t, n), BF16) for n in widths],
        compiler_params=_params(("parallel",)),
        name="inproj",
    )(x2, g, w)


def _s5_prep_body(lr_ref, li_ref, ldt_ref, br_ref, bi_ref, ar_ref, ai_ref, bbr_ref, bbi_ref):
    lr = lr_ref[...]
    li = li_ref[...]
    dt = jnp.exp(ldt_ref[...])
    mag = jnp.exp(lr * dt)
    ab_re = mag * jnp.cos(li * dt)
    ab_im = mag * jnp.sin(li * dt)
    nr = ab_re - 1.0
    ni = ab_im
    den = lr * lr + li * li
    coef_re = (nr * lr + ni * li) / den
    coef_im = (ni * lr - nr * li) / den
    br = br_ref[...]
    bi = bi_ref[...]
    ar_ref[...] = ab_re
    ai_ref[...] = ab_im
    bbr_ref[...] = coef_re * br - coef_im * bi
    bbi_ref[...] = coef_re * bi + coef_im * br


def _s5_prep(lam_re, lam_im, log_dt, b_re, b_im):
    g, p = lam_re.shape
    h = b_re.shape[-1]
    lr = lam_re.reshape(g, 1, p)
    li = lam_im.reshape(g, 1, p)
    ldt = jnp.broadcast_to(log_dt.reshape(g, 1, 1), (g, 1, p))
    br = jnp.transpose(b_re, (0, 2, 1))
    bi = jnp.transpose(b_im, (0, 2, 1))
    small = jax.ShapeDtypeStruct((g, 1, p), F32)
    big = jax.ShapeDtypeStruct((g, h, p), F32)
    return pl.pallas_call(_s5_prep_body, out_shape=[small, small, big, big], name="s5_prep")(lr, li, ldt, br, bi)


def _block_diag(m):
    g, r, c = m.shape
    idx = jnp.arange(g)
    return jnp.zeros((g, r, g, c), m.dtype).at[idx, :, idx, :].set(m).reshape(g * r, g * c)


def _s5_body(u_ref, b_ref, ar_ref, ai_ref, cre_ref, cim_ref, dsk_ref, wg_ref, bg_ref, y_ref,
             bu_ref, st_ref, *, steps):
    half_in = u_ref.shape[1] // 2
    half_st = st_ref.shape[1] // 4

    @pl.when(pl.program_id(0) == 0)
    def _():
        st_ref[...] = jnp.zeros_like(st_ref)

    for c in range(2):
        bu_ref[:, 2 * c * half_st:2 * (c + 1) * half_st] = jnp.dot(
            u_ref[:, c * half_in:(c + 1) * half_in], b_ref[c], preferred_element_type=F32)

    w = S5_COL_GROUP
    for c in range(2):
        for s in range(half_st // w):
            re0 = 2 * c * half_st + s * w
            im0 = re0 + half_st
            a0 = c * half_st + s * w
            ar = ar_ref[:, a0:a0 + w]
            ai = ai_ref[:, a0:a0 + w]

            def step(t, carry, re0=re0, im0=im0, ar=ar, ai=ai):
                xr, xi = carry
                row = pl.multiple_of(t * SUBLANES, SUBLANES)
                nxr = ar * xr - ai * xi + bu_ref[pl.ds(row, SUBLANES), re0:re0 + w]
                nxi = ar * xi + ai * xr + bu_ref[pl.ds(row, SUBLANES), im0:im0 + w]
                bu_ref[pl.ds(row, SUBLANES), re0:re0 + w] = nxr
                bu_ref[pl.ds(row, SUBLANES), im0:im0 + w] = nxi
                return nxr, nxi

            xr, xi = lax.fori_loop(0, steps, step,
                                   (st_ref[:, re0:re0 + w], st_ref[:, im0:im0 + w]), unroll=8)
            st_ref[:, re0:re0 + w] = xr
            st_ref[:, im0:im0 + w] = xi

    ys = []
    for c in range(2):
        xre = bu_ref[:, 2 * c * half_st:(2 * c + 1) * half_st].astype(BF16)
        xim = bu_ref[:, (2 * c + 1) * half_st:(2 * c + 2) * half_st].astype(BF16)
        ys.append(jnp.dot(xre, cre_ref[c], preferred_element_type=F32)
                  - jnp.dot(xim, cim_ref[c], preferred_element_type=F32))
    y = jnp.concatenate(ys, axis=1) + dsk_ref[...] * u_ref[...].astype(F32)
    v = jax.nn.gelu(y)
    gate = jnp.dot(v.astype(BF16), wg_ref[...], preferred_element_type=F32) + bg_ref[...]
    y_ref[...] = (v * jax.nn.sigmoid(gate)).astype(y_ref.dtype)


def _s5_mixer(u_tm, bsz, bmat, ar, ai, cre, cim, d_skip, w_glu, b_glu):
    rows, width = u_tm.shape
    seq = rows // bsz
    steps = min(S5_TIME_CHUNK, seq)
    r = steps * bsz
    n_state = ar.shape[1] * 2
    return pl.pallas_call(
        functools.partial(_s5_body, steps=steps),
        grid=(seq // steps,),
        in_specs=[pl.BlockSpec((r, width), lambda i: (i, 0)),
                  _resident(bmat.shape), _resident(ar.shape), _resident(ai.shape),
                  _resident(cre.shape), _resident(cim.shape), _resident(d_skip.shape),
                  _resident(w_glu.shape), _resident(b_glu.shape)],
        out_specs=pl.BlockSpec((r, width), lambda i: (i, 0)),
        out_shape=jax.ShapeDtypeStruct((rows, width), BF16),
        scratch_shapes=[pltpu.VMEM((r, n_state), F32), pltpu.VMEM((bsz, n_state), F32)],
        compiler_params=_params(("arbitrary",)),
        name="s5_mixer",
    )(u_tm, bmat, ar, ai, cre, cim, d_skip, w_glu, b_glu)


def _moba_tables(nb):
    items = [(i, jp, 0) for i in range(nb) for jp in range(i // 2)]
    n_past_iter, odd = divmod(len(items), 2)
    assert nb % 2 == 0 and not odd
    items += [(i, i // 2, 1 if i % 2 else 2) for i in range(nb)]
    n_iter = (len(items) + 1) // 2 + 2
    tab = np.zeros((6, 2 * n_iter), np.int32)
    for c in range(2 * n_iter):
        k, e = divmod(c, 2)

        def item(t):
            return items[t] if 0 <= t < len(items) else None

        one, two, three = item(2 * k + e), item(2 * (k - 1) + e), item(2 * (k - 2) + e)
        tab[0:3, c] = one if one else (0, 0, 0)
        tab[3, c] = two[0] if two else nb
        tab[4:6, c] = three[0:2] if three else (nb, 0)
    return tab, n_past_iter, n_iter


def _moba_body(tab_ref, q_ref, k_ref, v_ref, o_ref, kx_ref, vt_ref, cb_ref, qx_ref,
               m_ref, acc_ref, s_ring, cm_ring, p_ring, al_ring, *, nb, n_past_iter, n_iter):
    blk = MOBA_BLOCK
    pair = 2 * HEAD_DIM
    seq = nb * blk
    two = 2 * blk
    vrows = vt_ref.shape[2]

    @pl.when((pl.program_id(0) == 0) & (pl.program_id(1) == 0))
    def _():
        row = lax.broadcasted_iota(jnp.int32, (two, pair), 0)
        lane = lax.broadcasted_iota(jnp.int32, (two, pair), 1)
        ones_row = lax.broadcasted_iota(jnp.int32, (vrows - HEAD_DIM, two), 0) == 0
        for jp in range(nb // 2):
            block = jnp.where(row < blk, 2 * jp, 2 * jp + 1)
            kx_ref[jp, :, pair:] = jnp.where(lane == block, 1.0, 0.0).astype(BF16)
            for h in range(2):
                vt_ref[jp, h, HEAD_DIM:, :] = jnp.where(ones_row, 1.0, 0.0).astype(BF16)
        kpos = lax.broadcasted_iota(jnp.int32, (blk, blk), 0)
        qpos = lax.broadcasted_iota(jnp.int32, (blk, blk), 1)
        tri = jnp.where(kpos <= qpos, 0.0, NEG_INF)
        zero = jnp.zeros((blk, blk), F32)
        cb_ref[0] = jnp.concatenate([zero, zero], axis=0)
        cb_ref[1] = jnp.concatenate([zero, tri], axis=0)
        cb_ref[2] = jnp.concatenate([tri, jnp.full((blk, blk), NEG_INF, F32)], axis=0)
        s_ring[...] = jnp.zeros_like(s_ring)
        cm_ring[...] = jnp.zeros_like(cm_ring)
        p_ring[...] = jnp.zeros_like(p_ring)
        al_ring[...] = jnp.zeros_like(al_ring)

    for jp in range(nb // 2):
        rows = slice(jp * two, (jp + 1) * two)
        kx_ref[jp, :, 0:pair] = k_ref[0, rows, :]
        vt = v_ref[0, rows, :].astype(F32).T.astype(BF16)
        for h in range(2):
            vt_ref[jp, h, 0:HEAD_DIM, :] = vt[h * HEAD_DIM:(h + 1) * HEAD_DIM, :]
    km = jnp.mean(k_ref[0].astype(F32).reshape(nb, blk, pair), axis=1).astype(BF16)

    q2 = q_ref[0]
    lane = lax.broadcasted_iota(jnp.int32, q2.shape, 1)
    nidx = lax.broadcasted_iota(jnp.int32, (nb, seq), 0)
    qblk = lax.shift_right_logical(lax.broadcasted_iota(jnp.int32, (nb, seq), 1), blk.bit_length() - 1)
    for h in range(2):
        head = (lane < HEAD_DIM) if h == 0 else (lane >= HEAD_DIM)
        qh = jnp.where(head, q2, jnp.zeros_like(q2))
        g = lax.dot_general(km, qh, _NT, preferred_element_type=F32)
        g = jnp.where(nidx < qblk, g, NEG_INF)
        picked = jnp.zeros((nb, seq), jnp.int32)
        for _ in range(min(MOBA_TOPK, nb - 1)):
            top = jnp.max(g, axis=0, keepdims=True)
            first = jnp.min(jnp.where(g == top, nidx, nb), axis=0, keepdims=True)
            hit = nidx == first
            picked = jnp.where(hit, 1, picked)
            g = jnp.where(hit, TAKEN, g)
        allow = ((picked == 1) & (nidx < qblk)) | (nidx == qblk)
        bias = jnp.where(allow, 0.0, NEG_INF)
        bias = jnp.concatenate([bias, jnp.zeros((pair - nb, seq), F32)], axis=0)
        qs = (qh.astype(F32) * (HEAD_DIM ** -0.5 * LOG2_E)).T.astype(BF16)
        bias = bias.astype(BF16)
        for i in range(nb):
            qx_ref[2 * i + h, 0:pair, :] = qs[:, i * blk:(i + 1) * blk]
            qx_ref[2 * i + h, pair:, :] = bias[:, i * blk:(i + 1) * blk]

    m_ref[...] = jnp.full(m_ref.shape, NEG_INF, F32)
    acc_ref[...] = jnp.zeros_like(acc_ref)

    def step(k, carry, own_block):
        for e in range(2):
            c = 2 * k + e
            i1, j1, var = tab_ref[0, c], tab_ref[1, c], tab_ref[2, c]
            i2, i3, j3 = tab_ref[3, c], tab_ref[4, c], tab_ref[5, c]
            for h in range(2):
                r = 2 * e + h
                a3 = 2 * i3 + h
                pv = jnp.dot(vt_ref[j3, h], p_ring[r], preferred_element_type=F32)
                acc_ref[a3] = al_ring[r] * acc_ref[a3] + pv
                a2 = 2 * i2 + h
                m_old = m_ref[a2]
                m_new = jnp.maximum(m_old, cm_ring[r])
                m_ref[a2] = m_new
                al_ring[r] = jnp.exp2(m_old - m_new)
                p_ring[r] = jnp.exp2(s_ring[r] - m_new).astype(BF16)
                s = jnp.dot(kx_ref[j1], qx_ref[2 * i1 + h], preferred_element_type=F32)
                if own_block:
                    s = s + cb_ref[var]
                s_ring[r] = s
                cm_ring[r] = jnp.max(s, axis=0, keepdims=True)
        return carry

    lax.fori_loop(0, n_past_iter, functools.partial(step, own_block=False), 0)
    lax.fori_loop(n_past_iter, n_iter, functools.partial(step, own_block=True), 0)

    for i in range(nb):
        o = jnp.concatenate([acc_ref[2 * i + h, 0:HEAD_DIM, :] / acc_ref[2 * i + h, HEAD_DIM:HEAD_DIM + 1, :]
                             for h in range(2)], axis=0)
        o_ref[0, i * blk:(i + 1) * blk, :] = o.T.astype(o_ref.dtype)


def _moba(q, k, v):
    bsz, seq, width = q.shape
    blk = MOBA_BLOCK
    nb = seq // blk
    pair = 2 * HEAD_DIM
    tab, n_past_iter, n_iter = _moba_tables(nb)
    spec = pl.BlockSpec((1, seq, pair), lambda b, hp, tab_ref: (b, 0, hp))
    n_state = 2 * (nb + 1)
    vrows = HEAD_DIM + BF16_SUBLANES
    return pl.pallas_call(
        functools.partial(_moba_body, nb=nb, n_past_iter=n_past_iter, n_iter=n_iter),
        grid_spec=pltpu.PrefetchScalarGridSpec(
            num_scalar_prefetch=1,
            grid=(bsz, width // pair),
            in_specs=[spec, spec, spec],
            out_specs=spec,
            scratch_shapes=[
                pltpu.VMEM((nb // 2, 2 * blk, 2 * pair), BF16),
                pltpu.VMEM((nb // 2, 2, vrows, 2 * blk), BF16),
                pltpu.VMEM((3, 2 * blk, blk), F32),
                pltpu.VMEM((2 * nb, 2 * pair, blk), BF16),
                pltpu.VMEM((n_state, 1, blk), F32),
                pltpu.VMEM((n_state, vrows, blk), F32),
                pltpu.VMEM((4, 2 * blk, blk), F32),
                pltpu.VMEM((4, 1, blk), F32),
                pltpu.VMEM((4, 2 * blk, blk), BF16),
                pltpu.VMEM((4, 1, blk), F32),
            ]),
        out_shape=jax.ShapeDtypeStruct((bsz, seq, width), BF16),
        compiler_params=_params(("arbitrary", "arbitrary")),
        name="moba",
    )(jnp.asarray(tab), q, k, v)


def _post_body(x_ref, ya_ref, yb_ref, ga_ref, gb_ref, wa_ref, wb_ref, wo_ref, gpm_ref, gpf_ref,
               w1_ref, w2_ref, gpo_ref, o_ref):
    a = jnp.dot(ya_ref[...], wa_ref[...], preferred_element_type=F32)
    b = jnp.dot(yb_ref[...], wb_ref[...], preferred_element_type=F32)
    merged = (jax.nn.sigmoid(ga_ref[...].astype(F32)) * a
              + jax.nn.sigmoid(gb_ref[...].astype(F32)) * b)
    mix = jnp.dot(merged.astype(BF16), wo_ref[...], preferred_element_type=F32)
    h = x_ref[...] + _rms(mix, gpm_ref[...])
    f = _rms(h, gpf_ref[...]).astype(BF16)
    d_ff = w1_ref.shape[1]
    ck = min(FFN_CHUNK, d_ff)
    acc = jnp.zeros(h.shape, F32)
    for c in range(d_ff // ck):
        t = jnp.dot(f, w1_ref[:, c * ck:(c + 1) * ck], preferred_element_type=F32)
        t = jnp.square(jnp.maximum(t, 0.0)).astype(BF16)
        acc = acc + jnp.dot(t, w2_ref[c * ck:(c + 1) * ck, :], preferred_element_type=F32)
    o_ref[...] = h + _rms(acc, gpo_ref[...])


def _post(x2, ya, yb, ga, gb, wa, wb, wo, gpm, gpf, w1, w2, gpo):
    t, d = x2.shape
    tm = min(TOKEN_TILE, t)

    def rows(a):
        return pl.BlockSpec((tm, a.shape[1]), lambda i: (i, 0))

    consts = (wa, wb, wo, gpm, gpf, w1, w2, gpo)
    return pl.pallas_call(
        _post_body,
        grid=(t // tm,),
        in_specs=[rows(a) for a in (x2, ya, yb, ga, gb)] + [_resident(c.shape) for c in consts],
        out_specs=rows(x2),
        out_shape=jax.ShapeDtypeStruct((t, d), x2.dtype),
        compiler_params=_params(("parallel",)),
        name="post",
    )(x2, ya, yb, ga, gb, *consts)


def _layer(h, g_pre_mix, w_in, lam_re, lam_im, log_dt, b_re, b_im, c_re, c_im, d_skip, w_glu, b_glu,
           w_branch_a, w_branch_b, w_out, g_post_mix, g_pre_ffn, w_ff1, w_ff2, g_post_ffn):
    bsz, seq, d = h.shape
    ssm_w = lam_re.shape[0] * SSM_GROUP
    attn_w = ATTN_HEADS * HEAD_DIM
    x2 = h.reshape(bsz * seq, d)

    zs, q, k, v, ga, gb = _inproj(x2, g_pre_mix.reshape(1, d), w_in.astype(BF16),
                                  (ssm_w, attn_w, attn_w, attn_w, d, d))

    ar, ai, bbr, bbi = _s5_prep(lam_re, lam_im, log_dt, b_re, b_im)
    g, p = lam_re.shape
    hg = g // 2
    n_half = hg * p

    def halves(m):
        return jnp.stack([_block_diag(m[:hg]), _block_diag(m[hg:])])

    bmat = jnp.concatenate([halves(bbr), halves(bbi)], axis=2).astype(BF16)
    cre = halves(jnp.transpose(c_re, (0, 2, 1))).astype(BF16)
    cim = halves(jnp.transpose(c_im, (0, 2, 1))).astype(BF16)
    ar8 = jnp.broadcast_to(ar.reshape(1, 2 * n_half), (bsz, 2 * n_half))
    ai8 = jnp.broadcast_to(ai.reshape(1, 2 * n_half), (bsz, 2 * n_half))

    u_tm = jnp.transpose(zs.reshape(bsz, seq, ssm_w), (1, 0, 2)).reshape(seq * bsz, ssm_w)
    ya_tm = _s5_mixer(u_tm, bsz, bmat, ar8, ai8, cre, cim, d_skip.reshape(1, ssm_w),
                      w_glu.astype(BF16), b_glu.reshape(1, ssm_w))
    ya = jnp.transpose(ya_tm.reshape(seq, bsz, ssm_w), (1, 0, 2)).reshape(bsz * seq, ssm_w)

    shape3 = (bsz, seq, attn_w)
    yb = _moba(q.reshape(shape3), k.reshape(shape3), v.reshape(shape3)).reshape(bsz * seq, attn_w)

    out = _post(x2, ya, yb, ga, gb, w_branch_a.astype(BF16), w_branch_b.astype(BF16),
                w_out.astype(BF16), g_post_mix.reshape(1, d), g_pre_ffn.reshape(1, d),
                w_ff1.astype(BF16), w_ff2.astype(BF16), g_post_ffn.reshape(1, d))
    return out.reshape(bsz, seq, d)


def kernel(x, g_pre_mix, w_in, lam_re, lam_im, log_dt, b_re, b_im, c_re, c_im, d_skip, w_glu, b_glu,
           w_branch_a, w_branch_b, w_out, g_post_mix, g_pre_ffn, w_ff1, w_ff2, g_post_ffn):
    per_layer = (g_pre_mix, w_in, lam_re, lam_im, log_dt, b_re, b_im, c_re, c_im, d_skip, w_glu, b_glu,
                 w_branch_a, w_branch_b, w_out, g_post_mix, g_pre_ffn, w_ff1, w_ff2, g_post_ffn)
    h = x
    for l in range(g_pre_mix.shape[0]):
        h = _layer(h, *(p[l] for p in per_layer))
    return h
```

```python
import functools

import jax
import jax.numpy as jnp
import numpy as np
from jax import lax
from jax.experimental import pallas as pl
from jax.experimental.pallas import tpu as pltpu

F32 = jnp.float32
BF16 = jnp.bfloat16

RMS_EPS = 1e-6
NEG_INF = -1e30
TAKEN = -3e38

SSM_GROUP = 16
SSM_STATE = 64
ATTN_HEADS = 8
HEAD_DIM = 64
MOBA_BLOCK = 256
MOBA_TOPK = 3
MOBA_ITEMS = 8

LOG2_E = 1.4426950408889634

SUBLANES = 8
BF16_SUBLANES = 16
LANES = 128
VMEM_LIMIT_BYTES = 56 * 1024 * 1024

TOKEN_TILE = 512
S5_TIME_CHUNK = 64
S5_COL_GROUP = 512
FFN_CHUNK = 1024

_NT = (((1,), (1,)), ((), ()))


def _params(sem):
    return pltpu.CompilerParams(dimension_semantics=sem, vmem_limit_bytes=VMEM_LIMIT_BYTES)


def _resident(shape):
    zeros = (0,) * len(shape)
    return pl.BlockSpec(shape, lambda *_: zeros, pipeline_mode=pl.Buffered(1))


def _rms(x, g):
    var = jnp.mean(x * x, axis=-1, keepdims=True)
    return (x * lax.rsqrt(var + RMS_EPS)) * g


def _inproj_body(x_ref, g_ref, w_ref, *out_refs):
    ub = _rms(x_ref[...], g_ref[...]).astype(BF16)
    col = 0
    for ref in out_refs:
        n = ref.shape[-1]
        ref[...] = jnp.dot(ub, w_ref[:, col:col + n], preferred_element_type=F32).astype(ref.dtype)
        col += n


def _inproj(x2, g, w, widths):
    t, d = x2.shape
    tm = min(TOKEN_TILE, t)
    return pl.pallas_call(
        _inproj_body,
        grid=(t // tm,),
        in_specs=[pl.BlockSpec((tm, d), lambda i: (i, 0)), _resident(g.shape), _resident(w.shape)],
        out_specs=[pl.BlockSpec((tm, n), lambda i: (i, 0)) for n in widths],
        out_shape=[jax.ShapeDtypeStruct((t, n), BF16) for n in widths],
        compiler_params=_params(("parallel",)),
        name="inproj",
    )(x2, g, w)


def _s5_prep_body(lr_ref, li_ref, ldt_ref, br_ref, bi_ref, ar_ref, ai_ref, bbr_ref, bbi_ref):
    lr = lr_ref[...]
    li = li_ref[...]
    dt = jnp.exp(ldt_ref[...])
    mag = jnp.exp(lr * dt)
    ab_re = mag * jnp.cos(li * dt)
    ab_im = mag * jnp.sin(li * dt)
    nr = ab_re - 1.0
    ni = ab_im
    den = lr * lr + li * li
    coef_re = (nr * lr + ni * li) / den
    coef_im = (ni * lr - nr * li) / den
    br = br_ref[...]
    bi = bi_ref[...]
    ar_ref[...] = ab_re
    ai_ref[...] = ab_im
    bbr_ref[...] = coef_re * br - coef_im * bi
    bbi_ref[...] = coef_re * bi + coef_im * br


def _s5_prep(lam_re, lam_im, log_dt, b_re, b_im):
    g, p = lam_re.shape
    h = b_re.shape[-1]
    lr = lam_re.reshape(g, 1, p)
    li = lam_im.reshape(g, 1, p)
    ldt = jnp.broadcast_to(log_dt.reshape(g, 1, 1), (g, 1, p))
    br = jnp.transpose(b_re, (0, 2, 1))
    bi = jnp.transpose(b_im, (0, 2, 1))
    small = jax.ShapeDtypeStruct((g, 1, p), F32)
    big = jax.ShapeDtypeStruct((g, h, p), F32)
    return pl.pallas_call(_s5_prep_body, out_shape=[small, small, big, big], name="s5_prep")(lr, li, ldt, br, bi)


def _block_diag(m):
    g, r, c = m.shape
    idx = jnp.arange(g)
    return jnp.zeros((g, r, g, c), m.dtype).at[idx, :, idx, :].set(m).reshape(g * r, g * c)


def _s5_body(u_ref, b_ref, ar_ref, ai_ref, cre_ref, cim_ref, dsk_ref, wg_ref, bg_ref, y_ref,
             bu_ref, st_ref, *, steps):
    half_in = u_ref.shape[1] // 2
    half_st = st_ref.shape[1] // 4

    @pl.when(pl.program_id(0) == 0)
    def _():
        st_ref[...] = jnp.zeros_like(st_ref)

    for c in range(2):
        bu_ref[:, 2 * c * half_st:2 * (c + 1) * half_st] = jnp.dot(
            u_ref[:, c * half_in:(c + 1) * half_in], b_ref[c], preferred_element_type=F32)

    w = S5_COL_GROUP
    for c in range(2):
        for s in range(half_st // w):
            re0 = 2 * c * half_st + s * w
            im0 = re0 + half_st
            a0 = c * half_st + s * w
            ar = ar_ref[:, a0:a0 + w]
            ai = ai_ref[:, a0:a0 + w]

            def step(t, carry, re0=re0, im0=im0, ar=ar, ai=ai):
                xr, xi = carry
                row = pl.multiple_of(t * SUBLANES, SUBLANES)
                nxr = ar * xr - ai * xi + bu_ref[pl.ds(row, SUBLANES), re0:re0 + w]
                nxi = ar * xi + ai * xr + bu_ref[pl.ds(row, SUBLANES), im0:im0 + w]
                bu_ref[pl.ds(row, SUBLANES), re0:re0 + w] = nxr
                bu_ref[pl.ds(row, SUBLANES), im0:im0 + w] = nxi
                return nxr, nxi

            xr, xi = lax.fori_loop(0, steps, step,
                                   (st_ref[:, re0:re0 + w], st_ref[:, im0:im0 + w]), unroll=8)
            st_ref[:, re0:re0 + w] = xr
            st_ref[:, im0:im0 + w] = xi

    ys = []
    for c in range(2):
        xre = bu_ref[:, 2 * c * half_st:(2 * c + 1) * half_st].astype(BF16)
        xim = bu_ref[:, (2 * c + 1) * half_st:(2 * c + 2) * half_st].astype(BF16)
        ys.append(jnp.dot(xre, cre_ref[c], preferred_element_type=F32)
                  - jnp.dot(xim, cim_ref[c], preferred_element_type=F32))
    y = jnp.concatenate(ys, axis=1) + dsk_ref[...] * u_ref[...].astype(F32)
    v = jax.nn.gelu(y)
    gate = jnp.dot(v.astype(BF16), wg_ref[...], preferred_element_type=F32) + bg_ref[...]
    y_ref[...] = (v * jax.nn.sigmoid(gate)).astype(y_ref.dtype)


def _s5_mixer(u_tm, bsz, bmat, ar, ai, cre, cim, d_skip, w_glu, b_glu):
    rows, width = u_tm.shape
    seq = rows // bsz
    steps = min(S5_TIME_CHUNK, seq)
    r = steps * bsz
    n_state = ar.shape[1] * 2
    return pl.pallas_call(
        functools.partial(_s5_body, steps=steps),
        grid=(seq // steps,),
        in_specs=[pl.BlockSpec((r, width), lambda i: (i, 0)),
                  _resident(bmat.shape), _resident(ar.shape), _resident(ai.shape),
                  _resident(cre.shape), _resident(cim.shape), _resident(d_skip.shape),
                  _resident(w_glu.shape), _resident(b_glu.shape)],
        out_specs=pl.BlockSpec((r, width), lambda i: (i, 0)),
        out_shape=jax.ShapeDtypeStruct((rows, width), BF16),
        scratch_shapes=[pltpu.VMEM((r, n_state), F32), pltpu.VMEM((bsz, n_state), F32)],
        compiler_params=_params(("arbitrary",)),
        name="s5_mixer",
    )(u_tm, bmat, ar, ai, cre, cim, d_skip, w_glu, b_glu)


def _moba_tables(nb):
    per = MOBA_ITEMS
    items = [(i, jp, 0) for i in range(nb) for jp in range(i // 2)]
    n_past, rest = divmod(len(items), per)
    items += [(i, i // 2, 0 if i % 2 else 1) for i in range(nb)]
    n_groups, rest2 = divmod(len(items), per)
    assert nb % 2 == 0 and not rest and not rest2 and 2 <= n_past < n_groups
    tab = np.zeros((6, per * (n_groups + 2)), np.int32)
    for c in range(tab.shape[1]):
        if c < len(items):
            tab[0:3, c] = items[c]
        if 0 <= c - per < len(items):
            tab[3, c] = items[c - per][0]
        if 0 <= c - 2 * per < len(items):
            tab[4:6, c] = items[c - 2 * per][0:2]
    return tab, n_past, n_groups


def _moba_body(tab_ref, q_ref, k_ref, v_ref, o_ref, kx_ref, vt_ref, cb_ref, qx_ref,
               m_ref, acc_ref, s_ring, cm_ring, p_ring, al_ring, *, nb, n_past, n_groups):
    blk = MOBA_BLOCK
    pair = 2 * HEAD_DIM
    seq = nb * blk
    two = 2 * blk
    vrows = vt_ref.shape[2]

    @pl.when((pl.program_id(0) == 0) & (pl.program_id(1) == 0))
    def _():
        row = lax.broadcasted_iota(jnp.int32, (two, pair), 0)
        lane = lax.broadcasted_iota(jnp.int32, (two, pair), 1)
        ones_row = lax.broadcasted_iota(jnp.int32, (vrows - HEAD_DIM, two), 0) == 0
        for jp in range(nb // 2):
            block = jnp.where(row < blk, 2 * jp, 2 * jp + 1)
            kx_ref[jp, :, pair:] = jnp.where(lane == block, 1.0, 0.0).astype(BF16)
            for h in range(2):
                vt_ref[jp, h, HEAD_DIM:, :] = jnp.where(ones_row, 1.0, 0.0).astype(BF16)
        kpos = lax.broadcasted_iota(jnp.int32, (blk, blk), 0)
        qpos = lax.broadcasted_iota(jnp.int32, (blk, blk), 1)
        tri = jnp.where(kpos <= qpos, 0.0, NEG_INF)
        cb_ref[0] = jnp.concatenate([jnp.zeros((blk, blk), F32), tri], axis=0)
        cb_ref[1] = jnp.concatenate([tri, jnp.full((blk, blk), NEG_INF, F32)], axis=0)

    for jp in range(nb // 2):
        rows = slice(jp * two, (jp + 1) * two)
        kx_ref[jp, :, 0:pair] = k_ref[0, rows, :]
        vt = v_ref[0, rows, :].astype(F32).T.astype(BF16)
        for h in range(2):
            vt_ref[jp, h, 0:HEAD_DIM, :] = vt[h * HEAD_DIM:(h + 1) * HEAD_DIM, :]
    km = jnp.mean(k_ref[0].astype(F32).reshape(nb, blk, pair), axis=1).astype(BF16)

    q2 = q_ref[0]
    lane = lax.broadcasted_iota(jnp.int32, q2.shape, 1)
    nidx = lax.broadcasted_iota(jnp.int32, (nb, seq), 0)
    qblk = lax.shift_right_logical(lax.broadcasted_iota(jnp.int32, (nb, seq), 1), blk.bit_length() - 1)
    for h in range(2):
        head = (lane < HEAD_DIM) if h == 0 else (lane >= HEAD_DIM)
        qh = jnp.where(head, q2, jnp.zeros_like(q2))
        g = lax.dot_general(km, qh, _NT, preferred_element_type=F32)
        g = jnp.where(nidx < qblk, g, NEG_INF)
        picked = jnp.zeros((nb, seq), jnp.int32)
        for _ in range(min(MOBA_TOPK, nb - 1)):
            top = jnp.max(g, axis=0, keepdims=True)
            first = jnp.min(jnp.where(g == top, nidx, nb), axis=0, keepdims=True)
            hit = nidx == first
            picked = jnp.where(hit, 1, picked)
            g = jnp.where(hit, TAKEN, g)
        allow = ((picked == 1) & (nidx < qblk)) | (nidx == qblk)
        bias = jnp.where(allow, 0.0, NEG_INF)
        bias = jnp.concatenate([bias, jnp.zeros((pair - nb, seq), F32)], axis=0)
        qs = (qh.astype(F32) * (HEAD_DIM ** -0.5 * LOG2_E)).T.astype(BF16)
        bias = bias.astype(BF16)
        for i in range(nb):
            qx_ref[2 * i + h, 0:pair, :] = qs[:, i * blk:(i + 1) * blk]
            qx_ref[2 * i + h, pair:, :] = bias[:, i * blk:(i + 1) * blk]

    m_ref[...] = jnp.full(m_ref.shape, NEG_INF, F32)
    acc_ref[...] = jnp.zeros_like(acc_ref)

    def step(k, carry, own_block, stages=(1, 2, 3)):
        for e in range(MOBA_ITEMS):
            c = MOBA_ITEMS * k + e
            for h in range(2):
                r = 2 * e + h
                if 3 in stages:
                    a3 = 2 * tab_ref[4, c] + h
                    pv = jnp.dot(vt_ref[tab_ref[5, c], h], p_ring[r], preferred_element_type=F32)
                    acc_ref[a3] = al_ring[r] * acc_ref[a3] + pv
                if 2 in stages:
                    a2 = 2 * tab_ref[3, c] + h
                    m_old = m_ref[a2]
                    m_new = jnp.maximum(m_old, cm_ring[r])
                    m_ref[a2] = m_new
                    al_ring[r] = jnp.exp2(m_old - m_new)
                    p_ring[r] = jnp.exp2(s_ring[r] - m_new).astype(BF16)
                if 1 in stages:
                    s = jnp.dot(kx_ref[tab_ref[1, c]], qx_ref[2 * tab_ref[0, c] + h],
                                preferred_element_type=F32)
                    if own_block:
                        s = s + cb_ref[tab_ref[2, c]]
                    s_ring[r] = s
                    cm_ring[r] = jnp.max(s, axis=0, keepdims=True)
        return carry

    step(0, 0, own_block=False, stages=(1,))
    step(1, 0, own_block=False, stages=(1, 2))
    lax.fori_loop(2, n_past, functools.partial(step, own_block=False), 0)
    lax.fori_loop(n_past, n_groups, functools.partial(step, own_block=True), 0)
    step(n_groups, 0, own_block=False, stages=(2, 3))
    step(n_groups + 1, 0, own_block=False, stages=(3,))

    for i in range(nb):
        o = jnp.concatenate([acc_ref[2 * i + h, 0:HEAD_DIM, :] / acc_ref[2 * i + h, HEAD_DIM:HEAD_DIM + 1, :]
                             for h in range(2)], axis=0)
        o_ref[0, i * blk:(i + 1) * blk, :] = o.T.astype(o_ref.dtype)


def _moba(q, k, v):
    bsz, seq, width = q.shape
    blk = MOBA_BLOCK
    nb = seq // blk
    pair = 2 * HEAD_DIM
    tab, n_past, n_groups = _moba_tables(nb)
    spec = pl.BlockSpec((1, seq, pair), lambda b, hp, tab_ref: (b, 0, hp))
    n_state = 2 * nb
    n_ring = 2 * MOBA_ITEMS
    vrows = HEAD_DIM + BF16_SUBLANES
    return pl.pallas_call(
        functools.partial(_moba_body, nb=nb, n_past=n_past, n_groups=n_groups),
        grid_spec=pltpu.PrefetchScalarGridSpec(
            num_scalar_prefetch=1,
            grid=(bsz, width // pair),
            in_specs=[spec, spec, spec],
            out_specs=spec,
            scratch_shapes=[
                pltpu.VMEM((nb // 2, 2 * blk, 2 * pair), BF16),
                pltpu.VMEM((nb // 2, 2, vrows, 2 * blk), BF16),
                pltpu.VMEM((2, 2 * blk, blk), F32),
                pltpu.VMEM((2 * nb, 2 * pair, blk), BF16),
                pltpu.VMEM((n_state, 1, blk), F32),
                pltpu.VMEM((n_state, vrows, blk), F32),
                pltpu.VMEM((n_ring, 2 * blk, blk), F32),
                pltpu.VMEM((n_ring, 1, blk), F32),
                pltpu.VMEM((n_ring, 2 * blk, blk), BF16),
                pltpu.VMEM((n_ring, 1, blk), F32),
            ]),
        out_shape=jax.ShapeDtypeStruct((bsz, seq, width), BF16),
        compiler_params=_params(("arbitrary", "arbitrary")),
        name="moba",
    )(jnp.asarray(tab), q, k, v)


def _post_body(x_ref, ya_ref, yb_ref, ga_ref, gb_ref, wa_ref, wb_ref, wo_ref, gpm_ref, gpf_ref,
               w1_ref, w2_ref, gpo_ref, o_ref):
    a = jnp.dot(ya_ref[...], wa_ref[...], preferred_element_type=F32)
    b = jnp.dot(yb_ref[...], wb_ref[...], preferred_element_type=F32)
    merged = (jax.nn.sigmoid(ga_ref[...].astype(F32)) * a
              + jax.nn.sigmoid(gb_ref[...].astype(F32)) * b)
    mix = jnp.dot(merged.astype(BF16), wo_ref[...], preferred_element_type=F32)
    h = x_ref[...] + _rms(mix, gpm_ref[...])
    f = _rms(h, gpf_ref[...]).astype(BF16)
    d_ff = w1_ref.shape[1]
    ck = min(FFN_CHUNK, d_ff)
    acc = jnp.zeros(h.shape, F32)
    for c in range(d_ff // ck):
        t = jnp.dot(f, w1_ref[:, c * ck:(c + 1) * ck], preferred_element_type=F32)
        t = jnp.square(jnp.maximum(t, 0.0)).astype(BF16)
        acc = acc + jnp.dot(t, w2_ref[c * ck:(c + 1) * ck, :], preferred_element_type=F32)
    o_ref[...] = h + _rms(acc, gpo_ref[...])


def _post(x2, ya, yb, ga, gb, wa, wb, wo, gpm, gpf, w1, w2, gpo):
    t, d = x2.shape
    tm = min(TOKEN_TILE, t)

    def rows(a):
        return pl.BlockSpec((tm, a.shape[1]), lambda i: (i, 0))

    consts = (wa, wb, wo, gpm, gpf, w1, w2, gpo)
    return pl.pallas_call(
        _post_body,
        grid=(t // tm,),
        in_specs=[rows(a) for a in (x2, ya, yb, ga, gb)] + [_resident(c.shape) for c in consts],
        out_specs=rows(x2),
        out_shape=jax.ShapeDtypeStruct((t, d), x2.dtype),
        compiler_params=_params(("parallel",)),
        name="post",
    )(x2, ya, yb, ga, gb, *consts)


def _layer(h, g_pre_mix, w_in, lam_re, lam_im, log_dt, b_re, b_im, c_re, c_im, d_skip, w_glu, b_glu,
           w_branch_a, w_branch_b, w_out, g_post_mix, g_pre_ffn, w_ff1, w_ff2, g_post_ffn):
    bsz, seq, d = h.shape
    ssm_w = lam_re.shape[0] * SSM_GROUP
    attn_w = ATTN_HEADS * HEAD_DIM
    x2 = h.reshape(bsz * seq, d)

    zs, q, k, v, ga, gb = _inproj(x2, g_pre_mix.reshape(1, d), w_in.astype(BF16),
                                  (ssm_w, attn_w, attn_w, attn_w, d, d))

    ar, ai, bbr, bbi = _s5_prep(lam_re, lam_im, log_dt, b_re, b_im)
    g, p = lam_re.shape
    hg = g // 2
    n_half = hg * p

    def halves(m):
        return jnp.stack([_block_diag(m[:hg]), _block_diag(m[hg:])])

    bmat = jnp.concatenate([halves(bbr), halves(bbi)], axis=2).astype(BF16)
    cre = halves(jnp.transpose(c_re, (0, 2, 1))).astype(BF16)
    cim = halves(jnp.transpose(c_im, (0, 2, 1))).astype(BF16)
    ar8 = jnp.broadcast_to(ar.reshape(1, 2 * n_half), (bsz, 2 * n_half))
    ai8 = jnp.broadcast_to(ai.reshape(1, 2 * n_half), (bsz, 2 * n_half))

    u_tm = jnp.transpose(zs.reshape(bsz, seq, ssm_w), (1, 0, 2)).reshape(seq * bsz, ssm_w)
    ya_tm = _s5_mixer(u_tm, bsz, bmat, ar8, ai8, cre, cim, d_skip.reshape(1, ssm_w),
                      w_glu.astype(BF16), b_glu.reshape(1, ssm_w))
    ya = jnp.transpose(ya_tm.reshape(seq, bsz, ssm_w), (1, 0, 2)).reshape(bsz * seq, ssm_w)

    shape3 = (bsz, seq, attn_w)
    yb = _moba(q.reshape(shape3), k.reshape(shape3), v.reshape(shape3)).reshape(bsz * seq, attn_w)

    out = _post(x2, ya, yb, ga, gb, w_branch_a.astype(BF16), w_branch_b.astype(BF16),
                w_out.astype(BF16), g_post_mix.reshape(1, d), g_pre_ffn.reshape(1, d),
                w_ff1.astype(BF16), w_ff2.astype(BF16), g_post_ffn.reshape(1, d))
    return out.reshape(bsz, seq, d)


def kernel(x, g_pre_mix, w_in, lam_re, lam_im, log_dt, b_re, b_im, c_re, c_im, d_skip, w_glu, b_glu,
           w_branch_a, w_branch_b, w_out, g_post_mix, g_pre_ffn, w_ff1, w_ff2, g_post_ffn):
    per_layer = (g_pre_mix, w_in, lam_re, lam_im, log_dt, b_re, b_im, c_re, c_im, d_skip, w_glu, b_glu,
                 w_branch_a, w_branch_b, w_out, g_post_mix, g_pre_ffn, w_ff1, w_ff2, g_post_ffn)
    h = x
    for l in range(g_pre_mix.shape[0]):
        h = _layer(h, *(p[l] for p in per_layer))
    return h
```

```python
import functools

import jax
import jax.numpy as jnp
import numpy as np
from jax import lax
from jax.experimental import pallas as pl
from jax.experimental.pallas import tpu as pltpu

F32 = jnp.float32
BF16 = jnp.bfloat16

RMS_EPS = 1e-6
NEG_INF = -1e30
TAKEN = -3e38

SSM_GROUP = 16
SSM_STATE = 64
ATTN_HEADS = 8
HEAD_DIM = 64
MOBA_BLOCK = 256
MOBA_TOPK = 3
MOBA_ITEMS = 2

LOG2_E = 1.4426950408889634

SUBLANES = 8
BF16_SUBLANES = 16
LANES = 128
VMEM_LIMIT_BYTES = 56 * 1024 * 1024

TOKEN_TILE = 512
S5_TIME_CHUNK = 64
S5_COL_GROUP = 512
FFN_CHUNK = 1024

_NT = (((1,), (1,)), ((), ()))


def _params(sem):
    return pltpu.CompilerParams(dimension_semantics=sem, vmem_limit_bytes=VMEM_LIMIT_BYTES)


def _resident(shape):
    zeros = (0,) * len(shape)
    return pl.BlockSpec(shape, lambda *_: zeros, pipeline_mode=pl.Buffered(1))


def _rms(x, g):
    var = jnp.mean(x * x, axis=-1, keepdims=True)
    return (x * lax.rsqrt(var + RMS_EPS)) * g


def _inproj_body(x_ref, g_ref, w_ref, *out_refs):
    ub = _rms(x_ref[...], g_ref[...]).astype(BF16)
    col = 0
    for ref in out_refs:
        n = ref.shape[-1]
        ref[...] = jnp.dot(ub, w_ref[:, col:col + n], preferred_element_type=F32).astype(ref.dtype)
        col += n


def _inproj(x2, g, w, widths):
    t, d = x2.shape
    tm = min(TOKEN_TILE, t)
    return pl.pallas_call(
        _inproj_body,
        grid=(t // tm,),
        in_specs=[pl.BlockSpec((tm, d), lambda i: (i, 0)), _resident(g.shape), _resident(w.shape)],
        out_specs=[pl.BlockSpec((tm, n), lambda i: (i, 0)) for n in widths],
        out_shape=[jax.ShapeDtypeStruct((t, n), BF16) for n in widths],
        compiler_params=_params(("parallel",)),
        name="inproj",
    )(x2, g, w)


def _s5_prep_body(lr_ref, li_ref, ldt_ref, br_ref, bi_ref, ar_ref, ai_ref, bbr_ref, bbi_ref):
    lr = lr_ref[...]
    li = li_ref[...]
    dt = jnp.exp(ldt_ref[...])
    mag = jnp.exp(lr * dt)
    ab_re = mag * jnp.cos(li * dt)
    ab_im = mag * jnp.sin(li * dt)
    nr = ab_re - 1.0
    ni = ab_im
    den = lr * lr + li * li
    coef_re = (nr * lr + ni * li) / den
    coef_im = (ni * lr - nr * li) / den
    br = br_ref[...]
    bi = bi_ref[...]
    ar_ref[...] = ab_re
    ai_ref[...] = ab_im
    bbr_ref[...] = coef_re * br - coef_im * bi
    bbi_ref[...] = coef_re * bi + coef_im * br


def _s5_prep(lam_re, lam_im, log_dt, b_re, b_im):
    g, p = lam_re.shape
    h = b_re.shape[-1]
    lr = lam_re.reshape(g, 1, p)
    li = lam_im.reshape(g, 1, p)
    ldt = jnp.broadcast_to(log_dt.reshape(g, 1, 1), (g, 1, p))
    br = jnp.transpose(b_re, (0, 2, 1))
    bi = jnp.transpose(b_im, (0, 2, 1))
    small = jax.ShapeDtypeStruct((g, 1, p), F32)
    big = jax.ShapeDtypeStruct((g, h, p), F32)
    return pl.pallas_call(_s5_prep_body, out_shape=[small, small, big, big], name="s5_prep")(lr, li, ldt, br, bi)


def _block_diag(m):
    g, r, c = m.shape
    idx = jnp.arange(g)
    return jnp.zeros((g, r, g, c), m.dtype).at[idx, :, idx, :].set(m).reshape(g * r, g * c)


def _s5_body(u_ref, b_ref, ar_ref, ai_ref, cre_ref, cim_ref, dsk_ref, wg_ref, bg_ref, y_ref,
             bu_ref, st_ref, *, steps):
    half_in = u_ref.shape[1] // 2
    half_st = st_ref.shape[1] // 4

    @pl.when(pl.program_id(0) == 0)
    def _():
        st_ref[...] = jnp.zeros_like(st_ref)

    for c in range(2):
        bu_ref[:, 2 * c * half_st:2 * (c + 1) * half_st] = jnp.dot(
            u_ref[:, c * half_in:(c + 1) * half_in], b_ref[c], preferred_element_type=F32)

    w = S5_COL_GROUP
    for c in range(2):
        for s in range(half_st // w):
            re0 = 2 * c * half_st + s * w
            im0 = re0 + half_st
            a0 = c * half_st + s * w
            ar = ar_ref[:, a0:a0 + w]
            ai = ai_ref[:, a0:a0 + w]

            def step(t, carry, re0=re0, im0=im0, ar=ar, ai=ai):
                xr, xi = carry
                row = pl.multiple_of(t * SUBLANES, SUBLANES)
                nxr = ar * xr - ai * xi + bu_ref[pl.ds(row, SUBLANES), re0:re0 + w]
                nxi = ar * xi + ai * xr + bu_ref[pl.ds(row, SUBLANES), im0:im0 + w]
                bu_ref[pl.ds(row, SUBLANES), re0:re0 + w] = nxr
                bu_ref[pl.ds(row, SUBLANES), im0:im0 + w] = nxi
                return nxr, nxi

            xr, xi = lax.fori_loop(0, steps, step,
                                   (st_ref[:, re0:re0 + w], st_ref[:, im0:im0 + w]), unroll=8)
            st_ref[:, re0:re0 + w] = xr
            st_ref[:, im0:im0 + w] = xi

    ys = []
    for c in range(2):
        xre = bu_ref[:, 2 * c * half_st:(2 * c + 1) * half_st].astype(BF16)
        xim = bu_ref[:, (2 * c + 1) * half_st:(2 * c + 2) * half_st].astype(BF16)
        ys.append(jnp.dot(xre, cre_ref[c], preferred_element_type=F32)
                  - jnp.dot(xim, cim_ref[c], preferred_element_type=F32))
    y = jnp.concatenate(ys, axis=1) + dsk_ref[...] * u_ref[...].astype(F32)
    v = jax.nn.gelu(y)
    gate = jnp.dot(v.astype(BF16), wg_ref[...], preferred_element_type=F32) + bg_ref[...]
    y_ref[...] = (v * jax.nn.sigmoid(gate)).astype(y_ref.dtype)


def _s5_mixer(u_tm, bsz, bmat, ar, ai, cre, cim, d_skip, w_glu, b_glu):
    rows, width = u_tm.shape
    seq = rows // bsz
    steps = min(S5_TIME_CHUNK, seq)
    r = steps * bsz
    n_state = ar.shape[1] * 2
    return pl.pallas_call(
        functools.partial(_s5_body, steps=steps),
        grid=(seq // steps,),
        in_specs=[pl.BlockSpec((r, width), lambda i: (i, 0)),
                  _resident(bmat.shape), _resident(ar.shape), _resident(ai.shape),
                  _resident(cre.shape), _resident(cim.shape), _resident(d_skip.shape),
                  _resident(w_glu.shape), _resident(b_glu.shape)],
        out_specs=pl.BlockSpec((r, width), lambda i: (i, 0)),
        out_shape=jax.ShapeDtypeStruct((rows, width), BF16),
        scratch_shapes=[pltpu.VMEM((r, n_state), F32), pltpu.VMEM((bsz, n_state), F32)],
        compiler_params=_params(("arbitrary",)),
        name="s5_mixer",
    )(u_tm, bmat, ar, ai, cre, cim, d_skip, w_glu, b_glu)


def _moba_tables(nb):
    per = MOBA_ITEMS
    items = [(i, jp, 0) for i in range(nb) for jp in range(i // 2)]
    n_past, rest = divmod(len(items), per)
    items += [(i, i // 2, 0 if i % 2 else 1) for i in range(nb)]
    n_groups, rest2 = divmod(len(items), per)
    assert nb % 2 == 0 and not rest and not rest2 and 2 <= n_past < n_groups
    tab = np.zeros((6, per * (n_groups + 2)), np.int32)
    for c in range(tab.shape[1]):
        if c < len(items):
            tab[0:3, c] = items[c]
        if 0 <= c - per < len(items):
            tab[3, c] = items[c - per][0]
        if 0 <= c - 2 * per < len(items):
            tab[4:6, c] = items[c - 2 * per][0:2]
    return tab, n_past, n_groups


def _moba_body(tab_ref, q_ref, k_ref, v_ref, o_ref, kx_ref, vt_ref, cb_ref, qx_ref,
               m_ref, acc_ref, s_ring, cm_ring, p_ring, al_ring, *, nb, n_past, n_groups):
    blk = MOBA_BLOCK
    pair = 2 * HEAD_DIM
    seq = nb * blk
    two = 2 * blk
    vrows = vt_ref.shape[2]

    @pl.when((pl.program_id(0) == 0) & (pl.program_id(1) == 0))
    def _():
        row = lax.broadcasted_iota(jnp.int32, (two, pair), 0)
        lane = lax.broadcasted_iota(jnp.int32, (two, pair), 1)
        ones_row = lax.broadcasted_iota(jnp.int32, (vrows - HEAD_DIM, two), 0) == 0
        for jp in range(nb // 2):
            block = jnp.where(row < blk, 2 * jp, 2 * jp + 1)
            kx_ref[jp, :, pair:] = jnp.where(lane == block, 1.0, 0.0).astype(BF16)
            for h in range(2):
                vt_ref[jp, h, HEAD_DIM:, :] = jnp.where(ones_row, 1.0, 0.0).astype(BF16)
        kpos = lax.broadcasted_iota(jnp.int32, (blk, blk), 0)
        qpos = lax.broadcasted_iota(jnp.int32, (blk, blk), 1)
        tri = jnp.where(kpos <= qpos, 0.0, NEG_INF)
        cb_ref[0] = jnp.concatenate([jnp.zeros((blk, blk), F32), tri], axis=0)
        cb_ref[1] = jnp.concatenate([tri, jnp.full((blk, blk), NEG_INF, F32)], axis=0)

    for jp in range(nb // 2):
        rows = slice(jp * two, (jp + 1) * two)
        kx_ref[jp, :, 0:pair] = k_ref[0, rows, :]
        vt = v_ref[0, rows, :].astype(F32).T.astype(BF16)
        for h in range(2):
            vt_ref[jp, h, 0:HEAD_DIM, :] = vt[h * HEAD_DIM:(h + 1) * HEAD_DIM, :]
    km = jnp.mean(k_ref[0].astype(F32).reshape(nb, blk, pair), axis=1).astype(BF16)

    q2 = q_ref[0]
    lane = lax.broadcasted_iota(jnp.int32, q2.shape, 1)
    nidx = lax.broadcasted_iota(jnp.int32, (nb, seq), 0)
    qblk = lax.shift_right_logical(lax.broadcasted_iota(jnp.int32, (nb, seq), 1), blk.bit_length() - 1)
    for h in range(2):
        head = (lane < HEAD_DIM) if h == 0 else (lane >= HEAD_DIM)
        qh = jnp.where(head, q2, jnp.zeros_like(q2))
        g = lax.dot_general(km, qh, _NT, preferred_element_type=F32)
        g = jnp.where(nidx < qblk, g, NEG_INF)
        picked = jnp.zeros((nb, seq), jnp.int32)
        for _ in range(min(MOBA_TOPK, nb - 1)):
            top = jnp.max(g, axis=0, keepdims=True)
            first = jnp.min(jnp.where(g == top, nidx, nb), axis=0, keepdims=True)
            hit = nidx == first
            picked = jnp.where(hit, 1, picked)
            g = jnp.where(hit, TAKEN, g)
        allow = ((picked == 1) & (nidx < qblk)) | (nidx == qblk)
        bias = jnp.where(allow, 0.0, NEG_INF)
        bias = jnp.concatenate([bias, jnp.zeros((pair - nb, seq), F32)], axis=0)
        qs = (qh.astype(F32) * (HEAD_DIM ** -0.5 * LOG2_E)).T.astype(BF16)
        bias = bias.astype(BF16)
        for i in range(nb):
            qx_ref[2 * i + h, 0:pair, :] = qs[:, i * blk:(i + 1) * blk]
            qx_ref[2 * i + h, pair:, :] = bias[:, i * blk:(i + 1) * blk]

    m_ref[...] = jnp.full(m_ref.shape, NEG_INF, F32)
    acc_ref[...] = jnp.zeros_like(acc_ref)

    def step(k, carry, own_block, stages=(1, 2, 3)):
        for e in range(MOBA_ITEMS):
            c = MOBA_ITEMS * k + e
            for h in range(2):
                r = 2 * e + h
                if 3 in stages:
                    a3 = 2 * tab_ref[4, c] + h
                    pv = jnp.dot(vt_ref[tab_ref[5, c], h], p_ring[r], preferred_element_type=F32)
                    acc_ref[a3] = al_ring[r] * acc_ref[a3] + pv
                if 2 in stages:
                    a2 = 2 * tab_ref[3, c] + h
                    m_old = m_ref[a2]
                    m_new = jnp.maximum(m_old, cm_ring[r])
                    m_ref[a2] = m_new
                    al_ring[r] = jnp.exp2(m_old - m_new)
                    p_ring[r] = jnp.exp2(s_ring[r] - m_new).astype(BF16)
                if 1 in stages:
                    s = jnp.dot(kx_ref[tab_ref[1, c]], qx_ref[2 * tab_ref[0, c] + h],
                                preferred_element_type=F32)
                    if own_block:
                        s = s + cb_ref[tab_ref[2, c]]
                    s_ring[r] = s
                    cm_ring[r] = jnp.max(s, axis=0, keepdims=True)
        return carry

    step(0, 0, own_block=False, stages=(1,))
    step(1, 0, own_block=False, stages=(1, 2))
    lax.fori_loop(2, n_past, functools.partial(step, own_block=False), 0)
    lax.fori_loop(n_past, n_groups, functools.partial(step, own_block=True), 0)
    step(n_groups, 0, own_block=False, stages=(2, 3))
    step(n_groups + 1, 0, own_block=False, stages=(3,))

    for i in range(nb):
        o = jnp.concatenate([acc_ref[2 * i + h, 0:HEAD_DIM, :] / acc_ref[2 * i + h, HEAD_DIM:HEAD_DIM + 1, :]
                             for h in range(2)], axis=0)
        o_ref[0, i * blk:(i + 1) * blk, :] = o.T.astype(o_ref.dtype)


def _moba(q, k, v):
    bsz, seq, width = q.shape
    blk = MOBA_BLOCK
    nb = seq // blk
    pair = 2 * HEAD_DIM
    tab, n_past, n_groups = _moba_tables(nb)
    spec = pl.BlockSpec((1, seq, pair), lambda b, hp, tab_ref: (b, 0, hp))
    n_state = 2 * nb
    n_ring = 2 * MOBA_ITEMS
    vrows = HEAD_DIM + BF16_SUBLANES
    return pl.pallas_call(
        functools.partial(_moba_body, nb=nb, n_past=n_past, n_groups=n_groups),
        grid_spec=pltpu.PrefetchScalarGridSpec(
            num_scalar_prefetch=1,
            grid=(bsz, width // pair),
            in_specs=[spec, spec, spec],
            out_specs=spec,
            scratch_shapes=[
                pltpu.VMEM((nb // 2, 2 * blk, 2 * pair), BF16),
                pltpu.VMEM((nb // 2, 2, vrows, 2 * blk), BF16),
                pltpu.VMEM((2, 2 * blk, blk), F32),
                pltpu.VMEM((2 * nb, 2 * pair, blk), BF16),
                pltpu.VMEM((n_state, 1, blk), F32),
                pltpu.VMEM((n_state, vrows, blk), F32),
                pltpu.VMEM((n_ring, 2 * blk, blk), F32),
                pltpu.VMEM((n_ring, 1, blk), F32),
                pltpu.VMEM((n_ring, 2 * blk, blk), BF16),
                pltpu.VMEM((n_ring, 1, blk), F32),
            ]),
        out_shape=jax.ShapeDtypeStruct((bsz, seq, width), BF16),
        compiler_params=_params(("arbitrary", "arbitrary")),
        name="moba",
    )(jnp.asarray(tab), q, k, v)


def _post_body(x_ref, ya_ref, yb_ref, ga_ref, gb_ref, wa_ref, wb_ref, wo_ref, gpm_ref, gpf_ref,
               w1_ref, w2_ref, gpo_ref, o_ref):
    a = jnp.dot(ya_ref[...], wa_ref[...], preferred_element_type=F32)
    b = jnp.dot(yb_ref[...], wb_ref[...], preferred_element_type=F32)
    merged = (jax.nn.sigmoid(ga_ref[...].astype(F32)) * a
              + jax.nn.sigmoid(gb_ref[...].astype(F32)) * b)
    mix = jnp.dot(merged.astype(BF16), wo_ref[...], preferred_element_type=F32)
    h = x_ref[...] + _rms(mix, gpm_ref[...])
    f = _rms(h, gpf_ref[...]).astype(BF16)
    d_ff = w1_ref.shape[1]
    ck = min(FFN_CHUNK, d_ff)
    acc = jnp.zeros(h.shape, F32)
    for c in range(d_ff // ck):
        t = jnp.dot(f, w1_ref[:, c * ck:(c + 1) * ck], preferred_element_type=F32)
        t = jnp.square(jnp.maximum(t, 0.0)).astype(BF16)
        acc = acc + jnp.dot(t, w2_ref[c * ck:(c + 1) * ck, :], preferred_element_type=F32)
    o_ref[...] = h + _rms(acc, gpo_ref[...])


def _post(x2, ya, yb, ga, gb, wa, wb, wo, gpm, gpf, w1, w2, gpo):
    t, d = x2.shape
    tm = min(TOKEN_TILE, t)

    def rows(a):
        return pl.BlockSpec((tm, a.shape[1]), lambda i: (i, 0))

    consts = (wa, wb, wo, gpm, gpf, w1, w2, gpo)
    return pl.pallas_call(
        _post_body,
        grid=(t // tm,),
        in_specs=[rows(a) for a in (x2, ya, yb, ga, gb)] + [_resident(c.shape) for c in consts],
        out_specs=rows(x2),
        out_shape=jax.ShapeDtypeStruct((t, d), x2.dtype),
        compiler_params=_params(("parallel",)),
        name="post",
    )(x2, ya, yb, ga, gb, *consts)


def _layer(h, g_pre_mix, w_in, lam_re, lam_im, log_dt, b_re, b_im, c_re, c_im, d_skip, w_glu, b_glu,
           w_branch_a, w_branch_b, w_out, g_post_mix, g_pre_ffn, w_ff1, w_ff2, g_post_ffn):
    bsz, seq, d = h.shape
    ssm_w = lam_re.shape[0] * SSM_GROUP
    attn_w = ATTN_HEADS * HEAD_DIM
    x2 = h.reshape(bsz * seq, d)

    zs, q, k, v, ga, gb = _inproj(x2, g_pre_mix.reshape(1, d), w_in.astype(BF16),
                                  (ssm_w, attn_w, attn_w, attn_w, d, d))

    ar, ai, bbr, bbi = _s5_prep(lam_re, lam_im, log_dt, b_re, b_im)
    g, p = lam_re.shape
    hg = g // 2
    n_half = hg * p

    def halves(m):
        return jnp.stack([_block_diag(m[:hg]), _block_diag(m[hg:])])

    bmat = jnp.concatenate([halves(bbr), halves(bbi)], axis=2).astype(BF16)
    cre = halves(jnp.transpose(c_re, (0, 2, 1))).astype(BF16)
    cim = halves(jnp.transpose(c_im, (0, 2, 1))).astype(BF16)
    ar8 = jnp.broadcast_to(ar.reshape(1, 2 * n_half), (bsz, 2 * n_half))
    ai8 = jnp.broadcast_to(ai.reshape(1, 2 * n_half), (bsz, 2 * n_half))

    u_tm = jnp.transpose(zs.reshape(bsz, seq, ssm_w), (1, 0, 2)).reshape(seq * bsz, ssm_w)
    ya_tm = _s5_mixer(u_tm, bsz, bmat, ar8, ai8, cre, cim, d_skip.reshape(1, ssm_w),
                      w_glu.astype(BF16), b_glu.reshape(1, ssm_w))
    ya = jnp.transpose(ya_tm.reshape(seq, bsz, ssm_w), (1, 0, 2)).reshape(bsz * seq, ssm_w)

    shape3 = (bsz, seq, attn_w)
    yb = _moba(q.reshape(shape3), k.reshape(shape3), v.reshape(shape3)).reshape(bsz * seq, attn_w)

    out = _post(x2, ya, yb, ga, gb, w_branch_a.astype(BF16), w_branch_b.astype(BF16),
                w_out.astype(BF16), g_post_mix.reshape(1, d), g_pre_ffn.reshape(1, d),
                w_ff1.astype(BF16), w_ff2.astype(BF16), g_post_ffn.reshape(1, d))
    return out.reshape(bsz, seq, d)


def kernel(x, g_pre_mix, w_in, lam_re, lam_im, log_dt, b_re, b_im, c_re, c_im, d_skip, w_glu, b_glu,
           w_branch_a, w_branch_b, w_out, g_post_mix, g_pre_ffn, w_ff1, w_ff2, g_post_ffn):
    per_layer = (g_pre_mix, w_in, lam_re, lam_im, log_dt, b_re, b_im, c_re, c_im, d_skip, w_glu, b_glu,
                 w_branch_a, w_branch_b, w_out, g_post_mix, g_pre_ffn, w_ff1, w_ff2, g_post_ffn)
    h = x
    for l in range(g_pre_mix.shape[0]):
        h = _layer(h, *(p[l] for p in per_layer))
    return h
```

```python
import functools

import jax
import jax.numpy as jnp
import numpy as np
from jax import lax
from jax.experimental import pallas as pl
from jax.experimental.pallas import tpu as pltpu

F32 = jnp.float32
BF16 = jnp.bfloat16

RMS_EPS = 1e-6
NEG_INF = -1e30
TAKEN = -3e38

SSM_GROUP = 16
SSM_STATE = 64
ATTN_HEADS = 8
HEAD_DIM = 64
MOBA_BLOCK = 256
MOBA_TOPK = 3
MOBA_ITEMS = 4

LOG2_E = 1.4426950408889634

SUBLANES = 8
BF16_SUBLANES = 16
LANES = 128
VMEM_LIMIT_BYTES = 56 * 1024 * 1024

TOKEN_TILE = 512
S5_TIME_CHUNK = 64
S5_COL_GROUP = 512
FFN_CHUNK = 1024

_NT = (((1,), (1,)), ((), ()))


def _params(sem):
    return pltpu.CompilerParams(dimension_semantics=sem, vmem_limit_bytes=VMEM_LIMIT_BYTES)


def _resident(shape):
    zeros = (0,) * len(shape)
    return pl.BlockSpec(shape, lambda *_: zeros, pipeline_mode=pl.Buffered(1))


def _rms(x, g):
    var = jnp.mean(x * x, axis=-1, keepdims=True)
    return (x * lax.rsqrt(var + RMS_EPS)) * g


def _inproj_body(x_ref, g_ref, w_ref, *out_refs):
    ub = _rms(x_ref[...], g_ref[...]).astype(BF16)
    col = 0
    for ref in out_refs:
        n = ref.shape[-1]
        ref[...] = jnp.dot(ub, w_ref[:, col:col + n], preferred_element_type=F32).astype(ref.dtype)
        col += n


def _inproj(x2, g, w, widths):
    t, d = x2.shape
    tm = min(TOKEN_TILE, t)
    return pl.pallas_call(
        _inproj_body,
        grid=(t // tm,),
        in_specs=[pl.BlockSpec((tm, d), lambda i: (i, 0)), _resident(g.shape), _resident(w.shape)],
        out_specs=[pl.BlockSpec((tm, n), lambda i: (i, 0)) for n in widths],
        out_shape=[jax.ShapeDtypeStruct((t, n), BF16) for n in widths],
        compiler_params=_params(("parallel",)),
        name="inproj",
    )(x2, g, w)


def _s5_prep_body(lr_ref, li_ref, ldt_ref, br_ref, bi_ref, ar_ref, ai_ref, bbr_ref, bbi_ref):
    lr = lr_ref[...]
    li = li_ref[...]
    dt = jnp.exp(ldt_ref[...])
    mag = jnp.exp(lr * dt)
    ab_re = mag * jnp.cos(li * dt)
    ab_im = mag * jnp.sin(li * dt)
    nr = ab_re - 1.0
    ni = ab_im
    den = lr * lr + li * li
    coef_re = (nr * lr + ni * li) / den
    coef_im = (ni * lr - nr * li) / den
    br = br_ref[...]
    bi = bi_ref[...]
    ar_ref[...] = ab_re
    ai_ref[...] = ab_im
    bbr_ref[...] = coef_re * br - coef_im * bi
    bbi_ref[...] = coef_re * bi + coef_im * br


def _s5_prep(lam_re, lam_im, log_dt, b_re, b_im):
    g, p = lam_re.shape
    h = b_re.shape[-1]
    lr = lam_re.reshape(g, 1, p)
    li = lam_im.reshape(g, 1, p)
    ldt = jnp.broadcast_to(log_dt.reshape(g, 1, 1), (g, 1, p))
    br = jnp.transpose(b_re, (0, 2, 1))
    bi = jnp.transpose(b_im, (0, 2, 1))
    small = jax.ShapeDtypeStruct((g, 1, p), F32)
    big = jax.ShapeDtypeStruct((g, h, p), F32)
    return pl.pallas_call(_s5_prep_body, out_shape=[small, small, big, big], name="s5_prep")(lr, li, ldt, br, bi)


def _block_diag(m):
    g, r, c = m.shape
    idx = jnp.arange(g)
    return jnp.zeros((g, r, g, c), m.dtype).at[idx, :, idx, :].set(m).reshape(g * r, g * c)


def _s5_body(u_ref, b_ref, ar_ref, ai_ref, cre_ref, cim_ref, dsk_ref, wg_ref, bg_ref, y_ref,
             bu_ref, st_ref, *, steps):
    half_in = u_ref.shape[1] // 2
    half_st = st_ref.shape[1] // 4

    @pl.when(pl.program_id(0) == 0)
    def _():
        st_ref[...] = jnp.zeros_like(st_ref)

    for c in range(2):
        bu_ref[:, 2 * c * half_st:2 * (c + 1) * half_st] = jnp.dot(
            u_ref[:, c * half_in:(c + 1) * half_in], b_ref[c], preferred_element_type=F32)

    w = S5_COL_GROUP
    for c in range(2):
        for s in range(half_st // w):
            re0 = 2 * c * half_st + s * w
            im0 = re0 + half_st
            a0 = c * half_st + s * w
            ar = ar_ref[:, a0:a0 + w]
            ai = ai_ref[:, a0:a0 + w]

            def step(t, carry, re0=re0, im0=im0, ar=ar, ai=ai):
                xr, xi = carry
                row = pl.multiple_of(t * SUBLANES, SUBLANES)
                nxr = ar * xr - ai * xi + bu_ref[pl.ds(row, SUBLANES), re0:re0 + w]
                nxi = ar * xi + ai * xr + bu_ref[pl.ds(row, SUBLANES), im0:im0 + w]
                bu_ref[pl.ds(row, SUBLANES), re0:re0 + w] = nxr
                bu_ref[pl.ds(row, SUBLANES), im0:im0 + w] = nxi
                return nxr, nxi

            xr, xi = lax.fori_loop(0, steps, step,
                                   (st_ref[:, re0:re0 + w], st_ref[:, im0:im0 + w]), unroll=8)
            st_ref[:, re0:re0 + w] = xr
            st_ref[:, im0:im0 + w] = xi

    ys = []
    for c in range(2):
        xre = bu_ref[:, 2 * c * half_st:(2 * c + 1) * half_st].astype(BF16)
        xim = bu_ref[:, (2 * c + 1) * half_st:(2 * c + 2) * half_st].astype(BF16)
        ys.append(jnp.dot(xre, cre_ref[c], preferred_element_type=F32)
                  - jnp.dot(xim, cim_ref[c], preferred_element_type=F32))
    y = jnp.concatenate(ys, axis=1) + dsk_ref[...] * u_ref[...].astype(F32)
    v = jax.nn.gelu(y)
    gate = jnp.dot(v.astype(BF16), wg_ref[...], preferred_element_type=F32) + bg_ref[...]
    y_ref[...] = (v * jax.nn.sigmoid(gate)).astype(y_ref.dtype)


def _s5_mixer(u_tm, bsz, bmat, ar, ai, cre, cim, d_skip, w_glu, b_glu):
    rows, width = u_tm.shape
    seq = rows // bsz
    steps = min(S5_TIME_CHUNK, seq)
    r = steps * bsz
    n_state = ar.shape[1] * 2
    return pl.pallas_call(
        functools.partial(_s5_body, steps=steps),
        grid=(seq // steps,),
        in_specs=[pl.BlockSpec((r, width), lambda i: (i, 0)),
                  _resident(bmat.shape), _resident(ar.shape), _resident(ai.shape),
                  _resident(cre.shape), _resident(cim.shape), _resident(d_skip.shape),
                  _resident(w_glu.shape), _resident(b_glu.shape)],
        out_specs=pl.BlockSpec((r, width), lambda i: (i, 0)),
        out_shape=jax.ShapeDtypeStruct((rows, width), BF16),
        scratch_shapes=[pltpu.VMEM((r, n_state), F32), pltpu.VMEM((bsz, n_state), F32)],
        compiler_params=_params(("arbitrary",)),
        name="s5_mixer",
    )(u_tm, bmat, ar, ai, cre, cim, d_skip, w_glu, b_glu)


def _moba_tables(nb):
    per = MOBA_ITEMS
    items = [(i, jp, 0) for i in range(nb) for jp in range(i // 2)]
    n_past, rest = divmod(len(items), per)
    items += [(i, i // 2, 0 if i % 2 else 1) for i in range(nb)]
    n_groups, rest2 = divmod(len(items), per)
    assert nb % 2 == 0 and not rest and not rest2 and 2 <= n_past < n_groups
    tab = np.zeros((6, per * (n_groups + 2)), np.int32)
    for c in range(tab.shape[1]):
        if c < len(items):
            tab[0:3, c] = items[c]
        if 0 <= c - per < len(items):
            tab[3, c] = items[c - per][0]
        if 0 <= c - 2 * per < len(items):
            tab[4:6, c] = items[c - 2 * per][0:2]
    return tab, n_past, n_groups


def _moba_body(tab_ref, q_ref, k_ref, v_ref, o_ref, kx_ref, vt_ref, cb_ref, qx_ref,
               m_ref, acc_ref, s_ring, cm_ring, p_ring, al_ring, *, nb, n_past, n_groups):
    blk = MOBA_BLOCK
    pair = 2 * HEAD_DIM
    seq = nb * blk
    two = 2 * blk
    vrows = vt_ref.shape[2]

    @pl.when((pl.program_id(0) == 0) & (pl.program_id(1) == 0))
    def _():
        row = lax.broadcasted_iota(jnp.int32, (two, pair), 0)
        lane = lax.broadcasted_iota(jnp.int32, (two, pair), 1)
        ones_row = lax.broadcasted_iota(jnp.int32, (vrows - HEAD_DIM, two), 0) == 0
        for jp in range(nb // 2):
            block = jnp.where(row < blk, 2 * jp, 2 * jp + 1)
            kx_ref[jp, :, pair:] = jnp.where(lane == block, 1.0, 0.0).astype(BF16)
            for h in range(2):
                vt_ref[jp, h, HEAD_DIM:, :] = jnp.where(ones_row, 1.0, 0.0).astype(BF16)
        kpos = lax.broadcasted_iota(jnp.int32, (blk, blk), 0)
        qpos = lax.broadcasted_iota(jnp.int32, (blk, blk), 1)
        tri = jnp.where(kpos <= qpos, 0.0, NEG_INF)
        cb_ref[0] = jnp.concatenate([jnp.zeros((blk, blk), F32), tri], axis=0)
        cb_ref[1] = jnp.concatenate([tri, jnp.full((blk, blk), NEG_INF, F32)], axis=0)

    for jp in range(nb // 2):
        rows = slice(jp * two, (jp + 1) * two)
        kx_ref[jp, :, 0:pair] = k_ref[0, rows, :]
        vt = v_ref[0, rows, :].astype(F32).T.astype(BF16)
        for h in range(2):
            vt_ref[jp, h, 0:HEAD_DIM, :] = vt[h * HEAD_DIM:(h + 1) * HEAD_DIM, :]
    km = jnp.mean(k_ref[0].astype(F32).reshape(nb, blk, pair), axis=1).astype(BF16)

    q2 = q_ref[0]
    lane = lax.broadcasted_iota(jnp.int32, q2.shape, 1)
    nidx = lax.broadcasted_iota(jnp.int32, (nb, seq), 0)
    qblk = lax.shift_right_logical(lax.broadcasted_iota(jnp.int32, (nb, seq), 1), blk.bit_length() - 1)
    for h in range(2):
        head = (lane < HEAD_DIM) if h == 0 else (lane >= HEAD_DIM)
        qh = jnp.where(head, q2, jnp.zeros_like(q2))
        g = lax.dot_general(km, qh, _NT, preferred_element_type=F32)
        g = jnp.where(nidx < qblk, g, NEG_INF)
        picked = jnp.zeros((nb, seq), jnp.int32)
        for _ in range(min(MOBA_TOPK, nb - 1)):
            top = jnp.max(g, axis=0, keepdims=True)
            first = jnp.min(jnp.where(g == top, nidx, nb), axis=0, keepdims=True)
            hit = nidx == first
            picked = jnp.where(hit, 1, picked)
            g = jnp.where(hit, TAKEN, g)
        allow = ((picked == 1) & (nidx < qblk)) | (nidx == qblk)
        bias = jnp.where(allow, 0.0, NEG_INF)
        bias = jnp.concatenate([bias, jnp.zeros((pair - nb, seq), F32)], axis=0)
        qs = (qh.astype(F32) * (HEAD_DIM ** -0.5 * LOG2_E)).T.astype(BF16)
        bias = bias.astype(BF16)
        for i in range(nb):
            qx_ref[i, 0:pair, h * blk:(h + 1) * blk] = qs[:, i * blk:(i + 1) * blk]
            qx_ref[i, pair:, h * blk:(h + 1) * blk] = bias[:, i * blk:(i + 1) * blk]

    m_ref[...] = jnp.full(m_ref.shape, NEG_INF, F32)
    acc_ref[...] = jnp.zeros_like(acc_ref)

    def step(k, carry, own_block, stages=(1, 2, 3)):
        for e in range(MOBA_ITEMS):
            c = MOBA_ITEMS * k + e
            for h in range(2):
                r = 2 * e + h
                if 3 in stages:
                    a3 = 2 * tab_ref[4, c] + h
                    pv = jnp.dot(vt_ref[tab_ref[5, c], h], p_ring[r], preferred_element_type=F32)
                    acc_ref[a3] = al_ring[r] * acc_ref[a3] + pv
                if 2 in stages:
                    a2 = 2 * tab_ref[3, c] + h
                    m_old = m_ref[a2]
                    m_new = jnp.maximum(m_old, cm_ring[r])
                    m_ref[a2] = m_new
                    al_ring[r] = jnp.exp2(m_old - m_new)
                    p_ring[r] = jnp.exp2(s_ring[r] - m_new).astype(BF16)
            if 1 in stages:
                s2 = jnp.dot(kx_ref[tab_ref[1, c]], qx_ref[tab_ref[0, c]], preferred_element_type=F32)
                for h in range(2):
                    s = s2[:, h * blk:(h + 1) * blk]
                    if own_block:
                        s = s + cb_ref[tab_ref[2, c]]
                    s_ring[2 * e + h] = s
                    cm_ring[2 * e + h] = jnp.max(s, axis=0, keepdims=True)
        return carry

    step(0, 0, own_block=False, stages=(1,))
    step(1, 0, own_block=False, stages=(1, 2))
    lax.fori_loop(2, n_past, functools.partial(step, own_block=False), 0)
    lax.fori_loop(n_past, n_groups, functools.partial(step, own_block=True), 0)
    step(n_groups, 0, own_block=False, stages=(2, 3))
    step(n_groups + 1, 0, own_block=False, stages=(3,))

    for i in range(nb):
        o = jnp.concatenate([acc_ref[2 * i + h, 0:HEAD_DIM, :] / acc_ref[2 * i + h, HEAD_DIM:HEAD_DIM + 1, :]
                             for h in range(2)], axis=0)
        o_ref[0, i * blk:(i + 1) * blk, :] = o.T.astype(o_ref.dtype)


def _moba(q, k, v):
    bsz, seq, width = q.shape
    blk = MOBA_BLOCK
    nb = seq // blk
    pair = 2 * HEAD_DIM
    tab, n_past, n_groups = _moba_tables(nb)
    spec = pl.BlockSpec((1, seq, pair), lambda b, hp, tab_ref: (b, 0, hp))
    n_state = 2 * nb
    n_ring = 2 * MOBA_ITEMS
    vrows = HEAD_DIM + BF16_SUBLANES
    return pl.pallas_call(
        functools.partial(_moba_body, nb=nb, n_past=n_past, n_groups=n_groups),
        grid_spec=pltpu.PrefetchScalarGridSpec(
            num_scalar_prefetch=1,
            grid=(bsz, width // pair),
            in_specs=[spec, spec, spec],
            out_specs=spec,
            scratch_shapes=[
                pltpu.VMEM((nb // 2, 2 * blk, 2 * pair), BF16),
                pltpu.VMEM((nb // 2, 2, vrows, 2 * blk), BF16),
                pltpu.VMEM((2, 2 * blk, blk), F32),
                pltpu.VMEM((nb, 2 * pair, 2 * blk), BF16),
                pltpu.VMEM((n_state, 1, blk), F32),
                pltpu.VMEM((n_state, vrows, blk), F32),
                pltpu.VMEM((n_ring, 2 * blk, blk), F32),
                pltpu.VMEM((n_ring, 1, blk), F32),
                pltpu.VMEM((n_ring, 2 * blk, blk), BF16),
                pltpu.VMEM((n_ring, 1, blk), F32),
            ]),
        out_shape=jax.ShapeDtypeStruct((bsz, seq, width), BF16),
        compiler_params=_params(("arbitrary", "arbitrary")),
        name="moba",
    )(jnp.asarray(tab), q, k, v)


def _post_body(x_ref, ya_ref, yb_ref, ga_ref, gb_ref, wa_ref, wb_ref, wo_ref, gpm_ref, gpf_ref,
               w1_ref, w2_ref, gpo_ref, o_ref):
    a = jnp.dot(ya_ref[...], wa_ref[...], preferred_element_type=F32)
    b = jnp.dot(yb_ref[...], wb_ref[...], preferred_element_type=F32)
    merged = (jax.nn.sigmoid(ga_ref[...].astype(F32)) * a
              + jax.nn.sigmoid(gb_ref[...].astype(F32)) * b)
    mix = jnp.dot(merged.astype(BF16), wo_ref[...], preferred_element_type=F32)
    h = x_ref[...] + _rms(mix, gpm_ref[...])
    f = _rms(h, gpf_ref[...]).astype(BF16)
    d_ff = w1_ref.shape[1]
    ck = min(FFN_CHUNK, d_ff)
    acc = jnp.zeros(h.shape, F32)
    for c in range(d_ff // ck):
        t = jnp.dot(f, w1_ref[:, c * ck:(c + 1) * ck], preferred_element_type=F32)
        t = jnp.square(jnp.maximum(t, 0.0)).astype(BF16)
        acc = acc + jnp.dot(t, w2_ref[c * ck:(c + 1) * ck, :], preferred_element_type=F32)
    o_ref[...] = h + _rms(acc, gpo_ref[...])


def _post(x2, ya, yb, ga, gb, wa, wb, wo, gpm, gpf, w1, w2, gpo):
    t, d = x2.shape
    tm = min(TOKEN_TILE, t)

    def rows(a):
        return pl.BlockSpec((tm, a.shape[1]), lambda i: (i, 0))

    consts = (wa, wb, wo, gpm, gpf, w1, w2, gpo)
    return pl.pallas_call(
        _post_body,
        grid=(t // tm,),
        in_specs=[rows(a) for a in (x2, ya, yb, ga, gb)] + [_resident(c.shape) for c in consts],
        out_specs=rows(x2),
        out_shape=jax.ShapeDtypeStruct((t, d), x2.dtype),
        compiler_params=_params(("parallel",)),
        name="post",
    )(x2, ya, yb, ga, gb, *consts)


def _layer(h, g_pre_mix, w_in, lam_re, lam_im, log_dt, b_re, b_im, c_re, c_im, d_skip, w_glu, b_glu,
           w_branch_a, w_branch_b, w_out, g_post_mix, g_pre_ffn, w_ff1, w_ff2, g_post_ffn):
    bsz, seq, d = h.shape
    ssm_w = lam_re.shape[0] * SSM_GROUP
    attn_w = ATTN_HEADS * HEAD_DIM
    x2 = h.reshape(bsz * seq, d)

    zs, q, k, v, ga, gb = _inproj(x2, g_pre_mix.reshape(1, d), w_in.astype(BF16),
                                  (ssm_w, attn_w, attn_w, attn_w, d, d))

    ar, ai, bbr, bbi = _s5_prep(lam_re, lam_im, log_dt, b_re, b_im)
    g, p = lam_re.shape
    hg = g // 2
    n_half = hg * p

    def halves(m):
        return jnp.stack([_block_diag(m[:hg]), _block_diag(m[hg:])])

    bmat = jnp.concatenate([halves(bbr), halves(bbi)], axis=2).astype(BF16)
    cre = halves(jnp.transpose(c_re, (0, 2, 1))).astype(BF16)
    cim = halves(jnp.transpose(c_im, (0, 2, 1))).astype(BF16)
    ar8 = jnp.broadcast_to(ar.reshape(1, 2 * n_half), (bsz, 2 * n_half))
    ai8 = jnp.broadcast_to(ai.reshape(1, 2 * n_half), (bsz, 2 * n_half))

    u_tm = jnp.transpose(zs.reshape(bsz, seq, ssm_w), (1, 0, 2)).reshape(seq * bsz, ssm_w)
    ya_tm = _s5_mixer(u_tm, bsz, bmat, ar8, ai8, cre, cim, d_skip.reshape(1, ssm_w),
                      w_glu.astype(BF16), b_glu.reshape(1, ssm_w))
    ya = jnp.transpose(ya_tm.reshape(seq, bsz, ssm_w), (1, 0, 2)).reshape(bsz * seq, ssm_w)

    shape3 = (bsz, seq, attn_w)
    yb = _moba(q.reshape(shape3), k.reshape(shape3), v.reshape(shape3)).reshape(bsz * seq, attn_w)

    out = _post(x2, ya, yb, ga, gb, w_branch_a.astype(BF16), w_branch_b.astype(BF16),
                w_out.astype(BF16), g_post_mix.reshape(1, d), g_pre_ffn.reshape(1, d),
                w_ff1.astype(BF16), w_ff2.astype(BF16), g_post_ffn.reshape(1, d))
    return out.reshape(bsz, seq, d)


def kernel(x, g_pre_mix, w_in, lam_re, lam_im, log_dt, b_re, b_im, c_re, c_im, d_skip, w_glu, b_glu,
           w_branch_a, w_branch_b, w_out, g_post_mix, g_pre_ffn, w_ff1, w_ff2, g_post_ffn):
    per_layer = (g_pre_mix, w_in, lam_re, lam_im, log_dt, b_re, b_im, c_re, c_im, d_skip, w_glu, b_glu,
                 w_branch_a, w_branch_b, w_out, g_post_mix, g_pre_ffn, w_ff1, w_ff2, g_post_ffn)
    h = x
    for l in range(g_pre_mix.shape[0]):
        h = _layer(h, *(p[l] for p in per_layer))
    return h
```

```python
import functools

import jax
import jax.numpy as jnp
import numpy as np
from jax import lax
from jax.experimental import pallas as pl
from jax.experimental.pallas import tpu as pltpu

F32 = jnp.float32
BF16 = jnp.bfloat16

RMS_EPS = 1e-6
NEG_INF = -1e30
TAKEN = -3e38

SSM_GROUP = 16
SSM_STATE = 64
ATTN_HEADS = 8
HEAD_DIM = 64
MOBA_BLOCK = 256
MOBA_TOPK = 3
MOBA_ITEMS = 4

LOG2_E = 1.4426950408889634

SUBLANES = 8
BF16_SUBLANES = 16
LANES = 128
VMEM_LIMIT_BYTES = 56 * 1024 * 1024

TOKEN_TILE = 512
S5_TIME_CHUNK = 64
S5_COL_GROUP = 512
FFN_CHUNK = 1024

_NT = (((1,), (1,)), ((), ()))


def _params(sem):
    return pltpu.CompilerParams(dimension_semantics=sem, vmem_limit_bytes=VMEM_LIMIT_BYTES)


def _resident(shape):
    zeros = (0,) * len(shape)
    return pl.BlockSpec(shape, lambda *_: zeros, pipeline_mode=pl.Buffered(1))


def _rms(x, g):
    var = jnp.mean(x * x, axis=-1, keepdims=True)
    return (x * lax.rsqrt(var + RMS_EPS)) * g


def _inproj_body(x_ref, g_ref, w_ref, *out_refs):
    ub = _rms(x_ref[...], g_ref[...]).astype(BF16)
    col = 0
    for ref in out_refs:
        n = ref.shape[-1]
        ref[...] = jnp.dot(ub, w_ref[:, col:col + n], preferred_element_type=F32).astype(ref.dtype)
        col += n


def _inproj(x2, g, w, widths):
    t, d = x2.shape
    tm = min(TOKEN_TILE, t)
    return pl.pallas_call(
        _inproj_body,
        grid=(t // tm,),
        in_specs=[pl.BlockSpec((tm, d), lambda i: (i, 0)), _resident(g.shape), _resident(w.shape)],
        out_specs=[pl.BlockSpec((tm, n), lambda i: (i, 0)) for n in widths],
        out_shape=[jax.ShapeDtypeStruct((t, n), BF16) for n in widths],
        compiler_params=_params(("parallel",)),
        name="inproj",
    )(x2, g, w)


def _s5_prep_body(lr_ref, li_ref, ldt_ref, br_ref, bi_ref, ar_ref, ai_ref, bbr_ref, bbi_ref):
    lr = lr_ref[...]
    li = li_ref[...]
    dt = jnp.exp(ldt_ref[...])
    mag = jnp.exp(lr * dt)
    ab_re = mag * jnp.cos(li * dt)
    ab_im = mag * jnp.sin(li * dt)
    nr = ab_re - 1.0
    ni = ab_im
    den = lr * lr + li * li
    coef_re = (nr * lr + ni * li) / den
    coef_im = (ni * lr - nr * li) / den
    br = br_ref[...]
    bi = bi_ref[...]
    ar_ref[...] = ab_re
    ai_ref[...] = ab_im
    bbr_ref[...] = coef_re * br - coef_im * bi
    bbi_ref[...] = coef_re * bi + coef_im * br


def _s5_prep(lam_re, lam_im, log_dt, b_re, b_im):
    g, p = lam_re.shape
    h = b_re.shape[-1]
    lr = lam_re.reshape(g, 1, p)
    li = lam_im.reshape(g, 1, p)
    ldt = jnp.broadcast_to(log_dt.reshape(g, 1, 1), (g, 1, p))
    br = jnp.transpose(b_re, (0, 2, 1))
    bi = jnp.transpose(b_im, (0, 2, 1))
    small = jax.ShapeDtypeStruct((g, 1, p), F32)
    big = jax.ShapeDtypeStruct((g, h, p), F32)
    return pl.pallas_call(_s5_prep_body, out_shape=[small, small, big, big], name="s5_prep")(lr, li, ldt, br, bi)


def _block_diag(m):
    g, r, c = m.shape
    idx = jnp.arange(g)
    return jnp.zeros((g, r, g, c), m.dtype).at[idx, :, idx, :].set(m).reshape(g * r, g * c)


def _s5_body(u_ref, b_ref, ar_ref, ai_ref, cre_ref, cim_ref, dsk_ref, wg_ref, bg_ref, y_ref,
             bu_ref, st_ref, *, steps):
    half_in = u_ref.shape[1] // 2
    half_st = st_ref.shape[1] // 4

    @pl.when(pl.program_id(0) == 0)
    def _():
        st_ref[...] = jnp.zeros_like(st_ref)

    for c in range(2):
        bu_ref[:, 2 * c * half_st:2 * (c + 1) * half_st] = jnp.dot(
            u_ref[:, c * half_in:(c + 1) * half_in], b_ref[c], preferred_element_type=F32)

    w = S5_COL_GROUP
    for c in range(2):
        for s in range(half_st // w):
            re0 = 2 * c * half_st + s * w
            im0 = re0 + half_st
            a0 = c * half_st + s * w
            ar = ar_ref[:, a0:a0 + w]
            ai = ai_ref[:, a0:a0 + w]
            xr = st_ref[:, re0:re0 + w]
            xi = st_ref[:, im0:im0 + w]
            for t in range(steps):
                rows = slice(t * SUBLANES, (t + 1) * SUBLANES)
                xr, xi = (ar * xr - ai * xi + bu_ref[rows, re0:re0 + w],
                          ar * xi + ai * xr + bu_ref[rows, im0:im0 + w])
                bu_ref[rows, re0:re0 + w] = xr
                bu_ref[rows, im0:im0 + w] = xi
            st_ref[:, re0:re0 + w] = xr
            st_ref[:, im0:im0 + w] = xi

    ys = []
    for c in range(2):
        xre = bu_ref[:, 2 * c * half_st:(2 * c + 1) * half_st].astype(BF16)
        xim = bu_ref[:, (2 * c + 1) * half_st:(2 * c + 2) * half_st].astype(BF16)
        ys.append(jnp.dot(xre, cre_ref[c], preferred_element_type=F32)
                  - jnp.dot(xim, cim_ref[c], preferred_element_type=F32))
    y = jnp.concatenate(ys, axis=1) + dsk_ref[...] * u_ref[...].astype(F32)
    v = jax.nn.gelu(y)
    gate = jnp.dot(v.astype(BF16), wg_ref[...], preferred_element_type=F32) + bg_ref[...]
    y_ref[...] = (v * jax.nn.sigmoid(gate)).astype(y_ref.dtype)


def _s5_mixer(u_tm, bsz, bmat, ar, ai, cre, cim, d_skip, w_glu, b_glu):
    rows, width = u_tm.shape
    seq = rows // bsz
    steps = min(S5_TIME_CHUNK, seq)
    r = steps * bsz
    n_state = ar.shape[1] * 2
    return pl.pallas_call(
        functools.partial(_s5_body, steps=steps),
        grid=(seq // steps,),
        in_specs=[pl.BlockSpec((r, width), lambda i: (i, 0)),
                  _resident(bmat.shape), _resident(ar.shape), _resident(ai.shape),
                  _resident(cre.shape), _resident(cim.shape), _resident(d_skip.shape),
                  _resident(w_glu.shape), _resident(b_glu.shape)],
        out_specs=pl.BlockSpec((r, width), lambda i: (i, 0)),
        out_shape=jax.ShapeDtypeStruct((rows, width), BF16),
        scratch_shapes=[pltpu.VMEM((r, n_state), F32), pltpu.VMEM((bsz, n_state), F32)],
        compiler_params=_params(("arbitrary",)),
        name="s5_mixer",
    )(u_tm, bmat, ar, ai, cre, cim, d_skip, w_glu, b_glu)


def _moba_tables(nb):
    per = MOBA_ITEMS
    items = [(i, jp, 0) for i in range(nb) for jp in range(i // 2)]
    n_past, rest = divmod(len(items), per)
    items += [(i, i // 2, 0 if i % 2 else 1) for i in range(nb)]
    n_groups, rest2 = divmod(len(items), per)
    assert nb % 2 == 0 and not rest and not rest2 and 2 <= n_past < n_groups
    tab = np.zeros((6, per * (n_groups + 2)), np.int32)
    for c in range(tab.shape[1]):
        if c < len(items):
            tab[0:3, c] = items[c]
        if 0 <= c - per < len(items):
            tab[3, c] = items[c - per][0]
        if 0 <= c - 2 * per < len(items):
            tab[4:6, c] = items[c - 2 * per][0:2]
    return tab, n_past, n_groups


def _moba_body(tab_ref, q_ref, k_ref, v_ref, o_ref, kx_ref, vt_ref, cb_ref, qx_ref,
               m_ref, acc_ref, s_ring, cm_ring, p_ring, al_ring, *, nb, n_past, n_groups):
    blk = MOBA_BLOCK
    pair = 2 * HEAD_DIM
    seq = nb * blk
    two = 2 * blk
    vrows = vt_ref.shape[2]

    @pl.when((pl.program_id(0) == 0) & (pl.program_id(1) == 0))
    def _():
        row = lax.broadcasted_iota(jnp.int32, (two, pair), 0)
        lane = lax.broadcasted_iota(jnp.int32, (two, pair), 1)
        ones_row = lax.broadcasted_iota(jnp.int32, (vrows - HEAD_DIM, two), 0) == 0
        for jp in range(nb // 2):
            block = jnp.where(row < blk, 2 * jp, 2 * jp + 1)
            kx_ref[jp, :, pair:] = jnp.where(lane == block, 1.0, 0.0).astype(BF16)
            for h in range(2):
                vt_ref[jp, h, HEAD_DIM:, :] = jnp.where(ones_row, 1.0, 0.0).astype(BF16)
        kpos = lax.broadcasted_iota(jnp.int32, (blk, blk), 0)
        qpos = lax.broadcasted_iota(jnp.int32, (blk, blk), 1)
        tri = jnp.where(kpos <= qpos, 0.0, NEG_INF)
        cb_ref[0] = jnp.concatenate([jnp.zeros((blk, blk), F32), tri], axis=0)
        cb_ref[1] = jnp.concatenate([tri, jnp.full((blk, blk), NEG_INF, F32)], axis=0)

    for jp in range(nb // 2):
        rows = slice(jp * two, (jp + 1) * two)
        kx_ref[jp, :, 0:pair] = k_ref[0, rows, :]
        vt = v_ref[0, rows, :].astype(F32).T.astype(BF16)
        for h in range(2):
            vt_ref[jp, h, 0:HEAD_DIM, :] = vt[h * HEAD_DIM:(h + 1) * HEAD_DIM, :]
    km = jnp.mean(k_ref[0].astype(F32).reshape(nb, blk, pair), axis=1).astype(BF16)

    q2 = q_ref[0]
    lane = lax.broadcasted_iota(jnp.int32, q2.shape, 1)
    nidx = lax.broadcasted_iota(jnp.int32, (nb, seq), 0)
    qblk = lax.shift_right_logical(lax.broadcasted_iota(jnp.int32, (nb, seq), 1), blk.bit_length() - 1)
    for h in range(2):
        head = (lane < HEAD_DIM) if h == 0 else (lane >= HEAD_DIM)
        qh = jnp.where(head, q2, jnp.zeros_like(q2))
        g = lax.dot_general(km, qh, _NT, preferred_element_type=F32)
        g = jnp.where(nidx < qblk, g, NEG_INF)
        picked = jnp.zeros((nb, seq), jnp.int32)
        for _ in range(min(MOBA_TOPK, nb - 1)):
            top = jnp.max(g, axis=0, keepdims=True)
            first = jnp.min(jnp.where(g == top, nidx, nb), axis=0, keepdims=True)
            hit = nidx == first
            picked = jnp.where(hit, 1, picked)
            g = jnp.where(hit, TAKEN, g)
        allow = ((picked == 1) & (nidx < qblk)) | (nidx == qblk)
        bias = jnp.where(allow, 0.0, NEG_INF)
        bias = jnp.concatenate([bias, jnp.zeros((pair - nb, seq), F32)], axis=0)
        qs = (qh.astype(F32) * (HEAD_DIM ** -0.5 * LOG2_E)).T.astype(BF16)
        bias = bias.astype(BF16)
        for i in range(nb):
            qx_ref[i, 0:pair, h * blk:(h + 1) * blk] = qs[:, i * blk:(i + 1) * blk]
            qx_ref[i, pair:, h * blk:(h + 1) * blk] = bias[:, i * blk:(i + 1) * blk]

    m_ref[...] = jnp.full(m_ref.shape, NEG_INF, F32)
    acc_ref[...] = jnp.zeros_like(acc_ref)

    def step(k, carry, own_block, stages=(1, 2, 3)):
        for e in range(MOBA_ITEMS):
            c = MOBA_ITEMS * k + e
            for h in range(2):
                r = 2 * e + h
                if 3 in stages:
                    a3 = 2 * tab_ref[4, c] + h
                    pv = jnp.dot(vt_ref[tab_ref[5, c], h], p_ring[r], preferred_element_type=F32)
                    acc_ref[a3] = al_ring[r] * acc_ref[a3] + pv
                if 2 in stages:
                    a2 = 2 * tab_ref[3, c] + h
                    m_old = m_ref[a2]
                    m_new = jnp.maximum(m_old, cm_ring[r])
                    m_ref[a2] = m_new
                    al_ring[r] = jnp.exp2(m_old - m_new)
                    p_ring[r] = jnp.exp2(s_ring[r] - m_new).astype(BF16)
            if 1 in stages:
                s2 = jnp.dot(kx_ref[tab_ref[1, c]], qx_ref[tab_ref[0, c]], preferred_element_type=F32)
                for h in range(2):
                    s = s2[:, h * blk:(h + 1) * blk]
                    if own_block:
                        s = s + cb_ref[tab_ref[2, c]]
                    s_ring[2 * e + h] = s
                    cm_ring[2 * e + h] = jnp.max(s, axis=0, keepdims=True)
        return carry

    step(0, 0, own_block=False, stages=(1,))
    step(1, 0, own_block=False, stages=(1, 2))
    lax.fori_loop(2, n_past, functools.partial(step, own_block=False), 0)
    lax.fori_loop(n_past, n_groups, functools.partial(step, own_block=True), 0)
    step(n_groups, 0, own_block=False, stages=(2, 3))
    step(n_groups + 1, 0, own_block=False, stages=(3,))

    for i in range(nb):
        o = jnp.concatenate([acc_ref[2 * i + h, 0:HEAD_DIM, :] / acc_ref[2 * i + h, HEAD_DIM:HEAD_DIM + 1, :]
                             for h in range(2)], axis=0)
        o_ref[0, i * blk:(i + 1) * blk, :] = o.T.astype(o_ref.dtype)


def _moba(q, k, v):
    bsz, seq, width = q.shape
    blk = MOBA_BLOCK
    nb = seq // blk
    pair = 2 * HEAD_DIM
    tab, n_past, n_groups = _moba_tables(nb)
    spec = pl.BlockSpec((1, seq, pair), lambda b, hp, tab_ref: (b, 0, hp))
    n_state = 2 * nb
    n_ring = 2 * MOBA_ITEMS
    vrows = HEAD_DIM + BF16_SUBLANES
    return pl.pallas_call(
        functools.partial(_moba_body, nb=nb, n_past=n_past, n_groups=n_groups),
        grid_spec=pltpu.PrefetchScalarGridSpec(
            num_scalar_prefetch=1,
            grid=(bsz, width // pair),
            in_specs=[spec, spec, spec],
            out_specs=spec,
            scratch_shapes=[
                pltpu.VMEM((nb // 2, 2 * blk, 2 * pair), BF16),
                pltpu.VMEM((nb // 2, 2, vrows, 2 * blk), BF16),
                pltpu.VMEM((2, 2 * blk, blk), F32),
                pltpu.VMEM((nb, 2 * pair, 2 * blk), BF16),
                pltpu.VMEM((n_state, 1, blk), F32),
                pltpu.VMEM((n_state, vrows, blk), F32),
                pltpu.VMEM((n_ring, 2 * blk, blk), F32),
                pltpu.VMEM((n_ring, 1, blk), F32),
                pltpu.VMEM((n_ring, 2 * blk, blk), BF16),
                pltpu.VMEM((n_ring, 1, blk), F32),
            ]),
        out_shape=jax.ShapeDtypeStruct((bsz, seq, width), BF16),
        compiler_params=_params(("arbitrary", "arbitrary")),
        name="moba",
    )(jnp.asarray(tab), q, k, v)


def _post_body(x_ref, ya_ref, yb_ref, ga_ref, gb_ref, wa_ref, wb_ref, wo_ref, gpm_ref, gpf_ref,
               w1_ref, w2_ref, gpo_ref, o_ref):
    a = jnp.dot(ya_ref[...], wa_ref[...], preferred_element_type=F32)
    b = jnp.dot(yb_ref[...], wb_ref[...], preferred_element_type=F32)
    merged = (jax.nn.sigmoid(ga_ref[...].astype(F32)) * a
              + jax.nn.sigmoid(gb_ref[...].astype(F32)) * b)
    mix = jnp.dot(merged.astype(BF16), wo_ref[...], preferred_element_type=F32)
    h = x_ref[...] + _rms(mix, gpm_ref[...])
    f = _rms(h, gpf_ref[...]).astype(BF16)
    d_ff = w1_ref.shape[1]
    ck = min(FFN_CHUNK, d_ff)
    acc = jnp.zeros(h.shape, F32)
    for c in range(d_ff // ck):
        t = jnp.dot(f, w1_ref[:, c * ck:(c + 1) * ck], preferred_element_type=F32)
        t = jnp.square(jnp.maximum(t, 0.0)).astype(BF16)
        acc = acc + jnp.dot(t, w2_ref[c * ck:(c + 1) * ck, :], preferred_element_type=F32)
    o_ref[...] = h + _rms(acc, gpo_ref[...])


def _post(x2, ya, yb, ga, gb, wa, wb, wo, gpm, gpf, w1, w2, gpo):
    t, d = x2.shape
    tm = min(TOKEN_TILE, t)

    def rows(a):
        return pl.BlockSpec((tm, a.shape[1]), lambda i: (i, 0))

    consts = (wa, wb, wo, gpm, gpf, w1, w2, gpo)
    return pl.pallas_call(
        _post_body,
        grid=(t // tm,),
        in_specs=[rows(a) for a in (x2, ya, yb, ga, gb)] + [_resident(c.shape) for c in consts],
        out_specs=rows(x2),
        out_shape=jax.ShapeDtypeStruct((t, d), x2.dtype),
        compiler_params=_params(("parallel",)),
        name="post",
    )(x2, ya, yb, ga, gb, *consts)


def _layer(h, g_pre_mix, w_in, lam_re, lam_im, log_dt, b_re, b_im, c_re, c_im, d_skip, w_glu, b_glu,
           w_branch_a, w_branch_b, w_out, g_post_mix, g_pre_ffn, w_ff1, w_ff2, g_post_ffn):
    bsz, seq, d = h.shape
    ssm_w = lam_re.shape[0] * SSM_GROUP
    attn_w = ATTN_HEADS * HEAD_DIM
    x2 = h.reshape(bsz * seq, d)

    zs, q, k, v, ga, gb = _inproj(x2, g_pre_mix.reshape(1, d), w_in.astype(BF16),
                                  (ssm_w, attn_w, attn_w, attn_w, d, d))

    ar, ai, bbr, bbi = _s5_prep(lam_re, lam_im, log_dt, b_re, b_im)
    g, p = lam_re.shape
    hg = g // 2
    n_half = hg * p

    def halves(m):
        return jnp.stack([_block_diag(m[:hg]), _block_diag(m[hg:])])

    bmat = jnp.concatenate([halves(bbr), halves(bbi)], axis=2).astype(BF16)
    cre = halves(jnp.transpose(c_re, (0, 2, 1))).astype(BF16)
    cim = halves(jnp.transpose(c_im, (0, 2, 1))).astype(BF16)
    ar8 = jnp.broadcast_to(ar.reshape(1, 2 * n_half), (bsz, 2 * n_half))
    ai8 = jnp.broadcast_to(ai.reshape(1, 2 * n_half), (bsz, 2 * n_half))

    u_tm = jnp.transpose(zs.reshape(bsz, seq, ssm_w), (1, 0, 2)).reshape(seq * bsz, ssm_w)
    ya_tm = _s5_mixer(u_tm, bsz, bmat, ar8, ai8, cre, cim, d_skip.reshape(1, ssm_w),
                      w_glu.astype(BF16), b_glu.reshape(1, ssm_w))
    ya = jnp.transpose(ya_tm.reshape(seq, bsz, ssm_w), (1, 0, 2)).reshape(bsz * seq, ssm_w)

    shape3 = (bsz, seq, attn_w)
    yb = _moba(q.reshape(shape3), k.reshape(shape3), v.reshape(shape3)).reshape(bsz * seq, attn_w)

    out = _post(x2, ya, yb, ga, gb, w_branch_a.astype(BF16), w_branch_b.astype(BF16),
                w_out.astype(BF16), g_post_mix.reshape(1, d), g_pre_ffn.reshape(1, d),
                w_ff1.astype(BF16), w_ff2.astype(BF16), g_post_ffn.reshape(1, d))
    return out.reshape(bsz, seq, d)


def kernel(x, g_pre_mix, w_in, lam_re, lam_im, log_dt, b_re, b_im, c_re, c_im, d_skip, w_glu, b_glu,
           w_branch_a, w_branch_b, w_out, g_post_mix, g_pre_ffn, w_ff1, w_ff2, g_post_ffn):
    per_layer = (g_pre_mix, w_in, lam_re, lam_im, log_dt, b_re, b_im, c_re, c_im, d_skip, w_glu, b_glu,
                 w_branch_a, w_branch_b, w_out, g_post_mix, g_pre_ffn, w_ff1, w_ff2, g_post_ffn)
    h = x
    for l in range(g_pre_mix.shape[0]):
        h = _layer(h, *(p[l] for p in per_layer))
    return h
```

```python
import functools

import jax
import jax.numpy as jnp
import numpy as np
from jax import lax
from jax.experimental import pallas as pl
from jax.experimental.pallas import tpu as pltpu

F32 = jnp.float32
BF16 = jnp.bfloat16

RMS_EPS = 1e-6
NEG_INF = -1e30
TAKEN = -3e38

SSM_GROUP = 16
SSM_STATE = 64
ATTN_HEADS = 8
HEAD_DIM = 64
MOBA_BLOCK = 256
MOBA_TOPK = 3
MOBA_ITEMS = 4

LOG2_E = 1.4426950408889634

SUBLANES = 8
BF16_SUBLANES = 16
LANES = 128
VMEM_LIMIT_BYTES = 56 * 1024 * 1024

TOKEN_TILE = 512
S5_TIME_CHUNK = 64
S5_COL_GROUP = 512
FFN_CHUNK = 1024

_NT = (((1,), (1,)), ((), ()))


def _params(sem):
    return pltpu.CompilerParams(dimension_semantics=sem, vmem_limit_bytes=VMEM_LIMIT_BYTES)


def _resident(shape):
    zeros = (0,) * len(shape)
    return pl.BlockSpec(shape, lambda *_: zeros, pipeline_mode=pl.Buffered(1))


def _rms(x, g):
    var = jnp.mean(x * x, axis=-1, keepdims=True)
    return (x * lax.rsqrt(var + RMS_EPS)) * g


def _inproj_body(x_ref, g_ref, w_ref, zs_ref, *out_refs):
    bsz, tl, d = x_ref.shape
    ub = _rms(x_ref[...].reshape(bsz * tl, d), g_ref[...]).astype(BF16)
    col = 0
    for ref in (zs_ref,) + out_refs:
        n = ref.shape[-1]
        z = jnp.dot(ub, w_ref[:, col:col + n], preferred_element_type=F32).astype(ref.dtype)
        z = z.reshape(bsz, tl, n)
        ref[...] = jnp.swapaxes(z, 0, 1) if ref is zs_ref else z
        col += n


def _inproj(x, g, w, widths):
    bsz, seq, d = x.shape
    tl = min(TOKEN_TILE // bsz, seq)
    return pl.pallas_call(
        _inproj_body,
        grid=(seq // tl,),
        in_specs=[pl.BlockSpec((bsz, tl, d), lambda i: (0, i, 0)), _resident(g.shape), _resident(w.shape)],
        out_specs=[pl.BlockSpec((tl, bsz, widths[0]), lambda i: (i, 0, 0))]
        + [pl.BlockSpec((bsz, tl, n), lambda i: (0, i, 0)) for n in widths[1:]],
        out_shape=[jax.ShapeDtypeStruct((seq, bsz, widths[0]), BF16)]
        + [jax.ShapeDtypeStruct((bsz, seq, n), BF16) for n in widths[1:]],
        compiler_params=_params(("parallel",)),
        name="inproj",
    )(x, g, w)


def _s5_prep_body(lr_ref, li_ref, ldt_ref, br_ref, bi_ref, ar_ref, ai_ref, bbr_ref, bbi_ref):
    lr = lr_ref[...]
    li = li_ref[...]
    dt = jnp.exp(ldt_ref[...])
    mag = jnp.exp(lr * dt)
    ab_re = mag * jnp.cos(li * dt)
    ab_im = mag * jnp.sin(li * dt)
    nr = ab_re - 1.0
    ni = ab_im
    den = lr * lr + li * li
    coef_re = (nr * lr + ni * li) / den
    coef_im = (ni * lr - nr * li) / den
    br = br_ref[...]
    bi = bi_ref[...]
    ar_ref[...] = ab_re
    ai_ref[...] = ab_im
    bbr_ref[...] = coef_re * br - coef_im * bi
    bbi_ref[...] = coef_re * bi + coef_im * br


def _s5_prep(lam_re, lam_im, log_dt, b_re, b_im):
    g, p = lam_re.shape
    h = b_re.shape[-1]
    lr = lam_re.reshape(g, 1, p)
    li = lam_im.reshape(g, 1, p)
    ldt = jnp.broadcast_to(log_dt.reshape(g, 1, 1), (g, 1, p))
    br = jnp.transpose(b_re, (0, 2, 1))
    bi = jnp.transpose(b_im, (0, 2, 1))
    small = jax.ShapeDtypeStruct((g, 1, p), F32)
    big = jax.ShapeDtypeStruct((g, h, p), F32)
    return pl.pallas_call(_s5_prep_body, out_shape=[small, small, big, big], name="s5_prep")(lr, li, ldt, br, bi)


def _block_diag(m):
    g, r, c = m.shape
    idx = jnp.arange(g)
    return jnp.zeros((g, r, g, c), m.dtype).at[idx, :, idx, :].set(m).reshape(g * r, g * c)


def _s5_body(u_ref, b_ref, ar_ref, ai_ref, cre_ref, cim_ref, dsk_ref, wg_ref, bg_ref, y_ref,
             bu_ref, st_ref, *, steps):
    half_in = u_ref.shape[1] // 2
    half_st = st_ref.shape[1] // 4

    @pl.when(pl.program_id(0) == 0)
    def _():
        st_ref[...] = jnp.zeros_like(st_ref)

    for c in range(2):
        bu_ref[:, 2 * c * half_st:2 * (c + 1) * half_st] = jnp.dot(
            u_ref[:, c * half_in:(c + 1) * half_in], b_ref[c], preferred_element_type=F32)

    w = S5_COL_GROUP
    for c in range(2):
        for s in range(half_st // w):
            re0 = 2 * c * half_st + s * w
            im0 = re0 + half_st
            a0 = c * half_st + s * w
            ar = ar_ref[:, a0:a0 + w]
            ai = ai_ref[:, a0:a0 + w]
            xr = st_ref[:, re0:re0 + w]
            xi = st_ref[:, im0:im0 + w]
            for t in range(steps):
                rows = slice(t * SUBLANES, (t + 1) * SUBLANES)
                xr, xi = (ar * xr - ai * xi + bu_ref[rows, re0:re0 + w],
                          ar * xi + ai * xr + bu_ref[rows, im0:im0 + w])
                bu_ref[rows, re0:re0 + w] = xr
                bu_ref[rows, im0:im0 + w] = xi
            st_ref[:, re0:re0 + w] = xr
            st_ref[:, im0:im0 + w] = xi

    ys = []
    for c in range(2):
        xre = bu_ref[:, 2 * c * half_st:(2 * c + 1) * half_st].astype(BF16)
        xim = bu_ref[:, (2 * c + 1) * half_st:(2 * c + 2) * half_st].astype(BF16)
        ys.append(jnp.dot(xre, cre_ref[c], preferred_element_type=F32)
                  - jnp.dot(xim, cim_ref[c], preferred_element_type=F32))
    y = jnp.concatenate(ys, axis=1) + dsk_ref[...] * u_ref[...].astype(F32)
    v = jax.nn.gelu(y)
    gate = jnp.dot(v.astype(BF16), wg_ref[...], preferred_element_type=F32) + bg_ref[...]
    out = (v * jax.nn.sigmoid(gate)).astype(y_ref.dtype)
    bsz = y_ref.shape[0]
    y_ref[...] = jnp.swapaxes(out.reshape(steps, bsz, out.shape[1]), 0, 1)


def _s5_mixer(u_tm, bsz, bmat, ar, ai, cre, cim, d_skip, w_glu, b_glu):
    rows, width = u_tm.shape
    seq = rows // bsz
    steps = min(S5_TIME_CHUNK, seq)
    r = steps * bsz
    n_state = ar.shape[1] * 2
    return pl.pallas_call(
        functools.partial(_s5_body, steps=steps),
        grid=(seq // steps,),
        in_specs=[pl.BlockSpec((r, width), lambda i: (i, 0)),
                  _resident(bmat.shape), _resident(ar.shape), _resident(ai.shape),
                  _resident(cre.shape), _resident(cim.shape), _resident(d_skip.shape),
                  _resident(w_glu.shape), _resident(b_glu.shape)],
        out_specs=pl.BlockSpec((bsz, steps, width), lambda i: (0, i, 0)),
        out_shape=jax.ShapeDtypeStruct((bsz, seq, width), BF16),
        scratch_shapes=[pltpu.VMEM((r, n_state), F32), pltpu.VMEM((bsz, n_state), F32)],
        compiler_params=_params(("arbitrary",)),
        name="s5_mixer",
    )(u_tm, bmat, ar, ai, cre, cim, d_skip, w_glu, b_glu)


def _moba_tables(nb):
    per = MOBA_ITEMS
    items = [(i, jp, 0) for i in range(nb) for jp in range(i // 2)]
    n_past, rest = divmod(len(items), per)
    items += [(i, i // 2, 0 if i % 2 else 1) for i in range(nb)]
    n_groups, rest2 = divmod(len(items), per)
    assert nb % 2 == 0 and not rest and not rest2 and 2 <= n_past < n_groups
    tab = np.zeros((6, per * (n_groups + 2)), np.int32)
    for c in range(tab.shape[1]):
        if c < len(items):
            tab[0:3, c] = items[c]
        if 0 <= c - per < len(items):
            tab[3, c] = items[c - per][0]
        if 0 <= c - 2 * per < len(items):
            tab[4:6, c] = items[c - 2 * per][0:2]
    return tab, n_past, n_groups


def _moba_body(tab_ref, q_ref, k_ref, v_ref, o_ref, kx_ref, vt_ref, cb_ref, qx_ref,
               m_ref, acc_ref, s_ring, cm_ring, p_ring, al_ring, *, nb, n_past, n_groups):
    blk = MOBA_BLOCK
    pair = 2 * HEAD_DIM
    seq = nb * blk
    two = 2 * blk
    vrows = vt_ref.shape[2]

    @pl.when((pl.program_id(0) == 0) & (pl.program_id(1) == 0))
    def _():
        row = lax.broadcasted_iota(jnp.int32, (two, pair), 0)
        lane = lax.broadcasted_iota(jnp.int32, (two, pair), 1)
        ones_row = lax.broadcasted_iota(jnp.int32, (vrows - HEAD_DIM, two), 0) == 0
        for jp in range(nb // 2):
            block = jnp.where(row < blk, 2 * jp, 2 * jp + 1)
            kx_ref[jp, :, pair:] = jnp.where(lane == block, 1.0, 0.0).astype(BF16)
            for h in range(2):
                vt_ref[jp, h, HEAD_DIM:, :] = jnp.where(ones_row, 1.0, 0.0).astype(BF16)
        kpos = lax.broadcasted_iota(jnp.int32, (blk, blk), 0)
        qpos = lax.broadcasted_iota(jnp.int32, (blk, blk), 1)
        tri = jnp.where(kpos <= qpos, 0.0, NEG_INF)
        cb_ref[0] = jnp.concatenate([jnp.zeros((blk, blk), F32), tri], axis=0)
        cb_ref[1] = jnp.concatenate([tri, jnp.full((blk, blk), NEG_INF, F32)], axis=0)

    for jp in range(nb // 2):
        rows = slice(jp * two, (jp + 1) * two)
        kx_ref[jp, :, 0:pair] = k_ref[0, rows, :]
        vt = v_ref[0, rows, :].astype(F32).T.astype(BF16)
        for h in range(2):
            vt_ref[jp, h, 0:HEAD_DIM, :] = vt[h * HEAD_DIM:(h + 1) * HEAD_DIM, :]
    km = jnp.mean(k_ref[0].astype(F32).reshape(nb, blk, pair), axis=1).astype(BF16)

    q2 = q_ref[0]
    lane = lax.broadcasted_iota(jnp.int32, q2.shape, 1)
    nidx = lax.broadcasted_iota(jnp.int32, (nb, seq), 0)
    qblk = lax.shift_right_logical(lax.broadcasted_iota(jnp.int32, (nb, seq), 1), blk.bit_length() - 1)
    for h in range(2):
        head = (lane < HEAD_DIM) if h == 0 else (lane >= HEAD_DIM)
        qh = jnp.where(head, q2, jnp.zeros_like(q2))
        g = lax.dot_general(km, qh, _NT, preferred_element_type=F32)
        g = jnp.where(nidx < qblk, g, NEG_INF)
        picked = jnp.zeros((nb, seq), jnp.int32)
        for _ in range(min(MOBA_TOPK, nb - 1)):
            top = jnp.max(g, axis=0, keepdims=True)
            first = jnp.min(jnp.where(g == top, nidx, nb), axis=0, keepdims=True)
            hit = nidx == first
            picked = jnp.where(hit, 1, picked)
            g = jnp.where(hit, TAKEN, g)
        allow = ((picked == 1) & (nidx < qblk)) | (nidx == qblk)
        bias = jnp.where(allow, 0.0, NEG_INF)
        bias = jnp.concatenate([bias, jnp.zeros((pair - nb, seq), F32)], axis=0)
        qs = (qh.astype(F32) * (HEAD_DIM ** -0.5 * LOG2_E)).T.astype(BF16)
        bias = bias.astype(BF16)
        for i in range(nb):
            qx_ref[i, 0:pair, h * blk:(h + 1) * blk] = qs[:, i * blk:(i + 1) * blk]
            qx_ref[i, pair:, h * blk:(h + 1) * blk] = bias[:, i * blk:(i + 1) * blk]

    m_ref[...] = jnp.full(m_ref.shape, NEG_INF, F32)
    acc_ref[...] = jnp.zeros_like(acc_ref)

    def step(k, carry, own_block, stages=(1, 2, 3)):
        for e in range(MOBA_ITEMS):
            c = MOBA_ITEMS * k + e
            for h in range(2):
                r = 2 * e + h
                if 3 in stages:
                    a3 = 2 * tab_ref[4, c] + h
                    pv = jnp.dot(vt_ref[tab_ref[5, c], h], p_ring[r], preferred_element_type=F32)
                    acc_ref[a3] = al_ring[r] * acc_ref[a3] + pv
                if 2 in stages:
                    a2 = 2 * tab_ref[3, c] + h
                    m_old = m_ref[a2]
                    m_new = jnp.maximum(m_old, cm_ring[r])
                    m_ref[a2] = m_new
                    al_ring[r] = jnp.exp2(m_old - m_new)
                    p_ring[r] = jnp.exp2(s_ring[r] - m_new).astype(BF16)
            if 1 in stages:
                s2 = jnp.dot(kx_ref[tab_ref[1, c]], qx_ref[tab_ref[0, c]], preferred_element_type=F32)
                for h in range(2):
                    s = s2[:, h * blk:(h + 1) * blk]
                    if own_block:
                        s = s + cb_ref[tab_ref[2, c]]
                    s_ring[2 * e + h] = s
                    cm_ring[2 * e + h] = jnp.max(s, axis=0, keepdims=True)
        return carry

    step(0, 0, own_block=False, stages=(1,))
    step(1, 0, own_block=False, stages=(1, 2))
    lax.fori_loop(2, n_past, functools.partial(step, own_block=False), 0)
    lax.fori_loop(n_past, n_groups, functools.partial(step, own_block=True), 0)
    step(n_groups, 0, own_block=False, stages=(2, 3))
    step(n_groups + 1, 0, own_block=False, stages=(3,))

    for i in range(nb):
        o = jnp.concatenate([acc_ref[2 * i + h, 0:HEAD_DIM, :] / acc_ref[2 * i + h, HEAD_DIM:HEAD_DIM + 1, :]
                             for h in range(2)], axis=0)
        o_ref[0, i * blk:(i + 1) * blk, :] = o.T.astype(o_ref.dtype)


def _moba(q, k, v):
    bsz, seq, width = q.shape
    blk = MOBA_BLOCK
    nb = seq // blk
    pair = 2 * HEAD_DIM
    tab, n_past, n_groups = _moba_tables(nb)
    spec = pl.BlockSpec((1, seq, pair), lambda b, hp, tab_ref: (b, 0, hp))
    n_state = 2 * nb
    n_ring = 2 * MOBA_ITEMS
    vrows = HEAD_DIM + BF16_SUBLANES
    return pl.pallas_call(
        functools.partial(_moba_body, nb=nb, n_past=n_past, n_groups=n_groups),
        grid_spec=pltpu.PrefetchScalarGridSpec(
            num_scalar_prefetch=1,
            grid=(bsz, width // pair),
            in_specs=[spec, spec, spec],
            out_specs=spec,
            scratch_shapes=[
                pltpu.VMEM((nb // 2, 2 * blk, 2 * pair), BF16),
                pltpu.VMEM((nb // 2, 2, vrows, 2 * blk), BF16),
                pltpu.VMEM((2, 2 * blk, blk), F32),
                pltpu.VMEM((nb, 2 * pair, 2 * blk), BF16),
                pltpu.VMEM((n_state, 1, blk), F32),
                pltpu.VMEM((n_state, vrows, blk), F32),
                pltpu.VMEM((n_ring, 2 * blk, blk), F32),
                pltpu.VMEM((n_ring, 1, blk), F32),
                pltpu.VMEM((n_ring, 2 * blk, blk), BF16),
                pltpu.VMEM((n_ring, 1, blk), F32),
            ]),
        out_shape=jax.ShapeDtypeStruct((bsz, seq, width), BF16),
        compiler_params=_params(("arbitrary", "arbitrary")),
        name="moba",
    )(jnp.asarray(tab), q, k, v)


def _post_body(x_ref, ya_ref, yb_ref, ga_ref, gb_ref, wa_ref, wb_ref, wo_ref, gpm_ref, gpf_ref,
               w1_ref, w2_ref, gpo_ref, o_ref):
    a = jnp.dot(ya_ref[...], wa_ref[...], preferred_element_type=F32)
    b = jnp.dot(yb_ref[...], wb_ref[...], preferred_element_type=F32)
    merged = (jax.nn.sigmoid(ga_ref[...].astype(F32)) * a
              + jax.nn.sigmoid(gb_ref[...].astype(F32)) * b)
    mix = jnp.dot(merged.astype(BF16), wo_ref[...], preferred_element_type=F32)
    h = x_ref[...] + _rms(mix, gpm_ref[...])
    f = _rms(h, gpf_ref[...]).astype(BF16)
    d_ff = w1_ref.shape[1]
    ck = min(FFN_CHUNK, d_ff)
    acc = jnp.zeros(h.shape, F32)
    for c in range(d_ff // ck):
        t = jnp.dot(f, w1_ref[:, c * ck:(c + 1) * ck], preferred_element_type=F32)
        t = jnp.square(jnp.maximum(t, 0.0)).astype(BF16)
        acc = acc + jnp.dot(t, w2_ref[c * ck:(c + 1) * ck, :], preferred_element_type=F32)
    o_ref[...] = h + _rms(acc, gpo_ref[...])


def _post(x2, ya, yb, ga, gb, wa, wb, wo, gpm, gpf, w1, w2, gpo):
    t, d = x2.shape
    tm = min(TOKEN_TILE, t)

    def rows(a):
        return pl.BlockSpec((tm, a.shape[1]), lambda i: (i, 0))

    consts = (wa, wb, wo, gpm, gpf, w1, w2, gpo)
    return pl.pallas_call(
        _post_body,
        grid=(t // tm,),
        in_specs=[rows(a) for a in (x2, ya, yb, ga, gb)] + [_resident(c.shape) for c in consts],
        out_specs=rows(x2),
        out_shape=jax.ShapeDtypeStruct((t, d), x2.dtype),
        compiler_params=_params(("parallel",)),
        name="post",
    )(x2, ya, yb, ga, gb, *consts)


def _layer(h, g_pre_mix, w_in, lam_re, lam_im, log_dt, b_re, b_im, c_re, c_im, d_skip, w_glu, b_glu,
           w_branch_a, w_branch_b, w_out, g_post_mix, g_pre_ffn, w_ff1, w_ff2, g_post_ffn):
    bsz, seq, d = h.shape
    ssm_w = lam_re.shape[0] * SSM_GROUP
    attn_w = ATTN_HEADS * HEAD_DIM
    x2 = h.reshape(bsz * seq, d)

    zs, q, k, v, ga, gb = _inproj(h, g_pre_mix.reshape(1, d), w_in.astype(BF16),
                                  (ssm_w, attn_w, attn_w, attn_w, d, d))

    ar, ai, bbr, bbi = _s5_prep(lam_re, lam_im, log_dt, b_re, b_im)
    g, p = lam_re.shape
    hg = g // 2
    n_half = hg * p

    def halves(m):
        return jnp.stack([_block_diag(m[:hg]), _block_diag(m[hg:])])

    bmat = jnp.concatenate([halves(bbr), halves(bbi)], axis=2).astype(BF16)
    cre = halves(jnp.transpose(c_re, (0, 2, 1))).astype(BF16)
    cim = halves(jnp.transpose(c_im, (0, 2, 1))).astype(BF16)
    ar8 = jnp.broadcast_to(ar.reshape(1, 2 * n_half), (bsz, 2 * n_half))
    ai8 = jnp.broadcast_to(ai.reshape(1, 2 * n_half), (bsz, 2 * n_half))

    ya = _s5_mixer(zs.reshape(seq * bsz, ssm_w), bsz, bmat, ar8, ai8, cre, cim, d_skip.reshape(1, ssm_w),
                   w_glu.astype(BF16), b_glu.reshape(1, ssm_w)).reshape(bsz * seq, ssm_w)
    yb = _moba(q, k, v).reshape(bsz * seq, attn_w)
    ga, gb = ga.reshape(bsz * seq, d), gb.reshape(bsz * seq, d)

    out = _post(x2, ya, yb, ga, gb, w_branch_a.astype(BF16), w_branch_b.astype(BF16),
                w_out.astype(BF16), g_post_mix.reshape(1, d), g_pre_ffn.reshape(1, d),
                w_ff1.astype(BF16), w_ff2.astype(BF16), g_post_ffn.reshape(1, d))
    return out.reshape(bsz, seq, d)


def kernel(x, g_pre_mix, w_in, lam_re, lam_im, log_dt, b_re, b_im, c_re, c_im, d_skip, w_glu, b_glu,
           w_branch_a, w_branch_b, w_out, g_post_mix, g_pre_ffn, w_ff1, w_ff2, g_post_ffn):
    per_layer = (g_pre_mix, w_in, lam_re, lam_im, log_dt, b_re, b_im, c_re, c_im, d_skip, w_glu, b_glu,
                 w_branch_a, w_branch_b, w_out, g_post_mix, g_pre_ffn, w_ff1, w_ff2, g_post_ffn)
    h = x
    for l in range(g_pre_mix.shape[0]):
        h = _layer(h, *(p[l] for p in per_layer))
    return h
```

```python
import functools

import jax
import jax.numpy as jnp
import numpy as np
from jax import lax
from jax.experimental import pallas as pl
from jax.experimental.pallas import tpu as pltpu

F32 = jnp.float32
BF16 = jnp.bfloat16

RMS_EPS = 1e-6
NEG_INF = -1e30
TAKEN = -3e38

SSM_GROUP = 16
SSM_STATE = 64
ATTN_HEADS = 8
HEAD_DIM = 64
MOBA_BLOCK = 256
MOBA_TOPK = 3
MOBA_ITEMS = 4

LOG2_E = 1.4426950408889634

SUBLANES = 8
BF16_SUBLANES = 16
LANES = 128
VMEM_LIMIT_BYTES = 56 * 1024 * 1024

TOKEN_TILE = 512
S5_TIME_CHUNK = 64
S5_COL_GROUP = 512
FFN_CHUNK = 1024

_NT = (((1,), (1,)), ((), ()))


def _params(sem):
    return pltpu.CompilerParams(dimension_semantics=sem, vmem_limit_bytes=VMEM_LIMIT_BYTES)


def _resident(shape):
    zeros = (0,) * len(shape)
    return pl.BlockSpec(shape, lambda *_: zeros, pipeline_mode=pl.Buffered(1))


def _rms(x, g):
    var = jnp.mean(x * x, axis=-1, keepdims=True)
    return (x * lax.rsqrt(var + RMS_EPS)) * g


def _inproj_body(x_ref, g_ref, w_ref, zs_ref, *out_refs):
    bsz, tl, d = x_ref.shape
    ub = _rms(x_ref[...].reshape(bsz * tl, d), g_ref[...]).astype(BF16)
    col = 0
    for ref in (zs_ref,) + out_refs:
        n = ref.shape[-1]
        z = jnp.dot(ub, w_ref[:, col:col + n], preferred_element_type=F32).astype(ref.dtype)
        z = z.reshape(bsz, tl, n)
        ref[...] = jnp.swapaxes(z, 0, 1) if ref is zs_ref else z
        col += n


def _inproj(x, g, w, widths):
    bsz, seq, d = x.shape
    tl = min(TOKEN_TILE // bsz, seq)
    return pl.pallas_call(
        _inproj_body,
        grid=(seq // tl,),
        in_specs=[pl.BlockSpec((bsz, tl, d), lambda i: (0, i, 0)), _resident(g.shape), _resident(w.shape)],
        out_specs=[pl.BlockSpec((tl, bsz, widths[0]), lambda i: (i, 0, 0))]
        + [pl.BlockSpec((bsz, tl, n), lambda i: (0, i, 0)) for n in widths[1:]],
        out_shape=[jax.ShapeDtypeStruct((seq, bsz, widths[0]), BF16)]
        + [jax.ShapeDtypeStruct((bsz, seq, n), BF16) for n in widths[1:]],
        compiler_params=_params(("parallel",)),
        name="inproj",
    )(x, g, w)


def _s5_prep_body(lr_ref, li_ref, ldt_ref, br_ref, bi_ref, ar_ref, ai_ref, bbr_ref, bbi_ref):
    lr = lr_ref[...]
    li = li_ref[...]
    dt = jnp.exp(ldt_ref[...])
    mag = jnp.exp(lr * dt)
    ab_re = mag * jnp.cos(li * dt)
    ab_im = mag * jnp.sin(li * dt)
    nr = ab_re - 1.0
    ni = ab_im
    den = lr * lr + li * li
    coef_re = (nr * lr + ni * li) / den
    coef_im = (ni * lr - nr * li) / den
    br = br_ref[...]
    bi = bi_ref[...]
    ar_ref[...] = ab_re
    ai_ref[...] = ab_im
    bbr_ref[...] = coef_re * br - coef_im * bi
    bbi_ref[...] = coef_re * bi + coef_im * br


def _s5_prep(lam_re, lam_im, log_dt, b_re, b_im):
    g, p = lam_re.shape
    h = b_re.shape[-1]
    lr = lam_re.reshape(g, 1, p)
    li = lam_im.reshape(g, 1, p)
    ldt = jnp.broadcast_to(log_dt.reshape(g, 1, 1), (g, 1, p))
    br = jnp.transpose(b_re, (0, 2, 1))
    bi = jnp.transpose(b_im, (0, 2, 1))
    small = jax.ShapeDtypeStruct((g, 1, p), F32)
    big = jax.ShapeDtypeStruct((g, h, p), F32)
    return pl.pallas_call(_s5_prep_body, out_shape=[small, small, big, big], name="s5_prep")(lr, li, ldt, br, bi)


def _block_diag(m):
    g, r, c = m.shape
    on_diag = jnp.eye(g, dtype=bool)[:, None, :, None]
    return jnp.where(on_diag, m[:, :, None, :], jnp.zeros((), m.dtype)).reshape(g * r, g * c)


def _s5_body(u_ref, b_ref, ar_ref, ai_ref, cre_ref, cim_ref, dsk_ref, wg_ref, bg_ref, y_ref,
             bu_ref, st_ref, *, steps):
    half_in = u_ref.shape[1] // 2
    half_st = st_ref.shape[1] // 4

    @pl.when(pl.program_id(0) == 0)
    def _():
        st_ref[...] = jnp.zeros_like(st_ref)

    for c in range(2):
        bu_ref[:, 2 * c * half_st:2 * (c + 1) * half_st] = jnp.dot(
            u_ref[:, c * half_in:(c + 1) * half_in], b_ref[c], preferred_element_type=F32)

    w = S5_COL_GROUP
    for c in range(2):
        for s in range(half_st // w):
            re0 = 2 * c * half_st + s * w
            im0 = re0 + half_st
            a0 = c * half_st + s * w
            ar = ar_ref[:, a0:a0 + w]
            ai = ai_ref[:, a0:a0 + w]
            xr = st_ref[:, re0:re0 + w]
            xi = st_ref[:, im0:im0 + w]
            for t in range(steps):
                rows = slice(t * SUBLANES, (t + 1) * SUBLANES)
                xr, xi = (ar * xr - ai * xi + bu_ref[rows, re0:re0 + w],
                          ar * xi + ai * xr + bu_ref[rows, im0:im0 + w])
                bu_ref[rows, re0:re0 + w] = xr
                bu_ref[rows, im0:im0 + w] = xi
            st_ref[:, re0:re0 + w] = xr
            st_ref[:, im0:im0 + w] = xi

    ys = []
    for c in range(2):
        xre = bu_ref[:, 2 * c * half_st:(2 * c + 1) * half_st].astype(BF16)
        xim = bu_ref[:, (2 * c + 1) * half_st:(2 * c + 2) * half_st].astype(BF16)
        ys.append(jnp.dot(xre, cre_ref[c], preferred_element_type=F32)
                  - jnp.dot(xim, cim_ref[c], preferred_element_type=F32))
    y = jnp.concatenate(ys, axis=1) + dsk_ref[...] * u_ref[...].astype(F32)
    v = jax.nn.gelu(y)
    gate = jnp.dot(v.astype(BF16), wg_ref[...], preferred_element_type=F32) + bg_ref[...]
    out = (v * jax.nn.sigmoid(gate)).astype(y_ref.dtype)
    bsz = y_ref.shape[0]
    y_ref[...] = jnp.swapaxes(out.reshape(steps, bsz, out.shape[1]), 0, 1)


def _s5_mixer(u_tm, bsz, bmat, ar, ai, cre, cim, d_skip, w_glu, b_glu):
    rows, width = u_tm.shape
    seq = rows // bsz
    steps = min(S5_TIME_CHUNK, seq)
    r = steps * bsz
    n_state = ar.shape[1] * 2
    return pl.pallas_call(
        functools.partial(_s5_body, steps=steps),
        grid=(seq // steps,),
        in_specs=[pl.BlockSpec((r, width), lambda i: (i, 0)),
                  _resident(bmat.shape), _resident(ar.shape), _resident(ai.shape),
                  _resident(cre.shape), _resident(cim.shape), _resident(d_skip.shape),
                  _resident(w_glu.shape), _resident(b_glu.shape)],
        out_specs=pl.BlockSpec((bsz, steps, width), lambda i: (0, i, 0)),
        out_shape=jax.ShapeDtypeStruct((bsz, seq, width), BF16),
        scratch_shapes=[pltpu.VMEM((r, n_state), F32), pltpu.VMEM((bsz, n_state), F32)],
        compiler_params=_params(("arbitrary",)),
        name="s5_mixer",
    )(u_tm, bmat, ar, ai, cre, cim, d_skip, w_glu, b_glu)


def _moba_tables(nb):
    per = MOBA_ITEMS
    items = [(i, jp, 0) for i in range(nb) for jp in range(i // 2)]
    n_past, rest = divmod(len(items), per)
    items += [(i, i // 2, 0 if i % 2 else 1) for i in range(nb)]
    n_groups, rest2 = divmod(len(items), per)
    assert nb % 2 == 0 and not rest and not rest2 and 2 <= n_past < n_groups
    tab = np.zeros((6, per * (n_groups + 2)), np.int32)
    for c in range(tab.shape[1]):
        if c < len(items):
            tab[0:3, c] = items[c]
        if 0 <= c - per < len(items):
            tab[3, c] = items[c - per][0]
        if 0 <= c - 2 * per < len(items):
            tab[4:6, c] = items[c - 2 * per][0:2]
    return tab, n_past, n_groups


def _moba_body(tab_ref, q_ref, k_ref, v_ref, o_ref, kx_ref, vt_ref, cb_ref, qx_ref,
               m_ref, acc_ref, s_ring, cm_ring, p_ring, al_ring, *, nb, n_past, n_groups):
    blk = MOBA_BLOCK
    pair = 2 * HEAD_DIM
    seq = nb * blk
    two = 2 * blk
    vrows = vt_ref.shape[2]

    @pl.when((pl.program_id(0) == 0) & (pl.program_id(1) == 0))
    def _():
        row = lax.broadcasted_iota(jnp.int32, (two, pair), 0)
        lane = lax.broadcasted_iota(jnp.int32, (two, pair), 1)
        ones_row = lax.broadcasted_iota(jnp.int32, (vrows - HEAD_DIM, two), 0) == 0
        for jp in range(nb // 2):
            block = jnp.where(row < blk, 2 * jp, 2 * jp + 1)
            kx_ref[jp, :, pair:] = jnp.where(lane == block, 1.0, 0.0).astype(BF16)
            for h in range(2):
                vt_ref[jp, h, HEAD_DIM:, :] = jnp.where(ones_row, 1.0, 0.0).astype(BF16)
        kpos = lax.broadcasted_iota(jnp.int32, (blk, blk), 0)
        qpos = lax.broadcasted_iota(jnp.int32, (blk, blk), 1)
        tri = jnp.where(kpos <= qpos, 0.0, NEG_INF)
        cb_ref[0] = jnp.concatenate([jnp.zeros((blk, blk), F32), tri], axis=0)
        cb_ref[1] = jnp.concatenate([tri, jnp.full((blk, blk), NEG_INF, F32)], axis=0)

    for jp in range(nb // 2):
        rows = slice(jp * two, (jp + 1) * two)
        kx_ref[jp, :, 0:pair] = k_ref[0, rows, :]
        vt = v_ref[0, rows, :].astype(F32).T.astype(BF16)
        for h in range(2):
            vt_ref[jp, h, 0:HEAD_DIM, :] = vt[h * HEAD_DIM:(h + 1) * HEAD_DIM, :]
    km = jnp.mean(k_ref[0].astype(F32).reshape(nb, blk, pair), axis=1).astype(BF16)

    q2 = q_ref[0]
    lane = lax.broadcasted_iota(jnp.int32, q2.shape, 1)
    nidx = lax.broadcasted_iota(jnp.int32, (nb, seq), 0)
    qblk = lax.shift_right_logical(lax.broadcasted_iota(jnp.int32, (nb, seq), 1), blk.bit_length() - 1)
    for h in range(2):
        head = (lane < HEAD_DIM) if h == 0 else (lane >= HEAD_DIM)
        qh = jnp.where(head, q2, jnp.zeros_like(q2))
        g = lax.dot_general(km, qh, _NT, preferred_element_type=F32)
        g = jnp.where(nidx < qblk, g, NEG_INF)
        picked = jnp.zeros((nb, seq), jnp.int32)
        for _ in range(min(MOBA_TOPK, nb - 1)):
            top = jnp.max(g, axis=0, keepdims=True)
            first = jnp.min(jnp.where(g == top, nidx, nb), axis=0, keepdims=True)
            hit = nidx == first
            picked = jnp.where(hit, 1, picked)
            g = jnp.where(hit, TAKEN, g)
        allow = ((picked == 1) & (nidx < qblk)) | (nidx == qblk)
        bias = jnp.where(allow, 0.0, NEG_INF)
        bias = jnp.concatenate([bias, jnp.zeros((pair - nb, seq), F32)], axis=0)
        qs = (qh.astype(F32) * (HEAD_DIM ** -0.5 * LOG2_E)).T.astype(BF16)
        bias = bias.astype(BF16)
        for i in range(nb):
            qx_ref[i, 0:pair, h * blk:(h + 1) * blk] = qs[:, i * blk:(i + 1) * blk]
            qx_ref[i, pair:, h * blk:(h + 1) * blk] = bias[:, i * blk:(i + 1) * blk]

    m_ref[...] = jnp.full(m_ref.shape, NEG_INF, F32)
    acc_ref[...] = jnp.zeros_like(acc_ref)

    def step(k, carry, own_block, stages=(1, 2, 3)):
        for e in range(MOBA_ITEMS):
            c = MOBA_ITEMS * k + e
            for h in range(2):
                r = 2 * e + h
                if 3 in stages:
                    a3 = 2 * tab_ref[4, c] + h
                    pv = jnp.dot(vt_ref[tab_ref[5, c], h], p_ring[r], preferred_element_type=F32)
                    acc_ref[a3] = al_ring[r] * acc_ref[a3] + pv
                if 2 in stages:
                    a2 = 2 * tab_ref[3, c] + h
                    m_old = m_ref[a2]
                    m_new = jnp.maximum(m_old, cm_ring[r])
                    m_ref[a2] = m_new
                    al_ring[r] = jnp.exp2(m_old - m_new)
                    p_ring[r] = jnp.exp2(s_ring[r] - m_new).astype(BF16)
            if 1 in stages:
                s2 = jnp.dot(kx_ref[tab_ref[1, c]], qx_ref[tab_ref[0, c]], preferred_element_type=F32)
                for h in range(2):
                    s = s2[:, h * blk:(h + 1) * blk]
                    if own_block:
                        s = s + cb_ref[tab_ref[2, c]]
                    s_ring[2 * e + h] = s
                    cm_ring[2 * e + h] = jnp.max(s, axis=0, keepdims=True)
        return carry

    step(0, 0, own_block=False, stages=(1,))
    step(1, 0, own_block=False, stages=(1, 2))
    lax.fori_loop(2, n_past, functools.partial(step, own_block=False), 0)
    lax.fori_loop(n_past, n_groups, functools.partial(step, own_block=True), 0)
    step(n_groups, 0, own_block=False, stages=(2, 3))
    step(n_groups + 1, 0, own_block=False, stages=(3,))

    for i in range(nb):
        o = jnp.concatenate([acc_ref[2 * i + h, 0:HEAD_DIM, :] / acc_ref[2 * i + h, HEAD_DIM:HEAD_DIM + 1, :]
                             for h in range(2)], axis=0)
        o_ref[0, i * blk:(i + 1) * blk, :] = o.T.astype(o_ref.dtype)


def _moba(q, k, v):
    bsz, seq, width = q.shape
    blk = MOBA_BLOCK
    nb = seq // blk
    pair = 2 * HEAD_DIM
    tab, n_past, n_groups = _moba_tables(nb)
    spec = pl.BlockSpec((1, seq, pair), lambda b, hp, tab_ref: (b, 0, hp))
    n_state = 2 * nb
    n_ring = 2 * MOBA_ITEMS
    vrows = HEAD_DIM + BF16_SUBLANES
    return pl.pallas_call(
        functools.partial(_moba_body, nb=nb, n_past=n_past, n_groups=n_groups),
        grid_spec=pltpu.PrefetchScalarGridSpec(
            num_scalar_prefetch=1,
            grid=(bsz, width // pair),
            in_specs=[spec, spec, spec],
            out_specs=spec,
            scratch_shapes=[
                pltpu.VMEM((nb // 2, 2 * blk, 2 * pair), BF16),
                pltpu.VMEM((nb // 2, 2, vrows, 2 * blk), BF16),
                pltpu.VMEM((2, 2 * blk, blk), F32),
                pltpu.VMEM((nb, 2 * pair, 2 * blk), BF16),
                pltpu.VMEM((n_state, 1, blk), F32),
                pltpu.VMEM((n_state, vrows, blk), F32),
                pltpu.VMEM((n_ring, 2 * blk, blk), F32),
                pltpu.VMEM((n_ring, 1, blk), F32),
                pltpu.VMEM((n_ring, 2 * blk, blk), BF16),
                pltpu.VMEM((n_ring, 1, blk), F32),
            ]),
        out_shape=jax.ShapeDtypeStruct((bsz, seq, width), BF16),
        compiler_params=_params(("arbitrary", "arbitrary")),
        name="moba",
    )(jnp.asarray(tab), q, k, v)


def _post_body(x_ref, ya_ref, yb_ref, ga_ref, gb_ref, wa_ref, wb_ref, wo_ref, gpm_ref, gpf_ref,
               w1_ref, w2_ref, gpo_ref, o_ref):
    a = jnp.dot(ya_ref[...], wa_ref[...], preferred_element_type=F32)
    b = jnp.dot(yb_ref[...], wb_ref[...], preferred_element_type=F32)
    merged = (jax.nn.sigmoid(ga_ref[...].astype(F32)) * a
              + jax.nn.sigmoid(gb_ref[...].astype(F32)) * b)
    mix = jnp.dot(merged.astype(BF16), wo_ref[...], preferred_element_type=F32)
    h = x_ref[...] + _rms(mix, gpm_ref[...])
    f = _rms(h, gpf_ref[...]).astype(BF16)
    d_ff = w1_ref.shape[1]
    ck = min(FFN_CHUNK, d_ff)
    acc = jnp.zeros(h.shape, F32)
    for c in range(d_ff // ck):
        t = jnp.dot(f, w1_ref[:, c * ck:(c + 1) * ck], preferred_element_type=F32)
        t = jnp.square(jnp.maximum(t, 0.0)).astype(BF16)
        acc = acc + jnp.dot(t, w2_ref[c * ck:(c + 1) * ck, :], preferred_element_type=F32)
    o_ref[...] = h + _rms(acc, gpo_ref[...])


def _post(x2, ya, yb, ga, gb, wa, wb, wo, gpm, gpf, w1, w2, gpo):
    t, d = x2.shape
    tm = min(TOKEN_TILE, t)

    def rows(a):
        return pl.BlockSpec((tm, a.shape[1]), lambda i: (i, 0))

    consts = (wa, wb, wo, gpm, gpf, w1, w2, gpo)
    return pl.pallas_call(
        _post_body,
        grid=(t // tm,),
        in_specs=[rows(a) for a in (x2, ya, yb, ga, gb)] + [_resident(c.shape) for c in consts],
        out_specs=rows(x2),
        out_shape=jax.ShapeDtypeStruct((t, d), x2.dtype),
        compiler_params=_params(("parallel",)),
        name="post",
    )(x2, ya, yb, ga, gb, *consts)


def _layer(h, g_pre_mix, w_in, lam_re, lam_im, log_dt, b_re, b_im, c_re, c_im, d_skip, w_glu, b_glu,
           w_branch_a, w_branch_b, w_out, g_post_mix, g_pre_ffn, w_ff1, w_ff2, g_post_ffn):
    bsz, seq, d = h.shape
    ssm_w = lam_re.shape[0] * SSM_GROUP
    attn_w = ATTN_HEADS * HEAD_DIM
    x2 = h.reshape(bsz * seq, d)

    zs, q, k, v, ga, gb = _inproj(h, g_pre_mix.reshape(1, d), w_in.astype(BF16),
                                  (ssm_w, attn_w, attn_w, attn_w, d, d))

    ar, ai, bbr, bbi = _s5_prep(lam_re, lam_im, log_dt, b_re, b_im)
    g, p = lam_re.shape
    hg = g // 2
    n_half = hg * p

    def halves(m):
        return jnp.stack([_block_diag(m[:hg]), _block_diag(m[hg:])])

    bmat = jnp.concatenate([halves(bbr), halves(bbi)], axis=2).astype(BF16)
    cre = halves(jnp.transpose(c_re, (0, 2, 1))).astype(BF16)
    cim = halves(jnp.transpose(c_im, (0, 2, 1))).astype(BF16)
    ar8 = jnp.broadcast_to(ar.reshape(1, 2 * n_half), (bsz, 2 * n_half))
    ai8 = jnp.broadcast_to(ai.reshape(1, 2 * n_half), (bsz, 2 * n_half))

    ya = _s5_mixer(zs.reshape(seq * bsz, ssm_w), bsz, bmat, ar8, ai8, cre, cim, d_skip.reshape(1, ssm_w),
                   w_glu.astype(BF16), b_glu.reshape(1, ssm_w)).reshape(bsz * seq, ssm_w)
    yb = _moba(q, k, v).reshape(bsz * seq, attn_w)
    ga, gb = ga.reshape(bsz * seq, d), gb.reshape(bsz * seq, d)

    out = _post(x2, ya, yb, ga, gb, w_branch_a.astype(BF16), w_branch_b.astype(BF16),
                w_out.astype(BF16), g_post_mix.reshape(1, d), g_pre_ffn.reshape(1, d),
                w_ff1.astype(BF16), w_ff2.astype(BF16), g_post_ffn.reshape(1, d))
    return out.reshape(bsz, seq, d)


def kernel(x, g_pre_mix, w_in, lam_re, lam_im, log_dt, b_re, b_im, c_re, c_im, d_skip, w_glu, b_glu,
           w_branch_a, w_branch_b, w_out, g_post_mix, g_pre_ffn, w_ff1, w_ff2, g_post_ffn):
    per_layer = (g_pre_mix, w_in, lam_re, lam_im, log_dt, b_re, b_im, c_re, c_im, d_skip, w_glu, b_glu,
                 w_branch_a, w_branch_b, w_out, g_post_mix, g_pre_ffn, w_ff1, w_ff2, g_post_ffn)
    h = x
    for l in range(g_pre_mix.shape[0]):
        h = _layer(h, *(p[l] for p in per_layer))
    return h
```

```python
import functools

import jax
import jax.numpy as jnp
import numpy as np
from jax import lax
from jax.experimental import pallas as pl
from jax.experimental.pallas import tpu as pltpu

F32 = jnp.float32
BF16 = jnp.bfloat16

RMS_EPS = 1e-6
NEG_INF = -1e30
TAKEN = -3e38

SSM_GROUP = 16
SSM_STATE = 64
ATTN_HEADS = 8
HEAD_DIM = 64
MOBA_BLOCK = 256
MOBA_TOPK = 3
MOBA_ITEMS = 4

LOG2_E = 1.4426950408889634

SUBLANES = 8
BF16_SUBLANES = 16
LANES = 128
VMEM_LIMIT_BYTES = 56 * 1024 * 1024

TOKEN_TILE = 512
S5_TIME_CHUNK = 64
S5_COL_GROUP = 512
FFN_CHUNK = 1024

_NT = (((1,), (1,)), ((), ()))


def _params(sem):
    return pltpu.CompilerParams(dimension_semantics=sem, vmem_limit_bytes=VMEM_LIMIT_BYTES)


def _resident(shape):
    zeros = (0,) * len(shape)
    return pl.BlockSpec(shape, lambda *_: zeros, pipeline_mode=pl.Buffered(1))


def _rms(x, g):
    var = jnp.mean(x * x, axis=-1, keepdims=True)
    return (x * lax.rsqrt(var + RMS_EPS)) * g


def _inproj_body(x_ref, g_ref, w_ref, zs_ref, *out_refs):
    bsz, tl, d = x_ref.shape
    ub = _rms(x_ref[...].reshape(bsz * tl, d), g_ref[...]).astype(BF16)
    col = 0
    for ref in (zs_ref,) + out_refs:
        n = ref.shape[-1]
        z = jnp.dot(ub, w_ref[:, col:col + n], preferred_element_type=F32).astype(ref.dtype)
        z = z.reshape(bsz, tl, n)
        ref[...] = jnp.swapaxes(z, 0, 1) if ref is zs_ref else z
        col += n


def _inproj(x, g, w, widths):
    bsz, seq, d = x.shape
    tl = min(2 * TOKEN_TILE // bsz, seq)
    return pl.pallas_call(
        _inproj_body,
        grid=(seq // tl,),
        in_specs=[pl.BlockSpec((bsz, tl, d), lambda i: (0, i, 0)), _resident(g.shape), _resident(w.shape)],
        out_specs=[pl.BlockSpec((tl, bsz, widths[0]), lambda i: (i, 0, 0))]
        + [pl.BlockSpec((bsz, tl, n), lambda i: (0, i, 0)) for n in widths[1:]],
        out_shape=[jax.ShapeDtypeStruct((seq, bsz, widths[0]), BF16)]
        + [jax.ShapeDtypeStruct((bsz, seq, n), BF16) for n in widths[1:]],
        compiler_params=_params(("parallel",)),
        name="inproj",
    )(x, g, w)


def _s5_prep_body(lr_ref, li_ref, ldt_ref, br_ref, bi_ref, ar_ref, ai_ref, bbr_ref, bbi_ref):
    lr = lr_ref[...]
    li = li_ref[...]
    dt = jnp.exp(ldt_ref[...])
    mag = jnp.exp(lr * dt)
    ab_re = mag * jnp.cos(li * dt)
    ab_im = mag * jnp.sin(li * dt)
    nr = ab_re - 1.0
    ni = ab_im
    den = lr * lr + li * li
    coef_re = (nr * lr + ni * li) / den
    coef_im = (ni * lr - nr * li) / den
    br = br_ref[...]
    bi = bi_ref[...]
    ar_ref[...] = ab_re
    ai_ref[...] = ab_im
    bbr_ref[...] = coef_re * br - coef_im * bi
    bbi_ref[...] = coef_re * bi + coef_im * br


def _s5_prep(lam_re, lam_im, log_dt, b_re, b_im):
    g, p = lam_re.shape
    h = b_re.shape[-1]
    lr = lam_re.reshape(g, 1, p)
    li = lam_im.reshape(g, 1, p)
    ldt = jnp.broadcast_to(log_dt.reshape(g, 1, 1), (g, 1, p))
    br = jnp.transpose(b_re, (0, 2, 1))
    bi = jnp.transpose(b_im, (0, 2, 1))
    small = jax.ShapeDtypeStruct((g, 1, p), F32)
    big = jax.ShapeDtypeStruct((g, h, p), F32)
    return pl.pallas_call(_s5_prep_body, out_shape=[small, small, big, big], name="s5_prep")(lr, li, ldt, br, bi)


def _block_diag(m):
    g, r, c = m.shape
    on_diag = jnp.eye(g, dtype=bool)[:, None, :, None]
    return jnp.where(on_diag, m[:, :, None, :], jnp.zeros((), m.dtype)).reshape(g * r, g * c)


def _s5_body(u_ref, b_ref, ar_ref, ai_ref, cre_ref, cim_ref, dsk_ref, wg_ref, bg_ref, y_ref,
             bu_ref, st_ref, *, steps):
    half_in = u_ref.shape[1] // 2
    half_st = st_ref.shape[1] // 4

    @pl.when(pl.program_id(0) == 0)
    def _():
        st_ref[...] = jnp.zeros_like(st_ref)

    for c in range(2):
        bu_ref[:, 2 * c * half_st:2 * (c + 1) * half_st] = jnp.dot(
            u_ref[:, c * half_in:(c + 1) * half_in], b_ref[c], preferred_element_type=F32)

    w = S5_COL_GROUP
    for c in range(2):
        for s in range(half_st // w):
            re0 = 2 * c * half_st + s * w
            im0 = re0 + half_st
            a0 = c * half_st + s * w
            ar = ar_ref[:, a0:a0 + w]
            ai = ai_ref[:, a0:a0 + w]
            xr = st_ref[:, re0:re0 + w]
            xi = st_ref[:, im0:im0 + w]
            for t in range(steps):
                rows = slice(t * SUBLANES, (t + 1) * SUBLANES)
                xr, xi = (ar * xr - ai * xi + bu_ref[rows, re0:re0 + w],
                          ar * xi + ai * xr + bu_ref[rows, im0:im0 + w])
                bu_ref[rows, re0:re0 + w] = xr
                bu_ref[rows, im0:im0 + w] = xi
            st_ref[:, re0:re0 + w] = xr
            st_ref[:, im0:im0 + w] = xi

    ys = []
    for c in range(2):
        xre = bu_ref[:, 2 * c * half_st:(2 * c + 1) * half_st].astype(BF16)
        xim = bu_ref[:, (2 * c + 1) * half_st:(2 * c + 2) * half_st].astype(BF16)
        ys.append(jnp.dot(xre, cre_ref[c], preferred_element_type=F32)
                  - jnp.dot(xim, cim_ref[c], preferred_element_type=F32))
    y = jnp.concatenate(ys, axis=1) + dsk_ref[...] * u_ref[...].astype(F32)
    v = jax.nn.gelu(y)
    gate = jnp.dot(v.astype(BF16), wg_ref[...], preferred_element_type=F32) + bg_ref[...]
    out = (v * jax.nn.sigmoid(gate)).astype(y_ref.dtype)
    bsz = y_ref.shape[0]
    y_ref[...] = jnp.swapaxes(out.reshape(steps, bsz, out.shape[1]), 0, 1)


def _s5_mixer(u_tm, bsz, bmat, ar, ai, cre, cim, d_skip, w_glu, b_glu):
    rows, width = u_tm.shape
    seq = rows // bsz
    steps = min(S5_TIME_CHUNK, seq)
    r = steps * bsz
    n_state = ar.shape[1] * 2
    return pl.pallas_call(
        functools.partial(_s5_body, steps=steps),
        grid=(seq // steps,),
        in_specs=[pl.BlockSpec((r, width), lambda i: (i, 0)),
                  _resident(bmat.shape), _resident(ar.shape), _resident(ai.shape),
                  _resident(cre.shape), _resident(cim.shape), _resident(d_skip.shape),
                  _resident(w_glu.shape), _resident(b_glu.shape)],
        out_specs=pl.BlockSpec((bsz, steps, width), lambda i: (0, i, 0)),
        out_shape=jax.ShapeDtypeStruct((bsz, seq, width), BF16),
        scratch_shapes=[pltpu.VMEM((r, n_state), F32), pltpu.VMEM((bsz, n_state), F32)],
        compiler_params=_params(("arbitrary",)),
        name="s5_mixer",
    )(u_tm, bmat, ar, ai, cre, cim, d_skip, w_glu, b_glu)


def _moba_tables(nb):
    per = MOBA_ITEMS
    items = [(i, jp, 0) for i in range(nb) for jp in range(i // 2)]
    n_past, rest = divmod(len(items), per)
    items += [(i, i // 2, 0 if i % 2 else 1) for i in range(nb)]
    n_groups, rest2 = divmod(len(items), per)
    assert nb % 2 == 0 and not rest and not rest2 and 2 <= n_past < n_groups
    tab = np.zeros((6, per * (n_groups + 2)), np.int32)
    for c in range(tab.shape[1]):
        if c < len(items):
            tab[0:3, c] = items[c]
        if 0 <= c - per < len(items):
            tab[3, c] = items[c - per][0]
        if 0 <= c - 2 * per < len(items):
            tab[4:6, c] = items[c - 2 * per][0:2]
    return tab, n_past, n_groups


def _moba_body(tab_ref, q_ref, k_ref, v_ref, o_ref, kx_ref, vt_ref, cb_ref, qx_ref,
               m_ref, acc_ref, s_ring, cm_ring, p_ring, al_ring, *, nb, n_past, n_groups):
    blk = MOBA_BLOCK
    pair = 2 * HEAD_DIM
    seq = nb * blk
    two = 2 * blk
    vrows = vt_ref.shape[2]

    @pl.when((pl.program_id(0) == 0) & (pl.program_id(1) == 0))
    def _():
        row = lax.broadcasted_iota(jnp.int32, (two, pair), 0)
        lane = lax.broadcasted_iota(jnp.int32, (two, pair), 1)
        ones_row = lax.broadcasted_iota(jnp.int32, (vrows - HEAD_DIM, two), 0) == 0
        for jp in range(nb // 2):
            block = jnp.where(row < blk, 2 * jp, 2 * jp + 1)
            kx_ref[jp, :, pair:] = jnp.where(lane == block, 1.0, 0.0).astype(BF16)
            for h in range(2):
                vt_ref[jp, h, HEAD_DIM:, :] = jnp.where(ones_row, 1.0, 0.0).astype(BF16)
        kpos = lax.broadcasted_iota(jnp.int32, (blk, blk), 0)
        qpos = lax.broadcasted_iota(jnp.int32, (blk, blk), 1)
        tri = jnp.where(kpos <= qpos, 0.0, NEG_INF)
        cb_ref[0] = jnp.concatenate([jnp.zeros((blk, blk), F32), tri], axis=0)
        cb_ref[1] = jnp.concatenate([tri, jnp.full((blk, blk), NEG_INF, F32)], axis=0)

    for jp in range(nb // 2):
        rows = slice(jp * two, (jp + 1) * two)
        kx_ref[jp, :, 0:pair] = k_ref[0, rows, :]
        vt = v_ref[0, rows, :].astype(F32).T.astype(BF16)
        for h in range(2):
            vt_ref[jp, h, 0:HEAD_DIM, :] = vt[h * HEAD_DIM:(h + 1) * HEAD_DIM, :]
    km = jnp.mean(k_ref[0].astype(F32).reshape(nb, blk, pair), axis=1).astype(BF16)

    q2 = q_ref[0]
    lane = lax.broadcasted_iota(jnp.int32, q2.shape, 1)
    nidx = lax.broadcasted_iota(jnp.int32, (nb, seq), 0)
    qblk = lax.shift_right_logical(lax.broadcasted_iota(jnp.int32, (nb, seq), 1), blk.bit_length() - 1)
    for h in range(2):
        head = (lane < HEAD_DIM) if h == 0 else (lane >= HEAD_DIM)
        qh = jnp.where(head, q2, jnp.zeros_like(q2))
        g = lax.dot_general(km, qh, _NT, preferred_element_type=F32)
        g = jnp.where(nidx < qblk, g, NEG_INF)
        picked = jnp.zeros((nb, seq), jnp.int32)
        for _ in range(min(MOBA_TOPK, nb - 1)):
            top = jnp.max(g, axis=0, keepdims=True)
            first = jnp.min(jnp.where(g == top, nidx, nb), axis=0, keepdims=True)
            hit = nidx == first
            picked = jnp.where(hit, 1, picked)
            g = jnp.where(hit, TAKEN, g)
        allow = ((picked == 1) & (nidx < qblk)) | (nidx == qblk)
        bias = jnp.where(allow, 0.0, NEG_INF)
        bias = jnp.concatenate([bias, jnp.zeros((pair - nb, seq), F32)], axis=0)
        qs = (qh.astype(F32) * (HEAD_DIM ** -0.5 * LOG2_E)).T.astype(BF16)
        bias = bias.astype(BF16)
        for i in range(nb):
            qx_ref[i, 0:pair, h * blk:(h + 1) * blk] = qs[:, i * blk:(i + 1) * blk]
            qx_ref[i, pair:, h * blk:(h + 1) * blk] = bias[:, i * blk:(i + 1) * blk]

    m_ref[...] = jnp.full(m_ref.shape, NEG_INF, F32)
    acc_ref[...] = jnp.zeros_like(acc_ref)

    def step(k, carry, own_block, stages=(1, 2, 3)):
        for e in range(MOBA_ITEMS):
            c = MOBA_ITEMS * k + e
            for h in range(2):
                r = 2 * e + h
                if 3 in stages:
                    a3 = 2 * tab_ref[4, c] + h
                    pv = jnp.dot(vt_ref[tab_ref[5, c], h], p_ring[r], preferred_element_type=F32)
                    acc_ref[a3] = al_ring[r] * acc_ref[a3] + pv
                if 2 in stages:
                    a2 = 2 * tab_ref[3, c] + h
                    m_old = m_ref[a2]
                    m_new = jnp.maximum(m_old, cm_ring[r])
                    m_ref[a2] = m_new
                    al_ring[r] = jnp.exp2(m_old - m_new)
                    p_ring[r] = jnp.exp2(s_ring[r] - m_new).astype(BF16)
            if 1 in stages:
                s2 = jnp.dot(kx_ref[tab_ref[1, c]], qx_ref[tab_ref[0, c]], preferred_element_type=F32)
                for h in range(2):
                    s = s2[:, h * blk:(h + 1) * blk]
                    if own_block:
                        s = s + cb_ref[tab_ref[2, c]]
                    s_ring[2 * e + h] = s
                    cm_ring[2 * e + h] = jnp.max(s, axis=0, keepdims=True)
        return carry

    step(0, 0, own_block=False, stages=(1,))
    step(1, 0, own_block=False, stages=(1, 2))
    lax.fori_loop(2, n_past, functools.partial(step, own_block=False), 0)
    lax.fori_loop(n_past, n_groups, functools.partial(step, own_block=True), 0)
    step(n_groups, 0, own_block=False, stages=(2, 3))
    step(n_groups + 1, 0, own_block=False, stages=(3,))

    for i in range(nb):
        o = jnp.concatenate([acc_ref[2 * i + h, 0:HEAD_DIM, :] / acc_ref[2 * i + h, HEAD_DIM:HEAD_DIM + 1, :]
                             for h in range(2)], axis=0)
        o_ref[0, i * blk:(i + 1) * blk, :] = o.T.astype(o_ref.dtype)


def _moba(q, k, v):
    bsz, seq, width = q.shape
    blk = MOBA_BLOCK
    nb = seq // blk
    pair = 2 * HEAD_DIM
    tab, n_past, n_groups = _moba_tables(nb)
    spec = pl.BlockSpec((1, seq, pair), lambda b, hp, tab_ref: (b, 0, hp))
    n_state = 2 * nb
    n_ring = 2 * MOBA_ITEMS
    vrows = HEAD_DIM + BF16_SUBLANES
    return pl.pallas_call(
        functools.partial(_moba_body, nb=nb, n_past=n_past, n_groups=n_groups),
        grid_spec=pltpu.PrefetchScalarGridSpec(
            num_scalar_prefetch=1,
            grid=(bsz, width // pair),
            in_specs=[spec, spec, spec],
            out_specs=spec,
            scratch_shapes=[
                pltpu.VMEM((nb // 2, 2 * blk, 2 * pair), BF16),
                pltpu.VMEM((nb // 2, 2, vrows, 2 * blk), BF16),
                pltpu.VMEM((2, 2 * blk, blk), F32),
                pltpu.VMEM((nb, 2 * pair, 2 * blk), BF16),
                pltpu.VMEM((n_state, 1, blk), F32),
                pltpu.VMEM((n_state, vrows, blk), F32),
                pltpu.VMEM((n_ring, 2 * blk, blk), F32),
                pltpu.VMEM((n_ring, 1, blk), F32),
                pltpu.VMEM((n_ring, 2 * blk, blk), BF16),
                pltpu.VMEM((n_ring, 1, blk), F32),
            ]),
        out_shape=jax.ShapeDtypeStruct((bsz, seq, width), BF16),
        compiler_params=_params(("arbitrary", "arbitrary")),
        name="moba",
    )(jnp.asarray(tab), q, k, v)


def _post_body(x_ref, ya_ref, yb_ref, ga_ref, gb_ref, wa_ref, wb_ref, wo_ref, gpm_ref, gpf_ref,
               w1_ref, w2_ref, gpo_ref, o_ref):
    a = jnp.dot(ya_ref[...], wa_ref[...], preferred_element_type=F32)
    b = jnp.dot(yb_ref[...], wb_ref[...], preferred_element_type=F32)
    merged = (jax.nn.sigmoid(ga_ref[...].astype(F32)) * a
              + jax.nn.sigmoid(gb_ref[...].astype(F32)) * b)
    mix = jnp.dot(merged.astype(BF16), wo_ref[...], preferred_element_type=F32)
    h = x_ref[...] + _rms(mix, gpm_ref[...])
    f = _rms(h, gpf_ref[...]).astype(BF16)
    d_ff = w1_ref.shape[1]
    ck = min(FFN_CHUNK, d_ff)
    acc = jnp.zeros(h.shape, F32)
    for c in range(d_ff // ck):
        t = jnp.dot(f, w1_ref[:, c * ck:(c + 1) * ck], preferred_element_type=F32)
        t = jnp.square(jnp.maximum(t, 0.0)).astype(BF16)
        acc = acc + jnp.dot(t, w2_ref[c * ck:(c + 1) * ck, :], preferred_element_type=F32)
    o_ref[...] = h + _rms(acc, gpo_ref[...])


def _post(x2, ya, yb, ga, gb, wa, wb, wo, gpm, gpf, w1, w2, gpo):
    t, d = x2.shape
    tm = min(TOKEN_TILE, t)

    def rows(a):
        return pl.BlockSpec((tm, a.shape[1]), lambda i: (i, 0))

    consts = (wa, wb, wo, gpm, gpf, w1, w2, gpo)
    return pl.pallas_call(
        _post_body,
        grid=(t // tm,),
        in_specs=[rows(a) for a in (x2, ya, yb, ga, gb)] + [_resident(c.shape) for c in consts],
        out_specs=rows(x2),
        out_shape=jax.ShapeDtypeStruct((t, d), x2.dtype),
        compiler_params=_params(("parallel",)),
        name="post",
    )(x2, ya, yb, ga, gb, *consts)


def _layer(h, g_pre_mix, w_in, lam_re, lam_im, log_dt, b_re, b_im, c_re, c_im, d_skip, w_glu, b_glu,
           w_branch_a, w_branch_b, w_out, g_post_mix, g_pre_ffn, w_ff1, w_ff2, g_post_ffn):
    bsz, seq, d = h.shape
    ssm_w = lam_re.shape[0] * SSM_GROUP
    attn_w = ATTN_HEADS * HEAD_DIM
    x2 = h.reshape(bsz * seq, d)

    zs, q, k, v, ga, gb = _inproj(h, g_pre_mix.reshape(1, d), w_in.astype(BF16),
                                  (ssm_w, attn_w, attn_w, attn_w, d, d))

    ar, ai, bbr, bbi = _s5_prep(lam_re, lam_im, log_dt, b_re, b_im)
    g, p = lam_re.shape
    hg = g // 2
    n_half = hg * p

    def halves(m):
        return jnp.stack([_block_diag(m[:hg]), _block_diag(m[hg:])])

    bmat = jnp.concatenate([halves(bbr), halves(bbi)], axis=2).astype(BF16)
    cre = halves(jnp.transpose(c_re, (0, 2, 1))).astype(BF16)
    cim = halves(jnp.transpose(c_im, (0, 2, 1))).astype(BF16)
    ar8 = jnp.broadcast_to(ar.reshape(1, 2 * n_half), (bsz, 2 * n_half))
    ai8 = jnp.broadcast_to(ai.reshape(1, 2 * n_half), (bsz, 2 * n_half))

    ya = _s5_mixer(zs.reshape(seq * bsz, ssm_w), bsz, bmat, ar8, ai8, cre, cim, d_skip.reshape(1, ssm_w),
                   w_glu.astype(BF16), b_glu.reshape(1, ssm_w)).reshape(bsz * seq, ssm_w)
    yb = _moba(q, k, v).reshape(bsz * seq, attn_w)
    ga, gb = ga.reshape(bsz * seq, d), gb.reshape(bsz * seq, d)

    out = _post(x2, ya, yb, ga, gb, w_branch_a.astype(BF16), w_branch_b.astype(BF16),
                w_out.astype(BF16), g_post_mix.reshape(1, d), g_pre_ffn.reshape(1, d),
                w_ff1.astype(BF16), w_ff2.astype(BF16), g_post_ffn.reshape(1, d))
    return out.reshape(bsz, seq, d)


def kernel(x, g_pre_mix, w_in, lam_re, lam_im, log_dt, b_re, b_im, c_re, c_im, d_skip, w_glu, b_glu,
           w_branch_a, w_branch_b, w_out, g_post_mix, g_pre_ffn, w_ff1, w_ff2, g_post_ffn):
    per_layer = (g_pre_mix, w_in, lam_re, lam_im, log_dt, b_re, b_im, c_re, c_im, d_skip, w_glu, b_glu,
                 w_branch_a, w_branch_b, w_out, g_post_mix, g_pre_ffn, w_ff1, w_ff2, g_post_ffn)
    h = x
    for l in range(g_pre_mix.shape[0]):
        h = _layer(h, *(p[l] for p in per_layer))
    return h
```

```python
import functools

import jax
import jax.numpy as jnp
import numpy as np
from jax import lax
from jax.experimental import pallas as pl
from jax.experimental.pallas import tpu as pltpu

F32 = jnp.float32
BF16 = jnp.bfloat16

RMS_EPS = 1e-6
NEG_INF = -1e30
TAKEN = -3e38

SSM_GROUP = 16
SSM_STATE = 64
ATTN_HEADS = 8
HEAD_DIM = 64
MOBA_BLOCK = 256
MOBA_TOPK = 3
MOBA_ITEMS = 4

LOG2_E = 1.4426950408889634
MOBA_Q_SCALE = LOG2_E * HEAD_DIM ** -0.5

SUBLANES = 8
BF16_SUBLANES = 16
LANES = 128
VMEM_LIMIT_BYTES = 56 * 1024 * 1024

TOKEN_TILE = 512
S5_TIME_CHUNK = 64
S5_COL_GROUP = 512
FFN_CHUNK = 1024

def _params(sem):
    return pltpu.CompilerParams(dimension_semantics=sem, vmem_limit_bytes=VMEM_LIMIT_BYTES)


def _resident(shape):
    zeros = (0,) * len(shape)
    return pl.BlockSpec(shape, lambda *_: zeros, pipeline_mode=pl.Buffered(1))


def _rms(x, g):
    var = jnp.mean(x * x, axis=-1, keepdims=True)
    return (x * lax.rsqrt(var + RMS_EPS)) * g


def _inproj_body(x_ref, g_ref, w_ref, *out_refs, layouts, scales):
    bsz, tl, d = x_ref.shape
    ub = _rms(x_ref[...].reshape(bsz * tl, d), g_ref[...]).astype(BF16)
    col = 0
    for ref, layout, scale in zip(out_refs, layouts, scales):
        n = ref.shape[1] if layout == "bcl" else ref.shape[2]
        z = jnp.dot(ub, w_ref[:, col:col + n], preferred_element_type=F32).reshape(bsz, tl, n)
        if scale != 1.0:
            z = z * scale
        if layout == "lbc":
            z = jnp.swapaxes(z.astype(ref.dtype), 0, 1)
        elif layout == "bcl":
            z = jnp.swapaxes(z, 1, 2)
        ref[...] = z.astype(ref.dtype)
        col += n


def _inproj(x, g, w, widths, layouts, scales):
    bsz, seq, d = x.shape
    tl = min(2 * TOKEN_TILE // bsz, seq)
    specs = {"blc": lambda n: (pl.BlockSpec((bsz, tl, n), lambda i: (0, i, 0)), (bsz, seq, n)),
             "lbc": lambda n: (pl.BlockSpec((tl, bsz, n), lambda i: (i, 0, 0)), (seq, bsz, n)),
             "bcl": lambda n: (pl.BlockSpec((bsz, n, tl), lambda i: (0, 0, i)), (bsz, n, seq))}
    outs = [specs[layout](n) for n, layout in zip(widths, layouts)]
    return pl.pallas_call(
        functools.partial(_inproj_body, layouts=layouts, scales=scales),
        grid=(seq // tl,),
        in_specs=[pl.BlockSpec((bsz, tl, d), lambda i: (0, i, 0)), _resident(g.shape), _resident(w.shape)],
        out_specs=[spec for spec, _ in outs],
        out_shape=[jax.ShapeDtypeStruct(shape, BF16) for _, shape in outs],
        compiler_params=_params(("parallel",)),
        name="inproj",
    )(x, g, w)


def _s5_prep_body(lr_ref, li_ref, ldt_ref, br_ref, bi_ref, ar_ref, ai_ref, bbr_ref, bbi_ref):
    lr = lr_ref[...]
    li = li_ref[...]
    dt = jnp.exp(ldt_ref[...])
    mag = jnp.exp(lr * dt)
    ab_re = mag * jnp.cos(li * dt)
    ab_im = mag * jnp.sin(li * dt)
    nr = ab_re - 1.0
    ni = ab_im
    den = lr * lr + li * li
    coef_re = (nr * lr + ni * li) / den
    coef_im = (ni * lr - nr * li) / den
    br = br_ref[...]
    bi = bi_ref[...]
    ar_ref[...] = ab_re
    ai_ref[...] = ab_im
    bbr_ref[...] = coef_re * br - coef_im * bi
    bbi_ref[...] = coef_re * bi + coef_im * br


def _s5_prep(lam_re, lam_im, log_dt, b_re, b_im):
    g, p = lam_re.shape
    h = b_re.shape[-1]
    lr = lam_re.reshape(g, 1, p)
    li = lam_im.reshape(g, 1, p)
    ldt = jnp.broadcast_to(log_dt.reshape(g, 1, 1), (g, 1, p))
    br = jnp.transpose(b_re, (0, 2, 1))
    bi = jnp.transpose(b_im, (0, 2, 1))
    small = jax.ShapeDtypeStruct((g, 1, p), F32)
    big = jax.ShapeDtypeStruct((g, h, p), F32)
    return pl.pallas_call(_s5_prep_body, out_shape=[small, small, big, big], name="s5_prep")(lr, li, ldt, br, bi)


def _block_diag(m):
    g, r, c = m.shape
    on_diag = jnp.eye(g, dtype=bool)[:, None, :, None]
    return jnp.where(on_diag, m[:, :, None, :], jnp.zeros((), m.dtype)).reshape(g * r, g * c)


def _s5_body(u_ref, b_ref, ar_ref, ai_ref, cre_ref, cim_ref, dsk_ref, wg_ref, bg_ref, y_ref,
             bu_ref, st_ref, *, steps):
    half_in = u_ref.shape[1] // 2
    half_st = st_ref.shape[1] // 4

    @pl.when(pl.program_id(0) == 0)
    def _():
        st_ref[...] = jnp.zeros_like(st_ref)

    for c in range(2):
        bu_ref[:, 2 * c * half_st:2 * (c + 1) * half_st] = jnp.dot(
            u_ref[:, c * half_in:(c + 1) * half_in], b_ref[c], preferred_element_type=F32)

    w = S5_COL_GROUP
    for c in range(2):
        for s in range(half_st // w):
            re0 = 2 * c * half_st + s * w
            im0 = re0 + half_st
            a0 = c * half_st + s * w
            ar = ar_ref[:, a0:a0 + w]
            ai = ai_ref[:, a0:a0 + w]
            xr = st_ref[:, re0:re0 + w]
            xi = st_ref[:, im0:im0 + w]
            for t in range(steps):
                rows = slice(t * SUBLANES, (t + 1) * SUBLANES)
                xr, xi = (ar * xr - ai * xi + bu_ref[rows, re0:re0 + w],
                          ar * xi + ai * xr + bu_ref[rows, im0:im0 + w])
                bu_ref[rows, re0:re0 + w] = xr
                bu_ref[rows, im0:im0 + w] = xi
            st_ref[:, re0:re0 + w] = xr
            st_ref[:, im0:im0 + w] = xi

    ys = []
    for c in range(2):
        xre = bu_ref[:, 2 * c * half_st:(2 * c + 1) * half_st].astype(BF16)
        xim = bu_ref[:, (2 * c + 1) * half_st:(2 * c + 2) * half_st].astype(BF16)
        ys.append(jnp.dot(xre, cre_ref[c], preferred_element_type=F32)
                  - jnp.dot(xim, cim_ref[c], preferred_element_type=F32))
    y = jnp.concatenate(ys, axis=1) + dsk_ref[...] * u_ref[...].astype(F32)
    v = jax.nn.gelu(y)
    gate = jnp.dot(v.astype(BF16), wg_ref[...], preferred_element_type=F32) + bg_ref[...]
    out = (v * jax.nn.sigmoid(gate)).astype(y_ref.dtype)
    bsz = y_ref.shape[0]
    y_ref[...] = jnp.swapaxes(out.reshape(steps, bsz, out.shape[1]), 0, 1)


def _s5_mixer(u_tm, bsz, bmat, ar, ai, cre, cim, d_skip, w_glu, b_glu):
    rows, width = u_tm.shape
    seq = rows // bsz
    steps = min(S5_TIME_CHUNK, seq)
    r = steps * bsz
    n_state = ar.shape[1] * 2
    return pl.pallas_call(
        functools.partial(_s5_body, steps=steps),
        grid=(seq // steps,),
        in_specs=[pl.BlockSpec((r, width), lambda i: (i, 0)),
                  _resident(bmat.shape), _resident(ar.shape), _resident(ai.shape),
                  _resident(cre.shape), _resident(cim.shape), _resident(d_skip.shape),
                  _resident(w_glu.shape), _resident(b_glu.shape)],
        out_specs=pl.BlockSpec((bsz, steps, width), lambda i: (0, i, 0)),
        out_shape=jax.ShapeDtypeStruct((bsz, seq, width), BF16),
        scratch_shapes=[pltpu.VMEM((r, n_state), F32), pltpu.VMEM((bsz, n_state), F32)],
        compiler_params=_params(("arbitrary",)),
        name="s5_mixer",
    )(u_tm, bmat, ar, ai, cre, cim, d_skip, w_glu, b_glu)


def _moba_tables(nb):
    per = MOBA_ITEMS
    items = [(i, jp, 0) for i in range(nb) for jp in range(i // 2)]
    n_past, rest = divmod(len(items), per)
    items += [(i, i // 2, 0 if i % 2 else 1) for i in range(nb)]
    n_groups, rest2 = divmod(len(items), per)
    assert nb % 2 == 0 and not rest and not rest2 and 2 <= n_past < n_groups
    tab = np.zeros((6, per * (n_groups + 2)), np.int32)
    for c in range(tab.shape[1]):
        if c < len(items):
            tab[0:3, c] = items[c]
        if 0 <= c - per < len(items):
            tab[3, c] = items[c - per][0]
        if 0 <= c - 2 * per < len(items):
            tab[4:6, c] = items[c - 2 * per][0:2]
    return tab, n_past, n_groups


def _moba_body(tab_ref, q_ref, k_ref, v_ref, o_ref, kx_ref, vt_ref, cb_ref, qx_ref,
               m_ref, acc_ref, s_ring, cm_ring, p_ring, al_ring, *, nb, n_past, n_groups):
    blk = MOBA_BLOCK
    pair = 2 * HEAD_DIM
    seq = nb * blk
    two = 2 * blk
    vrows = vt_ref.shape[2]

    @pl.when((pl.program_id(0) == 0) & (pl.program_id(1) == 0))
    def _():
        row = lax.broadcasted_iota(jnp.int32, (two, pair), 0)
        lane = lax.broadcasted_iota(jnp.int32, (two, pair), 1)
        ones_row = lax.broadcasted_iota(jnp.int32, (vrows - HEAD_DIM, two), 0) == 0
        for jp in range(nb // 2):
            block = jnp.where(row < blk, 2 * jp, 2 * jp + 1)
            kx_ref[jp, :, pair:] = jnp.where(lane == block, 1.0, 0.0).astype(BF16)
            for h in range(2):
                vt_ref[jp, h, HEAD_DIM:, :] = jnp.where(ones_row, 1.0, 0.0).astype(BF16)
        kpos = lax.broadcasted_iota(jnp.int32, (blk, blk), 0)
        qpos = lax.broadcasted_iota(jnp.int32, (blk, blk), 1)
        tri = jnp.where(kpos <= qpos, 0.0, NEG_INF)
        for i in range(nb):
            for h in range(2):
                other = slice((1 - h) * HEAD_DIM, (2 - h) * HEAD_DIM)
                qx_ref[i, other, h * blk:(h + 1) * blk] = jnp.zeros((HEAD_DIM, blk), BF16)
        cb_ref[0] = jnp.concatenate([jnp.zeros((blk, blk), F32), tri], axis=0)
        cb_ref[1] = jnp.concatenate([tri, jnp.full((blk, blk), NEG_INF, F32)], axis=0)

    for jp in range(nb // 2):
        cols = slice(jp * two, (jp + 1) * two)
        kx_ref[jp, :, 0:pair] = k_ref[0, cols, :]
        for h in range(2):
            vt_ref[jp, h, 0:HEAD_DIM, :] = v_ref[0, h * HEAD_DIM:(h + 1) * HEAD_DIM, cols]
    km = jnp.mean(k_ref[0].astype(F32).reshape(nb, blk, pair), axis=1)

    q2 = q_ref[0]
    kdim = lax.broadcasted_iota(jnp.int32, km.shape, 1)
    nidx = lax.broadcasted_iota(jnp.int32, (nb, seq), 0)
    qblk = lax.shift_right_logical(lax.broadcasted_iota(jnp.int32, (nb, seq), 1), blk.bit_length() - 1)
    for h in range(2):
        dims = slice(h * HEAD_DIM, (h + 1) * HEAD_DIM)
        km_h = jnp.where((kdim < HEAD_DIM) if h == 0 else (kdim >= HEAD_DIM), km, 0.0).astype(BF16)
        g = jnp.dot(km_h, q2, preferred_element_type=F32)
        g = jnp.where(nidx < qblk, g, NEG_INF)
        picked = jnp.zeros((nb, seq), jnp.int32)
        for _ in range(min(MOBA_TOPK, nb - 1)):
            top = jnp.max(g, axis=0, keepdims=True)
            first = jnp.min(jnp.where(g == top, nidx, nb), axis=0, keepdims=True)
            hit = nidx == first
            picked = jnp.where(hit, 1, picked)
            g = jnp.where(hit, TAKEN, g)
        allow = ((picked == 1) & (nidx < qblk)) | (nidx == qblk)
        bias = jnp.where(allow, 0.0, NEG_INF)
        bias = jnp.concatenate([bias, jnp.zeros((pair - nb, seq), F32)], axis=0).astype(BF16)
        for i in range(nb):
            qx_ref[i, dims, h * blk:(h + 1) * blk] = q_ref[0, dims, i * blk:(i + 1) * blk]
            qx_ref[i, pair:, h * blk:(h + 1) * blk] = bias[:, i * blk:(i + 1) * blk]

    m_ref[...] = jnp.full(m_ref.shape, NEG_INF, F32)
    acc_ref[...] = jnp.zeros_like(acc_ref)

    def step(k, carry, own_block, stages=(1, 2, 3)):
        for e in range(MOBA_ITEMS):
            c = MOBA_ITEMS * k + e
            for h in range(2):
                r = 2 * e + h
                if 3 in stages:
                    a3 = 2 * tab_ref[4, c] + h
                    pv = jnp.dot(vt_ref[tab_ref[5, c], h], p_ring[r], preferred_element_type=F32)
                    acc_ref[a3] = al_ring[r] * acc_ref[a3] + pv
                if 2 in stages:
                    a2 = 2 * tab_ref[3, c] + h
                    m_old = m_ref[a2]
                    m_new = jnp.maximum(m_old, cm_ring[r])
                    m_ref[a2] = m_new
                    al_ring[r] = jnp.exp2(m_old - m_new)
                    p_ring[r] = jnp.exp2(s_ring[r] - m_new).astype(BF16)
            if 1 in stages:
                s2 = jnp.dot(kx_ref[tab_ref[1, c]], qx_ref[tab_ref[0, c]], preferred_element_type=F32)
                for h in range(2):
                    s = s2[:, h * blk:(h + 1) * blk]
                    if own_block:
                        s = s + cb_ref[tab_ref[2, c]]
                    s_ring[2 * e + h] = s
                    cm_ring[2 * e + h] = jnp.max(s, axis=0, keepdims=True)
        return carry

    step(0, 0, own_block=False, stages=(1,))
    step(1, 0, own_block=False, stages=(1, 2))
    lax.fori_loop(2, n_past, functools.partial(step, own_block=False), 0)
    lax.fori_loop(n_past, n_groups, functools.partial(step, own_block=True), 0)
    step(n_groups, 0, own_block=False, stages=(2, 3))
    step(n_groups + 1, 0, own_block=False, stages=(3,))

    for i in range(nb):
        o = jnp.concatenate([acc_ref[2 * i + h, 0:HEAD_DIM, :] / acc_ref[2 * i + h, HEAD_DIM:HEAD_DIM + 1, :]
                             for h in range(2)], axis=0)
        o_ref[0, i * blk:(i + 1) * blk, :] = o.T.astype(o_ref.dtype)


def _moba(qt, k, vt):
    bsz, seq, width = k.shape
    blk = MOBA_BLOCK
    nb = seq // blk
    pair = 2 * HEAD_DIM
    tab, n_past, n_groups = _moba_tables(nb)
    spec = pl.BlockSpec((1, seq, pair), lambda b, hp, tab_ref: (b, 0, hp))
    spec_t = pl.BlockSpec((1, pair, seq), lambda b, hp, tab_ref: (b, hp, 0))
    n_state = 2 * nb
    n_ring = 2 * MOBA_ITEMS
    vrows = HEAD_DIM + BF16_SUBLANES
    return pl.pallas_call(
        functools.partial(_moba_body, nb=nb, n_past=n_past, n_groups=n_groups),
        grid_spec=pltpu.PrefetchScalarGridSpec(
            num_scalar_prefetch=1,
            grid=(bsz, width // pair),
            in_specs=[spec_t, spec, spec_t],
            out_specs=spec,
            scratch_shapes=[
                pltpu.VMEM((nb // 2, 2 * blk, 2 * pair), BF16),
                pltpu.VMEM((nb // 2, 2, vrows, 2 * blk), BF16),
                pltpu.VMEM((2, 2 * blk, blk), F32),
                pltpu.VMEM((nb, 2 * pair, 2 * blk), BF16),
                pltpu.VMEM((n_state, 1, blk), F32),
                pltpu.VMEM((n_state, vrows, blk), F32),
                pltpu.VMEM((n_ring, 2 * blk, blk), F32),
                pltpu.VMEM((n_ring, 1, blk), F32),
                pltpu.VMEM((n_ring, 2 * blk, blk), BF16),
                pltpu.VMEM((n_ring, 1, blk), F32),
            ]),
        out_shape=jax.ShapeDtypeStruct((bsz, seq, width), BF16),
        compiler_params=_params(("arbitrary", "arbitrary")),
        name="moba",
    )(jnp.asarray(tab), qt, k, vt)


def _post_body(x_ref, ya_ref, yb_ref, ga_ref, gb_ref, wa_ref, wb_ref, wo_ref, gpm_ref, gpf_ref,
               w1_ref, w2_ref, gpo_ref, o_ref):
    a = jnp.dot(ya_ref[...], wa_ref[...], preferred_element_type=F32)
    b = jnp.dot(yb_ref[...], wb_ref[...], preferred_element_type=F32)
    merged = (jax.nn.sigmoid(ga_ref[...].astype(F32)) * a
              + jax.nn.sigmoid(gb_ref[...].astype(F32)) * b)
    mix = jnp.dot(merged.astype(BF16), wo_ref[...], preferred_element_type=F32)
    h = x_ref[...] + _rms(mix, gpm_ref[...])
    f = _rms(h, gpf_ref[...]).astype(BF16)
    d_ff = w1_ref.shape[1]
    ck = min(FFN_CHUNK, d_ff)
    acc = jnp.zeros(h.shape, F32)
    for c in range(d_ff // ck):
        t = jnp.dot(f, w1_ref[:, c * ck:(c + 1) * ck], preferred_element_type=F32)
        t = jnp.square(jnp.maximum(t, 0.0)).astype(BF16)
        acc = acc + jnp.dot(t, w2_ref[c * ck:(c + 1) * ck, :], preferred_element_type=F32)
    o_ref[...] = h + _rms(acc, gpo_ref[...])


def _post(x2, ya, yb, ga, gb, wa, wb, wo, gpm, gpf, w1, w2, gpo):
    t, d = x2.shape
    tm = min(TOKEN_TILE, t)

    def rows(a):
        return pl.BlockSpec((tm, a.shape[1]), lambda i: (i, 0))

    consts = (wa, wb, wo, gpm, gpf, w1, w2, gpo)
    return pl.pallas_call(
        _post_body,
        grid=(t // tm,),
        in_specs=[rows(a) for a in (x2, ya, yb, ga, gb)] + [_resident(c.shape) for c in consts],
        out_specs=rows(x2),
        out_shape=jax.ShapeDtypeStruct((t, d), x2.dtype),
        compiler_params=_params(("parallel",)),
        name="post",
    )(x2, ya, yb, ga, gb, *consts)


def _layer(h, g_pre_mix, w_in, lam_re, lam_im, log_dt, b_re, b_im, c_re, c_im, d_skip, w_glu, b_glu,
           w_branch_a, w_branch_b, w_out, g_post_mix, g_pre_ffn, w_ff1, w_ff2, g_post_ffn):
    bsz, seq, d = h.shape
    ssm_w = lam_re.shape[0] * SSM_GROUP
    attn_w = ATTN_HEADS * HEAD_DIM
    x2 = h.reshape(bsz * seq, d)

    zs, qt, k, vt, ga, gb = _inproj(h, g_pre_mix.reshape(1, d), w_in.astype(BF16),
                                    (ssm_w, attn_w, attn_w, attn_w, d, d),
                                    ("lbc", "bcl", "blc", "bcl", "blc", "blc"),
                                    (1.0, MOBA_Q_SCALE, 1.0, 1.0, 1.0, 1.0))

    ar, ai, bbr, bbi = _s5_prep(lam_re, lam_im, log_dt, b_re, b_im)
    g, p = lam_re.shape
    hg = g // 2
    n_half = hg * p

    def halves(m):
        return jnp.stack([_block_diag(m[:hg]), _block_diag(m[hg:])])

    bmat = jnp.concatenate([halves(bbr), halves(bbi)], axis=2).astype(BF16)
    cre = halves(jnp.transpose(c_re, (0, 2, 1))).astype(BF16)
    cim = halves(jnp.transpose(c_im, (0, 2, 1))).astype(BF16)
    ar8 = jnp.broadcast_to(ar.reshape(1, 2 * n_half), (bsz, 2 * n_half))
    ai8 = jnp.broadcast_to(ai.reshape(1, 2 * n_half), (bsz, 2 * n_half))

    ya = _s5_mixer(zs.reshape(seq * bsz, ssm_w), bsz, bmat, ar8, ai8, cre, cim, d_skip.reshape(1, ssm_w),
                   w_glu.astype(BF16), b_glu.reshape(1, ssm_w)).reshape(bsz * seq, ssm_w)
    yb = _moba(qt, k, vt).reshape(bsz * seq, attn_w)
    ga, gb = ga.reshape(bsz * seq, d), gb.reshape(bsz * seq, d)

    out = _post(x2, ya, yb, ga, gb, w_branch_a.astype(BF16), w_branch_b.astype(BF16),
                w_out.astype(BF16), g_post_mix.reshape(1, d), g_pre_ffn.reshape(1, d),
                w_ff1.astype(BF16), w_ff2.astype(BF16), g_post_ffn.reshape(1, d))
    return out.reshape(bsz, seq, d)


def kernel(x, g_pre_mix, w_in, lam_re, lam_im, log_dt, b_re, b_im, c_re, c_im, d_skip, w_glu, b_glu,
           w_branch_a, w_branch_b, w_out, g_post_mix, g_pre_ffn, w_ff1, w_ff2, g_post_ffn):
    per_layer = (g_pre_mix, w_in, lam_re, lam_im, log_dt, b_re, b_im, c_re, c_im, d_skip, w_glu, b_glu,
                 w_branch_a, w_branch_b, w_out, g_post_mix, g_pre_ffn, w_ff1, w_ff2, g_post_ffn)
    h = x
    for l in range(g_pre_mix.shape[0]):
        h = _layer(h, *(p[l] for p in per_layer))
    return h
```

```python
import functools

import jax
import jax.numpy as jnp
import numpy as np
from jax import lax
from jax.experimental import pallas as pl
from jax.experimental.pallas import tpu as pltpu

F32 = jnp.float32
BF16 = jnp.bfloat16

RMS_EPS = 1e-6
NEG_INF = -1e30
TAKEN = -3e38

SSM_GROUP = 16
ATTN_HEADS = 8
HEAD_DIM = 64
MOBA_BLOCK = 256
MOBA_TOPK = 3
MOBA_ITEMS = 4

LOG2_E = 1.4426950408889634
MOBA_Q_SCALE = LOG2_E * HEAD_DIM ** -0.5

SUBLANES = 8
BF16_SUBLANES = 16
VMEM_LIMIT_BYTES = 56 * 1024 * 1024

INPROJ_ROWS = 1024
POST_ROWS = 512
S5_TIME_CHUNK = 64
S5_COL_GROUP = 512
FFN_CHUNK = 1024


def _params(sem):
    return pltpu.CompilerParams(dimension_semantics=sem, vmem_limit_bytes=VMEM_LIMIT_BYTES)


def _resident(shape):
    zeros = (0,) * len(shape)
    return pl.BlockSpec(shape, lambda *_: zeros, pipeline_mode=pl.Buffered(1))


def _rms(x, g):
    var = jnp.mean(x * x, axis=-1, keepdims=True)
    return (x * lax.rsqrt(var + RMS_EPS)) * g


def _inproj_body(x_ref, g_ref, w_ref, *out_refs, layouts, scales):
    bsz, tl, d = x_ref.shape
    ub = _rms(x_ref[...].reshape(bsz * tl, d), g_ref[...]).astype(BF16)
    col = 0
    for ref, layout, scale in zip(out_refs, layouts, scales):
        n = ref.shape[1] if layout == "bcl" else ref.shape[2]
        z = jnp.dot(ub, w_ref[:, col:col + n], preferred_element_type=F32).reshape(bsz, tl, n)
        if scale != 1.0:
            z = z * scale
        if layout == "lbc":
            z = jnp.swapaxes(z.astype(ref.dtype), 0, 1)
        elif layout == "bcl":
            z = jnp.swapaxes(z, 1, 2)
        ref[...] = z.astype(ref.dtype)
        col += n


def _inproj(x, g, w, widths, layouts, scales):
    bsz, seq, d = x.shape
    tl = min(INPROJ_ROWS // bsz, seq)
    specs = {"blc": lambda n: (pl.BlockSpec((bsz, tl, n), lambda i: (0, i, 0)), (bsz, seq, n)),
             "lbc": lambda n: (pl.BlockSpec((tl, bsz, n), lambda i: (i, 0, 0)), (seq, bsz, n)),
             "bcl": lambda n: (pl.BlockSpec((bsz, n, tl), lambda i: (0, 0, i)), (bsz, n, seq))}
    outs = [specs[layout](n) for n, layout in zip(widths, layouts)]
    return pl.pallas_call(
        functools.partial(_inproj_body, layouts=layouts, scales=scales),
        grid=(seq // tl,),
        in_specs=[pl.BlockSpec((bsz, tl, d), lambda i: (0, i, 0)), _resident(g.shape), _resident(w.shape)],
        out_specs=[spec for spec, _ in outs],
        out_shape=[jax.ShapeDtypeStruct(shape, BF16) for _, shape in outs],
        compiler_params=_params(("parallel",)),
        name="inproj",
    )(x, g, w)


def _s5_prep_body(lr_ref, li_ref, ldt_ref, br_ref, bi_ref, ar_ref, ai_ref, bbr_ref, bbi_ref):
    lr = lr_ref[...]
    li = li_ref[...]
    dt = jnp.exp(ldt_ref[...])
    mag = jnp.exp(lr * dt)
    ab_re = mag * jnp.cos(li * dt)
    ab_im = mag * jnp.sin(li * dt)
    nr = ab_re - 1.0
    ni = ab_im
    den = lr * lr + li * li
    coef_re = (nr * lr + ni * li) / den
    coef_im = (ni * lr - nr * li) / den
    br = br_ref[...]
    bi = bi_ref[...]
    ar_ref[...] = ab_re
    ai_ref[...] = ab_im
    bbr_ref[...] = coef_re * br - coef_im * bi
    bbi_ref[...] = coef_re * bi + coef_im * br


def _s5_prep(lam_re, lam_im, log_dt, b_re, b_im):
    g, p = lam_re.shape
    h = b_re.shape[-1]
    lr = lam_re.reshape(g, 1, p)
    li = lam_im.reshape(g, 1, p)
    ldt = jnp.broadcast_to(log_dt.reshape(g, 1, 1), (g, 1, p))
    br = jnp.transpose(b_re, (0, 2, 1))
    bi = jnp.transpose(b_im, (0, 2, 1))
    small = jax.ShapeDtypeStruct((g, 1, p), F32)
    big = jax.ShapeDtypeStruct((g, h, p), F32)
    return pl.pallas_call(_s5_prep_body, out_shape=[small, small, big, big], name="s5_prep")(lr, li, ldt, br, bi)


def _block_diag(m):
    g, r, c = m.shape
    on_diag = jnp.eye(g, dtype=bool)[:, None, :, None]
    return jnp.where(on_diag, m[:, :, None, :], jnp.zeros((), m.dtype)).reshape(g * r, g * c)


def _s5_body(u_ref, b_ref, ar_ref, ai_ref, cre_ref, cim_ref, dsk_ref, wg_ref, bg_ref, y_ref,
             bu_ref, st_ref, *, steps):
    half_in = u_ref.shape[1] // 2
    half_st = st_ref.shape[1] // 4

    @pl.when(pl.program_id(0) == 0)
    def _():
        st_ref[...] = jnp.zeros_like(st_ref)

    for c in range(2):
        bu_ref[:, 2 * c * half_st:2 * (c + 1) * half_st] = jnp.dot(
            u_ref[:, c * half_in:(c + 1) * half_in], b_ref[c], preferred_element_type=F32)

    w = S5_COL_GROUP
    for c in range(2):
        for s in range(half_st // w):
            re0 = 2 * c * half_st + s * w
            im0 = re0 + half_st
            a0 = c * half_st + s * w
            ar = ar_ref[:, a0:a0 + w]
            ai = ai_ref[:, a0:a0 + w]
            xr = st_ref[:, re0:re0 + w]
            xi = st_ref[:, im0:im0 + w]
            for t in range(steps):
                rows = slice(t * SUBLANES, (t + 1) * SUBLANES)
                xr, xi = (ar * xr - ai * xi + bu_ref[rows, re0:re0 + w],
                          ar * xi + ai * xr + bu_ref[rows, im0:im0 + w])
                bu_ref[rows, re0:re0 + w] = xr
                bu_ref[rows, im0:im0 + w] = xi
            st_ref[:, re0:re0 + w] = xr
            st_ref[:, im0:im0 + w] = xi

    ys = []
    for c in range(2):
        xre = bu_ref[:, 2 * c * half_st:(2 * c + 1) * half_st].astype(BF16)
        xim = bu_ref[:, (2 * c + 1) * half_st:(2 * c + 2) * half_st].astype(BF16)
        ys.append(jnp.dot(xre, cre_ref[c], preferred_element_type=F32)
                  - jnp.dot(xim, cim_ref[c], preferred_element_type=F32))
    y = jnp.concatenate(ys, axis=1) + dsk_ref[...] * u_ref[...].astype(F32)
    v = jax.nn.gelu(y)
    gate = jnp.dot(v.astype(BF16), wg_ref[...], preferred_element_type=F32) + bg_ref[...]
    out = (v * jax.nn.sigmoid(gate)).astype(y_ref.dtype)
    bsz = y_ref.shape[0]
    y_ref[...] = jnp.swapaxes(out.reshape(steps, bsz, out.shape[1]), 0, 1)


def _s5_mixer(u_tm, bsz, bmat, ar, ai, cre, cim, d_skip, w_glu, b_glu):
    rows, width = u_tm.shape
    seq = rows // bsz
    steps = min(S5_TIME_CHUNK, seq)
    r = steps * bsz
    n_state = ar.shape[1] * 2
    return pl.pallas_call(
        functools.partial(_s5_body, steps=steps),
        grid=(seq // steps,),
        in_specs=[pl.BlockSpec((r, width), lambda i: (i, 0)),
                  _resident(bmat.shape), _resident(ar.shape), _resident(ai.shape),
                  _resident(cre.shape), _resident(cim.shape), _resident(d_skip.shape),
                  _resident(w_glu.shape), _resident(b_glu.shape)],
        out_specs=pl.BlockSpec((bsz, steps, width), lambda i: (0, i, 0)),
        out_shape=jax.ShapeDtypeStruct((bsz, seq, width), BF16),
        scratch_shapes=[pltpu.VMEM((r, n_state), F32), pltpu.VMEM((bsz, n_state), F32)],
        compiler_params=_params(("arbitrary",)),
        name="s5_mixer",
    )(u_tm, bmat, ar, ai, cre, cim, d_skip, w_glu, b_glu)


def _moba_tables(nb):
    per = MOBA_ITEMS
    items = [(i, jp, 0) for i in range(nb) for jp in range(i // 2)]
    n_past, rest = divmod(len(items), per)
    items += [(i, i // 2, 0 if i % 2 else 1) for i in range(nb)]
    n_groups, rest2 = divmod(len(items), per)
    assert nb % 2 == 0 and not rest and not rest2 and 2 <= n_past < n_groups
    tab = np.zeros((6, per * (n_groups + 2)), np.int32)
    for c in range(tab.shape[1]):
        if c < len(items):
            tab[0:3, c] = items[c]
        if 0 <= c - per < len(items):
            tab[3, c] = items[c - per][0]
        if 0 <= c - 2 * per < len(items):
            tab[4:6, c] = items[c - 2 * per][0:2]
    return tab, n_past, n_groups


def _moba_body(tab_ref, q_ref, k_ref, v_ref, o_ref, kx_ref, vt_ref, cb_ref, qx_ref,
               m_ref, acc_ref, s_ring, cm_ring, p_ring, al_ring, *, nb, n_past, n_groups):
    blk = MOBA_BLOCK
    pair = 2 * HEAD_DIM
    seq = nb * blk
    two = 2 * blk
    vrows = vt_ref.shape[2]

    @pl.when((pl.program_id(0) == 0) & (pl.program_id(1) == 0))
    def _():
        row = lax.broadcasted_iota(jnp.int32, (two, pair), 0)
        lane = lax.broadcasted_iota(jnp.int32, (two, pair), 1)
        ones_row = lax.broadcasted_iota(jnp.int32, (vrows - HEAD_DIM, two), 0) == 0
        for jp in range(nb // 2):
            block = jnp.where(row < blk, 2 * jp, 2 * jp + 1)
            kx_ref[jp, :, pair:] = jnp.where(lane == block, 1.0, 0.0).astype(BF16)
            for h in range(2):
                vt_ref[jp, h, HEAD_DIM:, :] = jnp.where(ones_row, 1.0, 0.0).astype(BF16)
        kpos = lax.broadcasted_iota(jnp.int32, (blk, blk), 0)
        qpos = lax.broadcasted_iota(jnp.int32, (blk, blk), 1)
        tri = jnp.where(kpos <= qpos, 0.0, NEG_INF)
        for i in range(nb):
            for h in range(2):
                other = slice((1 - h) * HEAD_DIM, (2 - h) * HEAD_DIM)
                qx_ref[i, other, h * blk:(h + 1) * blk] = jnp.zeros((HEAD_DIM, blk), BF16)
        cb_ref[0] = jnp.concatenate([jnp.zeros((blk, blk), F32), tri], axis=0)
        cb_ref[1] = jnp.concatenate([tri, jnp.full((blk, blk), NEG_INF, F32)], axis=0)

    for jp in range(nb // 2):
        cols = slice(jp * two, (jp + 1) * two)
        kx_ref[jp, :, 0:pair] = k_ref[0, cols, :]
        for h in range(2):
            vt_ref[jp, h, 0:HEAD_DIM, :] = v_ref[0, h * HEAD_DIM:(h + 1) * HEAD_DIM, cols]
    km = jnp.mean(k_ref[0].astype(F32).reshape(nb, blk, pair), axis=1)

    q2 = q_ref[0]
    kdim = lax.broadcasted_iota(jnp.int32, km.shape, 1)
    nidx = lax.broadcasted_iota(jnp.int32, (nb, seq), 0)
    qblk = lax.shift_right_logical(lax.broadcasted_iota(jnp.int32, (nb, seq), 1), blk.bit_length() - 1)
    for h in range(2):
        dims = slice(h * HEAD_DIM, (h + 1) * HEAD_DIM)
        km_h = jnp.where((kdim < HEAD_DIM) if h == 0 else (kdim >= HEAD_DIM), km, 0.0).astype(BF16)
        g = jnp.dot(km_h, q2, preferred_element_type=F32)
        g = jnp.where(nidx < qblk, g, NEG_INF)
        picked = jnp.zeros((nb, seq), jnp.int32)
        for _ in range(min(MOBA_TOPK, nb - 1)):
            top = jnp.max(g, axis=0, keepdims=True)
            first = jnp.min(jnp.where(g == top, nidx, nb), axis=0, keepdims=True)
            hit = nidx == first
            picked = jnp.where(hit, 1, picked)
            g = jnp.where(hit, TAKEN, g)
        allow = ((picked == 1) & (nidx < qblk)) | (nidx == qblk)
        bias = jnp.where(allow, 0.0, NEG_INF)
        bias = jnp.concatenate([bias, jnp.zeros((pair - nb, seq), F32)], axis=0).astype(BF16)
        for i in range(nb):
            qx_ref[i, dims, h * blk:(h + 1) * blk] = q_ref[0, dims, i * blk:(i + 1) * blk]
            qx_ref[i, pair:, h * blk:(h + 1) * blk] = bias[:, i * blk:(i + 1) * blk]

    m_ref[...] = jnp.full(m_ref.shape, NEG_INF, F32)
    acc_ref[...] = jnp.zeros_like(acc_ref)

    def step(k, carry, own_block, stages=(1, 2, 3)):
        for e in range(MOBA_ITEMS):
            c = MOBA_ITEMS * k + e
            for h in range(2):
                r = 2 * e + h
                if 3 in stages:
                    a3 = 2 * tab_ref[4, c] + h
                    pv = jnp.dot(vt_ref[tab_ref[5, c], h], p_ring[r], preferred_element_type=F32)
                    acc_ref[a3] = al_ring[r] * acc_ref[a3] + pv
                if 2 in stages:
                    a2 = 2 * tab_ref[3, c] + h
                    m_old = m_ref[a2]
                    m_new = jnp.maximum(m_old, cm_ring[r])
                    m_ref[a2] = m_new
                    al_ring[r] = jnp.exp2(m_old - m_new)
                    p_ring[r] = jnp.exp2(s_ring[r] - m_new).astype(BF16)
            if 1 in stages:
                s2 = jnp.dot(kx_ref[tab_ref[1, c]], qx_ref[tab_ref[0, c]], preferred_element_type=F32)
                for h in range(2):
                    s = s2[:, h * blk:(h + 1) * blk]
                    if own_block:
                        s = s + cb_ref[tab_ref[2, c]]
                    s_ring[2 * e + h] = s
                    cm_ring[2 * e + h] = jnp.max(s, axis=0, keepdims=True)
        return carry

    step(0, 0, own_block=False, stages=(1,))
    step(1, 0, own_block=False, stages=(1, 2))
    lax.fori_loop(2, n_past, functools.partial(step, own_block=False), 0)
    lax.fori_loop(n_past, n_groups, functools.partial(step, own_block=True), 0)
    step(n_groups, 0, own_block=False, stages=(2, 3))
    step(n_groups + 1, 0, own_block=False, stages=(3,))

    for i in range(nb):
        o = jnp.concatenate([acc_ref[2 * i + h, 0:HEAD_DIM, :] / acc_ref[2 * i + h, HEAD_DIM:HEAD_DIM + 1, :]
                             for h in range(2)], axis=0)
        o_ref[0, i * blk:(i + 1) * blk, :] = o.T.astype(o_ref.dtype)


def _moba(qt, k, vt):
    bsz, seq, width = k.shape
    blk = MOBA_BLOCK
    nb = seq // blk
    pair = 2 * HEAD_DIM
    tab, n_past, n_groups = _moba_tables(nb)
    spec = pl.BlockSpec((1, seq, pair), lambda b, hp, tab_ref: (b, 0, hp))
    spec_t = pl.BlockSpec((1, pair, seq), lambda b, hp, tab_ref: (b, hp, 0))
    n_state = 2 * nb
    n_ring = 2 * MOBA_ITEMS
    vrows = HEAD_DIM + BF16_SUBLANES
    return pl.pallas_call(
        functools.partial(_moba_body, nb=nb, n_past=n_past, n_groups=n_groups),
        grid_spec=pltpu.PrefetchScalarGridSpec(
            num_scalar_prefetch=1,
            grid=(bsz, width // pair),
            in_specs=[spec_t, spec, spec_t],
            out_specs=spec,
            scratch_shapes=[
                pltpu.VMEM((nb // 2, 2 * blk, 2 * pair), BF16),
                pltpu.VMEM((nb // 2, 2, vrows, 2 * blk), BF16),
                pltpu.VMEM((2, 2 * blk, blk), F32),
                pltpu.VMEM((nb, 2 * pair, 2 * blk), BF16),
                pltpu.VMEM((n_state, 1, blk), F32),
                pltpu.VMEM((n_state, vrows, blk), F32),
                pltpu.VMEM((n_ring, 2 * blk, blk), F32),
                pltpu.VMEM((n_ring, 1, blk), F32),
                pltpu.VMEM((n_ring, 2 * blk, blk), BF16),
                pltpu.VMEM((n_ring, 1, blk), F32),
            ]),
        out_shape=jax.ShapeDtypeStruct((bsz, seq, width), BF16),
        compiler_params=_params(("arbitrary", "arbitrary")),
        name="moba",
    )(jnp.asarray(tab), qt, k, vt)


def _post_body(x_ref, ya_ref, yb_ref, ga_ref, gb_ref, wa_ref, wb_ref, wo_ref, gpm_ref, gpf_ref,
               w1_ref, w2_ref, gpo_ref, o_ref):
    a = jnp.dot(ya_ref[...], wa_ref[...], preferred_element_type=F32)
    b = jnp.dot(yb_ref[...], wb_ref[...], preferred_element_type=F32)
    merged = (jax.nn.sigmoid(ga_ref[...].astype(F32)) * a
              + jax.nn.sigmoid(gb_ref[...].astype(F32)) * b)
    mix = jnp.dot(merged.astype(BF16), wo_ref[...], preferred_element_type=F32)
    h = x_ref[...] + _rms(mix, gpm_ref[...])
    f = _rms(h, gpf_ref[...]).astype(BF16)
    d_ff = w1_ref.shape[1]
    ck = min(FFN_CHUNK, d_ff)
    acc = jnp.zeros(h.shape, F32)
    for c in range(d_ff // ck):
        t = jnp.dot(f, w1_ref[:, c * ck:(c + 1) * ck], preferred_element_type=F32)
        t = jnp.square(jnp.maximum(t, 0.0)).astype(BF16)
        acc = acc + jnp.dot(t, w2_ref[c * ck:(c + 1) * ck, :], preferred_element_type=F32)
    o_ref[...] = h + _rms(acc, gpo_ref[...])


def _post(x2, ya, yb, ga, gb, wa, wb, wo, gpm, gpf, w1, w2, gpo):
    t, d = x2.shape
    tm = min(POST_ROWS, t)

    def rows(a):
        return pl.BlockSpec((tm, a.shape[1]), lambda i: (i, 0))

    consts = (wa, wb, wo, gpm, gpf, w1, w2, gpo)
    return pl.pallas_call(
        _post_body,
        grid=(t // tm,),
        in_specs=[rows(a) for a in (x2, ya, yb, ga, gb)] + [_resident(c.shape) for c in consts],
        out_specs=rows(x2),
        out_shape=jax.ShapeDtypeStruct((t, d), x2.dtype),
        compiler_params=_params(("parallel",)),
        name="post",
    )(x2, ya, yb, ga, gb, *consts)


def _layer(h, g_pre_mix, w_in, lam_re, lam_im, log_dt, b_re, b_im, c_re, c_im, d_skip, w_glu, b_glu,
           w_branch_a, w_branch_b, w_out, g_post_mix, g_pre_ffn, w_ff1, w_ff2, g_post_ffn):
    bsz, seq, d = h.shape
    ssm_w = lam_re.shape[0] * SSM_GROUP
    attn_w = ATTN_HEADS * HEAD_DIM
    x2 = h.reshape(bsz * seq, d)

    zs, qt, k, vt, ga, gb = _inproj(h, g_pre_mix.reshape(1, d), w_in.astype(BF16),
                                    (ssm_w, attn_w, attn_w, attn_w, d, d),
                                    ("lbc", "bcl", "blc", "bcl", "blc", "blc"),
                                    (1.0, MOBA_Q_SCALE, 1.0, 1.0, 1.0, 1.0))

    ar, ai, bbr, bbi = _s5_prep(lam_re, lam_im, log_dt, b_re, b_im)
    g, p = lam_re.shape
    hg = g // 2
    n_half = hg * p

    def halves(m):
        return jnp.stack([_block_diag(m[:hg]), _block_diag(m[hg:])])

    bmat = jnp.concatenate([halves(bbr), halves(bbi)], axis=2).astype(BF16)
    cre = halves(jnp.transpose(c_re, (0, 2, 1))).astype(BF16)
    cim = halves(jnp.transpose(c_im, (0, 2, 1))).astype(BF16)
    ar8 = jnp.broadcast_to(ar.reshape(1, 2 * n_half), (bsz, 2 * n_half))
    ai8 = jnp.broadcast_to(ai.reshape(1, 2 * n_half), (bsz, 2 * n_half))

    ya = _s5_mixer(zs.reshape(seq * bsz, ssm_w), bsz, bmat, ar8, ai8, cre, cim, d_skip.reshape(1, ssm_w),
                   w_glu.astype(BF16), b_glu.reshape(1, ssm_w)).reshape(bsz * seq, ssm_w)
    yb = _moba(qt, k, vt).reshape(bsz * seq, attn_w)
    ga, gb = ga.reshape(bsz * seq, d), gb.reshape(bsz * seq, d)

    out = _post(x2, ya, yb, ga, gb, w_branch_a.astype(BF16), w_branch_b.astype(BF16),
                w_out.astype(BF16), g_post_mix.reshape(1, d), g_pre_ffn.reshape(1, d),
                w_ff1.astype(BF16), w_ff2.astype(BF16), g_post_ffn.reshape(1, d))
    return out.reshape(bsz, seq, d)


def kernel(x, g_pre_mix, w_in, lam_re, lam_im, log_dt, b_re, b_im, c_re, c_im, d_skip, w_glu, b_glu,
           w_branch_a, w_branch_b, w_out, g_post_mix, g_pre_ffn, w_ff1, w_ff2, g_post_ffn):
    per_layer = (g_pre_mix, w_in, lam_re, lam_im, log_dt, b_re, b_im, c_re, c_im, d_skip, w_glu, b_glu,
                 w_branch_a, w_branch_b, w_out, g_post_mix, g_pre_ffn, w_ff1, w_ff2, g_post_ffn)
    h = x
    for l in range(g_pre_mix.shape[0]):
        h = _layer(h, *(p[l] for p in per_layer))
    return h
```

```python
import functools

import jax
import jax.numpy as jnp
import numpy as np
from jax import lax
from jax.experimental import pallas as pl
from jax.experimental.pallas import tpu as pltpu

F32 = jnp.float32
BF16 = jnp.bfloat16

RMS_EPS = 1e-6
NEG_INF = -1e30
TAKEN = -3e38

SSM_GROUP = 16
ATTN_HEADS = 8
HEAD_DIM = 64
MOBA_BLOCK = 256
MOBA_TOPK = 3
MOBA_ITEMS = 4

LOG2_E = 1.4426950408889634
MOBA_Q_SCALE = LOG2_E * HEAD_DIM ** -0.5

SUBLANES = 8
BF16_SUBLANES = 16
VMEM_LIMIT_BYTES = 56 * 1024 * 1024

INPROJ_ROWS = 1024
POST_ROWS = 512
S5_TIME_CHUNK = 64
S5_COL_GROUP = 512
FFN_CHUNK = 1024


def _params(sem):
    return pltpu.CompilerParams(dimension_semantics=sem, vmem_limit_bytes=VMEM_LIMIT_BYTES)


def _resident(shape):
    zeros = (0,) * len(shape)
    return pl.BlockSpec(shape, lambda *_: zeros, pipeline_mode=pl.Buffered(1))


def _rms(x, g):
    var = jnp.mean(x * x, axis=-1, keepdims=True)
    return (x * lax.rsqrt(var + RMS_EPS)) * g


def _inproj_body(x_ref, g_ref, w_ref, *out_refs, layouts, scales):
    bsz, tl, d = x_ref.shape
    ub = _rms(x_ref[...].reshape(bsz * tl, d), g_ref[...]).astype(BF16)
    col = 0
    for ref, layout, scale in zip(out_refs, layouts, scales):
        n = ref.shape[1] if layout == "bcl" else ref.shape[2]
        z = jnp.dot(ub, w_ref[:, col:col + n], preferred_element_type=F32).reshape(bsz, tl, n)
        if scale != 1.0:
            z = z * scale
        if layout == "lbc":
            z = jnp.swapaxes(z.astype(ref.dtype), 0, 1)
        elif layout == "bcl":
            z = jnp.swapaxes(z, 1, 2)
        ref[...] = z.astype(ref.dtype)
        col += n


def _inproj(x, g, w, widths, layouts, scales):
    bsz, seq, d = x.shape
    tl = min(INPROJ_ROWS // bsz, seq)
    specs = {"blc": lambda n: (pl.BlockSpec((bsz, tl, n), lambda i: (0, i, 0)), (bsz, seq, n)),
             "lbc": lambda n: (pl.BlockSpec((tl, bsz, n), lambda i: (i, 0, 0)), (seq, bsz, n)),
             "bcl": lambda n: (pl.BlockSpec((bsz, n, tl), lambda i: (0, 0, i)), (bsz, n, seq))}
    outs = [specs[layout](n) for n, layout in zip(widths, layouts)]
    return pl.pallas_call(
        functools.partial(_inproj_body, layouts=layouts, scales=scales),
        grid=(seq // tl,),
        in_specs=[pl.BlockSpec((bsz, tl, d), lambda i: (0, i, 0)), _resident(g.shape), _resident(w.shape)],
        out_specs=[spec for spec, _ in outs],
        out_shape=[jax.ShapeDtypeStruct(shape, BF16) for _, shape in outs],
        compiler_params=_params(("parallel",)),
        name="inproj",
    )(x, g, w)


def _s5_prep_body(lr_ref, li_ref, ldt_ref, br_ref, bi_ref, ar_ref, ai_ref, bbr_ref, bbi_ref):
    lr = lr_ref[...]
    li = li_ref[...]
    dt = jnp.exp(ldt_ref[...])
    mag = jnp.exp(lr * dt)
    ab_re = mag * jnp.cos(li * dt)
    ab_im = mag * jnp.sin(li * dt)
    nr = ab_re - 1.0
    ni = ab_im
    den = lr * lr + li * li
    coef_re = (nr * lr + ni * li) / den
    coef_im = (ni * lr - nr * li) / den
    br = br_ref[...]
    bi = bi_ref[...]
    ar_ref[...] = ab_re
    ai_ref[...] = ab_im
    bbr_ref[...] = coef_re * br - coef_im * bi
    bbi_ref[...] = coef_re * bi + coef_im * br


def _s5_prep(lam_re, lam_im, log_dt, b_re, b_im):
    g, p = lam_re.shape
    h = b_re.shape[-1]
    lr = lam_re.reshape(g, 1, p)
    li = lam_im.reshape(g, 1, p)
    ldt = jnp.broadcast_to(log_dt.reshape(g, 1, 1), (g, 1, p))
    br = jnp.transpose(b_re, (0, 2, 1))
    bi = jnp.transpose(b_im, (0, 2, 1))
    small = jax.ShapeDtypeStruct((g, 1, p), F32)
    big = jax.ShapeDtypeStruct((g, h, p), F32)
    return pl.pallas_call(_s5_prep_body, out_shape=[small, small, big, big], name="s5_prep")(lr, li, ldt, br, bi)


def _block_diag(m):
    g, r, c = m.shape
    on_diag = jnp.eye(g, dtype=bool)[:, None, :, None]
    return jnp.where(on_diag, m[:, :, None, :], jnp.zeros((), m.dtype)).reshape(g * r, g * c)


def _s5_body(u_ref, b_ref, ar_ref, ai_ref, cre_ref, cim_ref, dsk_ref, wg_ref, bg_ref, y_ref,
             bu_ref, st_ref, *, steps):
    half_in = u_ref.shape[1] // 2
    half_st = st_ref.shape[1] // 4

    @pl.when(pl.program_id(0) == 0)
    def _():
        st_ref[...] = jnp.zeros_like(st_ref)

    for c in range(2):
        bu_ref[:, 2 * c * half_st:2 * (c + 1) * half_st] = jnp.dot(
            u_ref[:, c * half_in:(c + 1) * half_in], b_ref[c], preferred_element_type=F32)

    w = S5_COL_GROUP
    for c in range(2):
        for s in range(half_st // w):
            re0 = 2 * c * half_st + s * w
            im0 = re0 + half_st
            a0 = c * half_st + s * w
            ar = ar_ref[:, a0:a0 + w]
            ai = ai_ref[:, a0:a0 + w]
            xr = st_ref[:, re0:re0 + w]
            xi = st_ref[:, im0:im0 + w]
            for t in range(steps):
                rows = slice(t * SUBLANES, (t + 1) * SUBLANES)
                xr, xi = (ar * xr - ai * xi + bu_ref[rows, re0:re0 + w],
                          ar * xi + ai * xr + bu_ref[rows, im0:im0 + w])
                bu_ref[rows, re0:re0 + w] = xr
                bu_ref[rows, im0:im0 + w] = xi
            st_ref[:, re0:re0 + w] = xr
            st_ref[:, im0:im0 + w] = xi

    ys = []
    for c in range(2):
        xre = bu_ref[:, 2 * c * half_st:(2 * c + 1) * half_st].astype(BF16)
        xim = bu_ref[:, (2 * c + 1) * half_st:(2 * c + 2) * half_st].astype(BF16)
        ys.append(jnp.dot(xre, cre_ref[c], preferred_element_type=F32)
                  - jnp.dot(xim, cim_ref[c], preferred_element_type=F32))
    y = jnp.concatenate(ys, axis=1) + dsk_ref[...] * u_ref[...].astype(F32)
    v = jax.nn.gelu(y)
    gate = jnp.dot(v.astype(BF16), wg_ref[...], preferred_element_type=F32) + bg_ref[...]
    out = (v * jax.nn.sigmoid(gate)).astype(y_ref.dtype)
    bsz = y_ref.shape[0]
    y_ref[...] = jnp.swapaxes(out.reshape(steps, bsz, out.shape[1]), 0, 1)


def _s5_mixer(u_tm, bsz, bmat, ar, ai, cre, cim, d_skip, w_glu, b_glu):
    rows, width = u_tm.shape
    seq = rows // bsz
    steps = min(S5_TIME_CHUNK, seq)
    r = steps * bsz
    n_state = ar.shape[1] * 2
    return pl.pallas_call(
        functools.partial(_s5_body, steps=steps),
        grid=(seq // steps,),
        in_specs=[pl.BlockSpec((r, width), lambda i: (i, 0)),
                  _resident(bmat.shape), _resident(ar.shape), _resident(ai.shape),
                  _resident(cre.shape), _resident(cim.shape), _resident(d_skip.shape),
                  _resident(w_glu.shape), _resident(b_glu.shape)],
        out_specs=pl.BlockSpec((bsz, steps, width), lambda i: (0, i, 0)),
        out_shape=jax.ShapeDtypeStruct((bsz, seq, width), BF16),
        scratch_shapes=[pltpu.VMEM((r, n_state), F32), pltpu.VMEM((bsz, n_state), F32)],
        compiler_params=_params(("arbitrary",)),
        name="s5_mixer",
    )(u_tm, bmat, ar, ai, cre, cim, d_skip, w_glu, b_glu)


def _moba_tables(nb):
    per = MOBA_ITEMS
    items = [(i, jp, 0) for i in range(nb) for jp in range(i // 2)]
    n_past, rest = divmod(len(items), per)
    items += [(i, i // 2, 0 if i % 2 else 1) for i in range(nb)]
    n_groups, rest2 = divmod(len(items), per)
    assert nb % 2 == 0 and not rest and not rest2 and 2 <= n_past < n_groups
    tab = np.zeros((6, per * (n_groups + 2)), np.int32)
    for c in range(tab.shape[1]):
        if c < len(items):
            tab[0:3, c] = items[c]
        if 0 <= c - per < len(items):
            tab[3, c] = items[c - per][0]
        if 0 <= c - 2 * per < len(items):
            tab[4:6, c] = items[c - 2 * per][0:2]
    return tab, n_past, n_groups


def _moba_body(tab_ref, q_ref, k_ref, v_ref, o_ref, kx_ref, vt_ref, cb_ref, qx_ref,
               m_ref, acc_ref, s_ring, cm_ring, p_ring, al_ring, *, nb, tab, n_past, n_groups):
    blk = MOBA_BLOCK
    pair = 2 * HEAD_DIM
    seq = nb * blk
    two = 2 * blk
    vrows = vt_ref.shape[2]

    @pl.when((pl.program_id(0) == 0) & (pl.program_id(1) == 0))
    def _():
        row = lax.broadcasted_iota(jnp.int32, (two, pair), 0)
        lane = lax.broadcasted_iota(jnp.int32, (two, pair), 1)
        ones_row = lax.broadcasted_iota(jnp.int32, (vrows - HEAD_DIM, two), 0) == 0
        for jp in range(nb // 2):
            block = jnp.where(row < blk, 2 * jp, 2 * jp + 1)
            kx_ref[jp, :, pair:] = jnp.where(lane == block, 1.0, 0.0).astype(BF16)
            for h in range(2):
                vt_ref[jp, h, HEAD_DIM:, :] = jnp.where(ones_row, 1.0, 0.0).astype(BF16)
        kpos = lax.broadcasted_iota(jnp.int32, (blk, blk), 0)
        qpos = lax.broadcasted_iota(jnp.int32, (blk, blk), 1)
        tri = jnp.where(kpos <= qpos, 0.0, NEG_INF)
        for i in range(nb):
            for h in range(2):
                other = slice((1 - h) * HEAD_DIM, (2 - h) * HEAD_DIM)
                qx_ref[i, other, h * blk:(h + 1) * blk] = jnp.zeros((HEAD_DIM, blk), BF16)
        cb_ref[0] = jnp.concatenate([jnp.zeros((blk, blk), F32), tri], axis=0)
        cb_ref[1] = jnp.concatenate([tri, jnp.full((blk, blk), NEG_INF, F32)], axis=0)

    for jp in range(nb // 2):
        cols = slice(jp * two, (jp + 1) * two)
        kx_ref[jp, :, 0:pair] = k_ref[0, cols, :]
        for h in range(2):
            vt_ref[jp, h, 0:HEAD_DIM, :] = v_ref[0, h * HEAD_DIM:(h + 1) * HEAD_DIM, cols]
    km = jnp.mean(k_ref[0].astype(F32).reshape(nb, blk, pair), axis=1)

    q2 = q_ref[0]
    kdim = lax.broadcasted_iota(jnp.int32, km.shape, 1)
    nidx = lax.broadcasted_iota(jnp.int32, (nb, seq), 0)
    qblk = lax.shift_right_logical(lax.broadcasted_iota(jnp.int32, (nb, seq), 1), blk.bit_length() - 1)
    for h in range(2):
        dims = slice(h * HEAD_DIM, (h + 1) * HEAD_DIM)
        km_h = jnp.where((kdim < HEAD_DIM) if h == 0 else (kdim >= HEAD_DIM), km, 0.0).astype(BF16)
        g = jnp.dot(km_h, q2, preferred_element_type=F32)
        g = jnp.where(nidx < qblk, g, NEG_INF)
        picked = jnp.zeros((nb, seq), jnp.int32)
        for _ in range(min(MOBA_TOPK, nb - 1)):
            top = jnp.max(g, axis=0, keepdims=True)
            first = jnp.min(jnp.where(g == top, nidx, nb), axis=0, keepdims=True)
            hit = nidx == first
            picked = jnp.where(hit, 1, picked)
            g = jnp.where(hit, TAKEN, g)
        allow = ((picked == 1) & (nidx < qblk)) | (nidx == qblk)
        bias = jnp.where(allow, 0.0, NEG_INF)
        bias = jnp.concatenate([bias, jnp.zeros((pair - nb, seq), F32)], axis=0).astype(BF16)
        for i in range(nb):
            qx_ref[i, dims, h * blk:(h + 1) * blk] = q_ref[0, dims, i * blk:(i + 1) * blk]
            qx_ref[i, pair:, h * blk:(h + 1) * blk] = bias[:, i * blk:(i + 1) * blk]

    m_ref[...] = jnp.full(m_ref.shape, NEG_INF, F32)
    acc_ref[...] = jnp.zeros_like(acc_ref)

    def entry(row, c):
        return int(tab[row, c]) if isinstance(c, int) else tab_ref[row, c]

    def step(k, carry, own_block, stages=(1, 2, 3)):
        for e in range(MOBA_ITEMS):
            c = MOBA_ITEMS * k + e
            for h in range(2):
                r = 2 * e + h
                if 3 in stages:
                    a3 = 2 * entry(4, c) + h
                    pv = jnp.dot(vt_ref[entry(5, c), h], p_ring[r], preferred_element_type=F32)
                    acc_ref[a3] = al_ring[r] * acc_ref[a3] + pv
                if 2 in stages:
                    a2 = 2 * entry(3, c) + h
                    m_old = m_ref[a2]
                    m_new = jnp.maximum(m_old, cm_ring[r])
                    m_ref[a2] = m_new
                    al_ring[r] = jnp.exp2(m_old - m_new)
                    p_ring[r] = jnp.exp2(s_ring[r] - m_new).astype(BF16)
            if 1 in stages:
                s2 = jnp.dot(kx_ref[entry(1, c)], qx_ref[entry(0, c)], preferred_element_type=F32)
                for h in range(2):
                    s = s2[:, h * blk:(h + 1) * blk]
                    if own_block:
                        s = s + cb_ref[entry(2, c)]
                    s_ring[2 * e + h] = s
                    cm_ring[2 * e + h] = jnp.max(s, axis=0, keepdims=True)
        return carry

    step(0, 0, own_block=False, stages=(1,))
    step(1, 0, own_block=False, stages=(1, 2))
    lax.fori_loop(2, n_past, functools.partial(step, own_block=False), 0)
    lax.fori_loop(n_past, n_groups, functools.partial(step, own_block=True), 0)
    step(n_groups, 0, own_block=False, stages=(2, 3))
    step(n_groups + 1, 0, own_block=False, stages=(3,))

    for i in range(nb):
        o = jnp.concatenate([acc_ref[2 * i + h, 0:HEAD_DIM, :] / acc_ref[2 * i + h, HEAD_DIM:HEAD_DIM + 1, :]
                             for h in range(2)], axis=0)
        o_ref[0, i * blk:(i + 1) * blk, :] = o.T.astype(o_ref.dtype)


def _moba(qt, k, vt):
    bsz, seq, width = k.shape
    blk = MOBA_BLOCK
    nb = seq // blk
    pair = 2 * HEAD_DIM
    tab, n_past, n_groups = _moba_tables(nb)
    spec = pl.BlockSpec((1, seq, pair), lambda b, hp, tab_ref: (b, 0, hp))
    spec_t = pl.BlockSpec((1, pair, seq), lambda b, hp, tab_ref: (b, hp, 0))
    n_state = 2 * nb
    n_ring = 2 * MOBA_ITEMS
    vrows = HEAD_DIM + BF16_SUBLANES
    return pl.pallas_call(
        functools.partial(_moba_body, nb=nb, tab=tab, n_past=n_past, n_groups=n_groups),
        grid_spec=pltpu.PrefetchScalarGridSpec(
            num_scalar_prefetch=1,
            grid=(bsz, width // pair),
            in_specs=[spec_t, spec, spec_t],
            out_specs=spec,
            scratch_shapes=[
                pltpu.VMEM((nb // 2, 2 * blk, 2 * pair), BF16),
                pltpu.VMEM((nb // 2, 2, vrows, 2 * blk), BF16),
                pltpu.VMEM((2, 2 * blk, blk), F32),
                pltpu.VMEM((nb, 2 * pair, 2 * blk), BF16),
                pltpu.VMEM((n_state, 1, blk), F32),
                pltpu.VMEM((n_state, vrows, blk), F32),
                pltpu.VMEM((n_ring, 2 * blk, blk), F32),
                pltpu.VMEM((n_ring, 1, blk), F32),
                pltpu.VMEM((n_ring, 2 * blk, blk), BF16),
                pltpu.VMEM((n_ring, 1, blk), F32),
            ]),
        out_shape=jax.ShapeDtypeStruct((bsz, seq, width), BF16),
        compiler_params=_params(("arbitrary", "arbitrary")),
        name="moba",
    )(jnp.asarray(tab), qt, k, vt)


def _post_body(x_ref, ya_ref, yb_ref, ga_ref, gb_ref, wa_ref, wb_ref, wo_ref, gpm_ref, gpf_ref,
               w1_ref, w2_ref, gpo_ref, o_ref):
    a = jnp.dot(ya_ref[...], wa_ref[...], preferred_element_type=F32)
    b = jnp.dot(yb_ref[...], wb_ref[...], preferred_element_type=F32)
    merged = (jax.nn.sigmoid(ga_ref[...].astype(F32)) * a
              + jax.nn.sigmoid(gb_ref[...].astype(F32)) * b)
    mix = jnp.dot(merged.astype(BF16), wo_ref[...], preferred_element_type=F32)
    h = x_ref[...] + _rms(mix, gpm_ref[...])
    f = _rms(h, gpf_ref[...]).astype(BF16)
    d_ff = w1_ref.shape[1]
    ck = min(FFN_CHUNK, d_ff)
    acc = jnp.zeros(h.shape, F32)
    for c in range(d_ff // ck):
        t = jnp.dot(f, w1_ref[:, c * ck:(c + 1) * ck], preferred_element_type=F32)
        t = jnp.square(jnp.maximum(t, 0.0)).astype(BF16)
        acc = acc + jnp.dot(t, w2_ref[c * ck:(c + 1) * ck, :], preferred_element_type=F32)
    o_ref[...] = h + _rms(acc, gpo_ref[...])


def _post(x2, ya, yb, ga, gb, wa, wb, wo, gpm, gpf, w1, w2, gpo):
    t, d = x2.shape
    tm = min(POST_ROWS, t)

    def rows(a):
        return pl.BlockSpec((tm, a.shape[1]), lambda i: (i, 0))

    consts = (wa, wb, wo, gpm, gpf, w1, w2, gpo)
    return pl.pallas_call(
        _post_body,
        grid=(t // tm,),
        in_specs=[rows(a) for a in (x2, ya, yb, ga, gb)] + [_resident(c.shape) for c in consts],
        out_specs=rows(x2),
        out_shape=jax.ShapeDtypeStruct((t, d), x2.dtype),
        compiler_params=_params(("parallel",)),
        name="post",
    )(x2, ya, yb, ga, gb, *consts)


def _layer(h, g_pre_mix, w_in, lam_re, lam_im, log_dt, b_re, b_im, c_re, c_im, d_skip, w_glu, b_glu,
           w_branch_a, w_branch_b, w_out, g_post_mix, g_pre_ffn, w_ff1, w_ff2, g_post_ffn):
    bsz, seq, d = h.shape
    ssm_w = lam_re.shape[0] * SSM_GROUP
    attn_w = ATTN_HEADS * HEAD_DIM
    x2 = h.reshape(bsz * seq, d)

    zs, qt, k, vt, ga, gb = _inproj(h, g_pre_mix.reshape(1, d), w_in.astype(BF16),
                                    (ssm_w, attn_w, attn_w, attn_w, d, d),
                                    ("lbc", "bcl", "blc", "bcl", "blc", "blc"),
                                    (1.0, MOBA_Q_SCALE, 1.0, 1.0, 1.0, 1.0))

    ar, ai, bbr, bbi = _s5_prep(lam_re, lam_im, log_dt, b_re, b_im)
    g, p = lam_re.shape
    hg = g // 2
    n_half = hg * p

    def halves(m):
        return jnp.stack([_block_diag(m[:hg]), _block_diag(m[hg:])])

    bmat = jnp.concatenate([halves(bbr), halves(bbi)], axis=2).astype(BF16)
    cre = halves(jnp.transpose(c_re, (0, 2, 1))).astype(BF16)
    cim = halves(jnp.transpose(c_im, (0, 2, 1))).astype(BF16)
    ar8 = jnp.broadcast_to(ar.reshape(1, 2 * n_half), (bsz, 2 * n_half))
    ai8 = jnp.broadcast_to(ai.reshape(1, 2 * n_half), (bsz, 2 * n_half))

    ya = _s5_mixer(zs.reshape(seq * bsz, ssm_w), bsz, bmat, ar8, ai8, cre, cim, d_skip.reshape(1, ssm_w),
                   w_glu.astype(BF16), b_glu.reshape(1, ssm_w)).reshape(bsz * seq, ssm_w)
    yb = _moba(qt, k, vt).reshape(bsz * seq, attn_w)
    ga, gb = ga.reshape(bsz * seq, d), gb.reshape(bsz * seq, d)

    out = _post(x2, ya, yb, ga, gb, w_branch_a.astype(BF16), w_branch_b.astype(BF16),
                w_out.astype(BF16), g_post_mix.reshape(1, d), g_pre_ffn.reshape(1, d),
                w_ff1.astype(BF16), w_ff2.astype(BF16), g_post_ffn.reshape(1, d))
    return out.reshape(bsz, seq, d)


def kernel(x, g_pre_mix, w_in, lam_re, lam_im, log_dt, b_re, b_im, c_re, c_im, d_skip, w_glu, b_glu,
           w_branch_a, w_branch_b, w_out, g_post_mix, g_pre_ffn, w_ff1, w_ff2, g_post_ffn):
    per_layer = (g_pre_mix, w_in, lam_re, lam_im, log_dt, b_re, b_im, c_re, c_im, d_skip, w_glu, b_glu,
                 w_branch_a, w_branch_b, w_out, g_post_mix, g_pre_ffn, w_ff1, w_ff2, g_post_ffn)
    h = x
    for l in range(g_pre_mix.shape[0]):
        h = _layer(h, *(p[l] for p in per_layer))
    return h
```

```python
import functools

import jax
import jax.numpy as jnp
import numpy as np
from jax import lax
from jax.experimental import pallas as pl
from jax.experimental.pallas import tpu as pltpu

F32 = jnp.float32
BF16 = jnp.bfloat16

RMS_EPS = 1e-6
NEG_INF = -1e30
TAKEN = -3e38

SSM_GROUP = 16
ATTN_HEADS = 8
HEAD_DIM = 64
MOBA_BLOCK = 256
MOBA_TOPK = 3
MOBA_ITEMS = 2
MOBA_UNROLL = 3

LOG2_E = 1.4426950408889634
MOBA_Q_SCALE = LOG2_E * HEAD_DIM ** -0.5

SUBLANES = 8
BF16_SUBLANES = 16
VMEM_LIMIT_BYTES = 56 * 1024 * 1024

INPROJ_ROWS = 1024
POST_ROWS = 512
S5_TIME_CHUNK = 64
S5_COL_GROUP = 512
FFN_CHUNK = 1024


def _params(sem):
    return pltpu.CompilerParams(dimension_semantics=sem, vmem_limit_bytes=VMEM_LIMIT_BYTES)


def _resident(shape):
    zeros = (0,) * len(shape)
    return pl.BlockSpec(shape, lambda *_: zeros, pipeline_mode=pl.Buffered(1))


def _rms(x, g):
    var = jnp.mean(x * x, axis=-1, keepdims=True)
    return (x * lax.rsqrt(var + RMS_EPS)) * g


def _inproj_body(x_ref, g_ref, w_ref, *out_refs, layouts, scales):
    bsz, tl, d = x_ref.shape
    ub = _rms(x_ref[...].reshape(bsz * tl, d), g_ref[...]).astype(BF16)
    col = 0
    for ref, layout, scale in zip(out_refs, layouts, scales):
        n = ref.shape[1] if layout == "bcl" else ref.shape[2]
        z = jnp.dot(ub, w_ref[:, col:col + n], preferred_element_type=F32).reshape(bsz, tl, n)
        if scale != 1.0:
            z = z * scale
        if layout == "lbc":
            z = jnp.swapaxes(z.astype(ref.dtype), 0, 1)
        elif layout == "bcl":
            z = jnp.swapaxes(z, 1, 2)
        ref[...] = z.astype(ref.dtype)
        col += n


def _inproj(x, g, w, widths, layouts, scales):
    bsz, seq, d = x.shape
    tl = min(INPROJ_ROWS // bsz, seq)
    specs = {"blc": lambda n: (pl.BlockSpec((bsz, tl, n), lambda i: (0, i, 0)), (bsz, seq, n)),
             "lbc": lambda n: (pl.BlockSpec((tl, bsz, n), lambda i: (i, 0, 0)), (seq, bsz, n)),
             "bcl": lambda n: (pl.BlockSpec((bsz, n, tl), lambda i: (0, 0, i)), (bsz, n, seq))}
    outs = [specs[layout](n) for n, layout in zip(widths, layouts)]
    return pl.pallas_call(
        functools.partial(_inproj_body, layouts=layouts, scales=scales),
        grid=(seq // tl,),
        in_specs=[pl.BlockSpec((bsz, tl, d), lambda i: (0, i, 0)), _resident(g.shape), _resident(w.shape)],
        out_specs=[spec for spec, _ in outs],
        out_shape=[jax.ShapeDtypeStruct(shape, BF16) for _, shape in outs],
        compiler_params=_params(("parallel",)),
        name="inproj",
    )(x, g, w)


def _s5_prep_body(lr_ref, li_ref, ldt_ref, br_ref, bi_ref, ar_ref, ai_ref, bbr_ref, bbi_ref):
    lr = lr_ref[...]
    li = li_ref[...]
    dt = jnp.exp(ldt_ref[...])
    mag = jnp.exp(lr * dt)
    ab_re = mag * jnp.cos(li * dt)
    ab_im = mag * jnp.sin(li * dt)
    nr = ab_re - 1.0
    ni = ab_im
    den = lr * lr + li * li
    coef_re = (nr * lr + ni * li) / den
    coef_im = (ni * lr - nr * li) / den
    br = br_ref[...]
    bi = bi_ref[...]
    ar_ref[...] = ab_re
    ai_ref[...] = ab_im
    bbr_ref[...] = coef_re * br - coef_im * bi
    bbi_ref[...] = coef_re * bi + coef_im * br


def _s5_prep(lam_re, lam_im, log_dt, b_re, b_im):
    g, p = lam_re.shape
    h = b_re.shape[-1]
    lr = lam_re.reshape(g, 1, p)
    li = lam_im.reshape(g, 1, p)
    ldt = jnp.broadcast_to(log_dt.reshape(g, 1, 1), (g, 1, p))
    br = jnp.transpose(b_re, (0, 2, 1))
    bi = jnp.transpose(b_im, (0, 2, 1))
    small = jax.ShapeDtypeStruct((g, 1, p), F32)
    big = jax.ShapeDtypeStruct((g, h, p), F32)
    return pl.pallas_call(_s5_prep_body, out_shape=[small, small, big, big], name="s5_prep")(lr, li, ldt, br, bi)


def _block_diag(m):
    g, r, c = m.shape
    on_diag = jnp.eye(g, dtype=bool)[:, None, :, None]
    return jnp.where(on_diag, m[:, :, None, :], jnp.zeros((), m.dtype)).reshape(g * r, g * c)


def _s5_body(u_ref, b_ref, ar_ref, ai_ref, cre_ref, cim_ref, dsk_ref, wg_ref, bg_ref, y_ref,
             bu_ref, st_ref, *, steps):
    half_in = u_ref.shape[1] // 2
    half_st = st_ref.shape[1] // 4

    @pl.when(pl.program_id(0) == 0)
    def _():
        st_ref[...] = jnp.zeros_like(st_ref)

    for c in range(2):
        bu_ref[:, 2 * c * half_st:2 * (c + 1) * half_st] = jnp.dot(
            u_ref[:, c * half_in:(c + 1) * half_in], b_ref[c], preferred_element_type=F32)

    w = S5_COL_GROUP
    for c in range(2):
        for s in range(half_st // w):
            re0 = 2 * c * half_st + s * w
            im0 = re0 + half_st
            a0 = c * half_st + s * w
            ar = ar_ref[:, a0:a0 + w]
            ai = ai_ref[:, a0:a0 + w]
            xr = st_ref[:, re0:re0 + w]
            xi = st_ref[:, im0:im0 + w]
            for t in range(steps):
                rows = slice(t * SUBLANES, (t + 1) * SUBLANES)
                xr, xi = (ar * xr - ai * xi + bu_ref[rows, re0:re0 + w],
                          ar * xi + ai * xr + bu_ref[rows, im0:im0 + w])
                bu_ref[rows, re0:re0 + w] = xr
                bu_ref[rows, im0:im0 + w] = xi
            st_ref[:, re0:re0 + w] = xr
            st_ref[:, im0:im0 + w] = xi

    ys = []
    for c in range(2):
        xre = bu_ref[:, 2 * c * half_st:(2 * c + 1) * half_st].astype(BF16)
        xim = bu_ref[:, (2 * c + 1) * half_st:(2 * c + 2) * half_st].astype(BF16)
        ys.append(jnp.dot(xre, cre_ref[c], preferred_element_type=F32)
                  - jnp.dot(xim, cim_ref[c], preferred_element_type=F32))
    y = jnp.concatenate(ys, axis=1) + dsk_ref[...] * u_ref[...].astype(F32)
    v = jax.nn.gelu(y)
    gate = jnp.dot(v.astype(BF16), wg_ref[...], preferred_element_type=F32) + bg_ref[...]
    out = (v * jax.nn.sigmoid(gate)).astype(y_ref.dtype)
    bsz = y_ref.shape[0]
    y_ref[...] = jnp.swapaxes(out.reshape(steps, bsz, out.shape[1]), 0, 1)


def _s5_mixer(u_tm, bsz, bmat, ar, ai, cre, cim, d_skip, w_glu, b_glu):
    rows, width = u_tm.shape
    seq = rows // bsz
    steps = min(S5_TIME_CHUNK, seq)
    r = steps * bsz
    n_state = ar.shape[1] * 2
    return pl.pallas_call(
        functools.partial(_s5_body, steps=steps),
        grid=(seq // steps,),
        in_specs=[pl.BlockSpec((r, width), lambda i: (i, 0)),
                  _resident(bmat.shape), _resident(ar.shape), _resident(ai.shape),
                  _resident(cre.shape), _resident(cim.shape), _resident(d_skip.shape),
                  _resident(w_glu.shape), _resident(b_glu.shape)],
        out_specs=pl.BlockSpec((bsz, steps, width), lambda i: (0, i, 0)),
        out_shape=jax.ShapeDtypeStruct((bsz, seq, width), BF16),
        scratch_shapes=[pltpu.VMEM((r, n_state), F32), pltpu.VMEM((bsz, n_state), F32)],
        compiler_params=_params(("arbitrary",)),
        name="s5_mixer",
    )(u_tm, bmat, ar, ai, cre, cim, d_skip, w_glu, b_glu)


def _moba_tables(nb):
    per = MOBA_ITEMS
    items = [(i, jp, 0) for i in range(nb) for jp in range(i // 2)]
    n_past, rest = divmod(len(items), per)
    items += [(i, i // 2, 0 if i % 2 else 1) for i in range(nb)]
    n_groups, rest2 = divmod(len(items), per)
    assert nb % 2 == 0 and not rest and not rest2 and 2 <= n_past < n_groups
    tab = np.zeros((6, per * (n_groups + 2)), np.int32)
    for c in range(tab.shape[1]):
        if c < len(items):
            tab[0:3, c] = items[c]
        if 0 <= c - per < len(items):
            tab[3, c] = items[c - per][0]
        if 0 <= c - 2 * per < len(items):
            tab[4:6, c] = items[c - 2 * per][0:2]
    return tab, n_past, n_groups


def _moba_body(tab_ref, q_ref, k_ref, v_ref, o_ref, kx_ref, vt_ref, cb_ref, qx_ref,
               m_ref, acc_ref, s_ring, cm_ring, p_ring, al_ring, *, nb, tab, n_past, n_groups):
    blk = MOBA_BLOCK
    pair = 2 * HEAD_DIM
    seq = nb * blk
    two = 2 * blk
    vrows = vt_ref.shape[2]

    @pl.when((pl.program_id(0) == 0) & (pl.program_id(1) == 0))
    def _():
        row = lax.broadcasted_iota(jnp.int32, (two, pair), 0)
        lane = lax.broadcasted_iota(jnp.int32, (two, pair), 1)
        ones_row = lax.broadcasted_iota(jnp.int32, (vrows - HEAD_DIM, two), 0) == 0
        for jp in range(nb // 2):
            block = jnp.where(row < blk, 2 * jp, 2 * jp + 1)
            kx_ref[jp, :, pair:] = jnp.where(lane == block, 1.0, 0.0).astype(BF16)
            for h in range(2):
                vt_ref[jp, h, HEAD_DIM:, :] = jnp.where(ones_row, 1.0, 0.0).astype(BF16)
        kpos = lax.broadcasted_iota(jnp.int32, (blk, blk), 0)
        qpos = lax.broadcasted_iota(jnp.int32, (blk, blk), 1)
        tri = jnp.where(kpos <= qpos, 0.0, NEG_INF)
        for i in range(nb):
            for h in range(2):
                other = slice((1 - h) * HEAD_DIM, (2 - h) * HEAD_DIM)
                qx_ref[i, other, h * blk:(h + 1) * blk] = jnp.zeros((HEAD_DIM, blk), BF16)
        cb_ref[0] = jnp.concatenate([jnp.zeros((blk, blk), F32), tri], axis=0)
        cb_ref[1] = jnp.concatenate([tri, jnp.full((blk, blk), NEG_INF, F32)], axis=0)

    for jp in range(nb // 2):
        cols = slice(jp * two, (jp + 1) * two)
        kx_ref[jp, :, 0:pair] = k_ref[0, cols, :]
        for h in range(2):
            vt_ref[jp, h, 0:HEAD_DIM, :] = v_ref[0, h * HEAD_DIM:(h + 1) * HEAD_DIM, cols]
    km = jnp.mean(k_ref[0].astype(F32).reshape(nb, blk, pair), axis=1)

    q2 = q_ref[0]
    kdim = lax.broadcasted_iota(jnp.int32, km.shape, 1)
    nidx = lax.broadcasted_iota(jnp.int32, (nb, seq), 0)
    qblk = lax.shift_right_logical(lax.broadcasted_iota(jnp.int32, (nb, seq), 1), blk.bit_length() - 1)
    for h in range(2):
        dims = slice(h * HEAD_DIM, (h + 1) * HEAD_DIM)
        km_h = jnp.where((kdim < HEAD_DIM) if h == 0 else (kdim >= HEAD_DIM), km, 0.0).astype(BF16)
        g = jnp.dot(km_h, q2, preferred_element_type=F32)
        g = jnp.where(nidx < qblk, g, NEG_INF)
        picked = jnp.zeros((nb, seq), jnp.int32)
        for _ in range(min(MOBA_TOPK, nb - 1)):
            top = jnp.max(g, axis=0, keepdims=True)
            first = jnp.min(jnp.where(g == top, nidx, nb), axis=0, keepdims=True)
            hit = nidx == first
            picked = jnp.where(hit, 1, picked)
            g = jnp.where(hit, TAKEN, g)
        allow = ((picked == 1) & (nidx < qblk)) | (nidx == qblk)
        bias = jnp.where(allow, 0.0, NEG_INF)
        bias = jnp.concatenate([bias, jnp.zeros((pair - nb, seq), F32)], axis=0).astype(BF16)
        for i in range(nb):
            qx_ref[i, dims, h * blk:(h + 1) * blk] = q_ref[0, dims, i * blk:(i + 1) * blk]
            qx_ref[i, pair:, h * blk:(h + 1) * blk] = bias[:, i * blk:(i + 1) * blk]

    m_ref[...] = jnp.full(m_ref.shape, NEG_INF, F32)
    acc_ref[...] = jnp.zeros_like(acc_ref)

    def entry(row, c):
        return int(tab[row, c]) if isinstance(c, int) else tab_ref[row, c]

    def step(k, carry, own_block, stages=(1, 2, 3)):
        for e in range(MOBA_ITEMS):
            c = MOBA_ITEMS * k + e
            for h in range(2):
                r = 2 * e + h
                if 3 in stages:
                    a3 = 2 * entry(4, c) + h
                    pv = jnp.dot(vt_ref[entry(5, c), h], p_ring[r], preferred_element_type=F32)
                    acc_ref[a3] = al_ring[r] * acc_ref[a3] + pv
                if 2 in stages:
                    a2 = 2 * entry(3, c) + h
                    m_old = m_ref[a2]
                    m_new = jnp.maximum(m_old, cm_ring[r])
                    m_ref[a2] = m_new
                    al_ring[r] = jnp.exp2(m_old - m_new)
                    p_ring[r] = jnp.exp2(s_ring[r] - m_new).astype(BF16)
            if 1 in stages:
                s2 = jnp.dot(kx_ref[entry(1, c)], qx_ref[entry(0, c)], preferred_element_type=F32)
                for h in range(2):
                    s = s2[:, h * blk:(h + 1) * blk]
                    if own_block:
                        s = s + cb_ref[entry(2, c)]
                    s_ring[2 * e + h] = s
                    cm_ring[2 * e + h] = jnp.max(s, axis=0, keepdims=True)
        return carry

    step(0, 0, own_block=False, stages=(1,))
    step(1, 0, own_block=False, stages=(1, 2))
    once = jnp.minimum(pl.program_id(0) + 1, 1)
    for k0 in range(2, n_groups, MOBA_UNROLL):
        def block(_, carry, k0=k0):
            for k in range(k0, min(k0 + MOBA_UNROLL, n_groups)):
                step(k, 0, own_block=k >= n_past)
            return carry

        lax.fori_loop(0, once, block, 0)
    step(n_groups, 0, own_block=False, stages=(2, 3))
    step(n_groups + 1, 0, own_block=False, stages=(3,))

    for i in range(nb):
        o = jnp.concatenate([acc_ref[2 * i + h, 0:HEAD_DIM, :] / acc_ref[2 * i + h, HEAD_DIM:HEAD_DIM + 1, :]
                             for h in range(2)], axis=0)
        o_ref[0, i * blk:(i + 1) * blk, :] = o.T.astype(o_ref.dtype)


def _moba(qt, k, vt):
    bsz, seq, width = k.shape
    blk = MOBA_BLOCK
    nb = seq // blk
    pair = 2 * HEAD_DIM
    tab, n_past, n_groups = _moba_tables(nb)
    spec = pl.BlockSpec((1, seq, pair), lambda b, hp, tab_ref: (b, 0, hp))
    spec_t = pl.BlockSpec((1, pair, seq), lambda b, hp, tab_ref: (b, hp, 0))
    n_state = 2 * nb
    n_ring = 2 * MOBA_ITEMS
    vrows = HEAD_DIM + BF16_SUBLANES
    return pl.pallas_call(
        functools.partial(_moba_body, nb=nb, tab=tab, n_past=n_past, n_groups=n_groups),
        grid_spec=pltpu.PrefetchScalarGridSpec(
            num_scalar_prefetch=1,
            grid=(bsz, width // pair),
            in_specs=[spec_t, spec, spec_t],
            out_specs=spec,
            scratch_shapes=[
                pltpu.VMEM((nb // 2, 2 * blk, 2 * pair), BF16),
                pltpu.VMEM((nb // 2, 2, vrows, 2 * blk), BF16),
                pltpu.VMEM((2, 2 * blk, blk), F32),
                pltpu.VMEM((nb, 2 * pair, 2 * blk), BF16),
                pltpu.VMEM((n_state, 1, blk), F32),
                pltpu.VMEM((n_state, vrows, blk), F32),
                pltpu.VMEM((n_ring, 2 * blk, blk), F32),
                pltpu.VMEM((n_ring, 1, blk), F32),
                pltpu.VMEM((n_ring, 2 * blk, blk), BF16),
                pltpu.VMEM((n_ring, 1, blk), F32),
            ]),
        out_shape=jax.ShapeDtypeStruct((bsz, seq, width), BF16),
        compiler_params=_params(("arbitrary", "arbitrary")),
        name="moba",
    )(jnp.asarray(tab), qt, k, vt)


def _post_body(x_ref, ya_ref, yb_ref, ga_ref, gb_ref, wa_ref, wb_ref, wo_ref, gpm_ref, gpf_ref,
               w1_ref, w2_ref, gpo_ref, o_ref):
    a = jnp.dot(ya_ref[...], wa_ref[...], preferred_element_type=F32)
    b = jnp.dot(yb_ref[...], wb_ref[...], preferred_element_type=F32)
    merged = (jax.nn.sigmoid(ga_ref[...].astype(F32)) * a
              + jax.nn.sigmoid(gb_ref[...].astype(F32)) * b)
    mix = jnp.dot(merged.astype(BF16), wo_ref[...], preferred_element_type=F32)
    h = x_ref[...] + _rms(mix, gpm_ref[...])
    f = _rms(h, gpf_ref[...]).astype(BF16)
    d_ff = w1_ref.shape[1]
    ck = min(FFN_CHUNK, d_ff)
    acc = jnp.zeros(h.shape, F32)
    for c in range(d_ff // ck):
        t = jnp.dot(f, w1_ref[:, c * ck:(c + 1) * ck], preferred_element_type=F32)
        t = jnp.square(jnp.maximum(t, 0.0)).astype(BF16)
        acc = acc + jnp.dot(t, w2_ref[c * ck:(c + 1) * ck, :], preferred_element_type=F32)
    o_ref[...] = h + _rms(acc, gpo_ref[...])


def _post(x2, ya, yb, ga, gb, wa, wb, wo, gpm, gpf, w1, w2, gpo):
    t, d = x2.shape
    tm = min(POST_ROWS, t)

    def rows(a):
        return pl.BlockSpec((tm, a.shape[1]), lambda i: (i, 0))

    consts = (wa, wb, wo, gpm, gpf, w1, w2, gpo)
    return pl.pallas_call(
        _post_body,
        grid=(t // tm,),
        in_specs=[rows(a) for a in (x2, ya, yb, ga, gb)] + [_resident(c.shape) for c in consts],
        out_specs=rows(x2),
        out_shape=jax.ShapeDtypeStruct((t, d), x2.dtype),
        compiler_params=_params(("parallel",)),
        name="post",
    )(x2, ya, yb, ga, gb, *consts)


def _layer(h, g_pre_mix, w_in, lam_re, lam_im, log_dt, b_re, b_im, c_re, c_im, d_skip, w_glu, b_glu,
           w_branch_a, w_branch_b, w_out, g_post_mix, g_pre_ffn, w_ff1, w_ff2, g_post_ffn):
    bsz, seq, d = h.shape
    ssm_w = lam_re.shape[0] * SSM_GROUP
    attn_w = ATTN_HEADS * HEAD_DIM
    x2 = h.reshape(bsz * seq, d)

    zs, qt, k, vt, ga, gb = _inproj(h, g_pre_mix.reshape(1, d), w_in.astype(BF16),
                                    (ssm_w, attn_w, attn_w, attn_w, d, d),
                                    ("lbc", "bcl", "blc", "bcl", "blc", "blc"),
                                    (1.0, MOBA_Q_SCALE, 1.0, 1.0, 1.0, 1.0))

    ar, ai, bbr, bbi = _s5_prep(lam_re, lam_im, log_dt, b_re, b_im)
    g, p = lam_re.shape
    hg = g // 2
    n_half = hg * p

    def halves(m):
        return jnp.stack([_block_diag(m[:hg]), _block_diag(m[hg:])])

    bmat = jnp.concatenate([halves(bbr), halves(bbi)], axis=2).astype(BF16)
    cre = halves(jnp.transpose(c_re, (0, 2, 1))).astype(BF16)
    cim = halves(jnp.transpose(c_im, (0, 2, 1))).astype(BF16)
    ar8 = jnp.broadcast_to(ar.reshape(1, 2 * n_half), (bsz, 2 * n_half))
    ai8 = jnp.broadcast_to(ai.reshape(1, 2 * n_half), (bsz, 2 * n_half))

    ya = _s5_mixer(zs.reshape(seq * bsz, ssm_w), bsz, bmat, ar8, ai8, cre, cim, d_skip.reshape(1, ssm_w),
                   w_glu.astype(BF16), b_glu.reshape(1, ssm_w)).reshape(bsz * seq, ssm_w)
    yb = _moba(qt, k, vt).reshape(bsz * seq, attn_w)
    ga, gb = ga.reshape(bsz * seq, d), gb.reshape(bsz * seq, d)

    out = _post(x2, ya, yb, ga, gb, w_branch_a.astype(BF16), w_branch_b.astype(BF16),
                w_out.astype(BF16), g_post_mix.reshape(1, d), g_pre_ffn.reshape(1, d),
                w_ff1.astype(BF16), w_ff2.astype(BF16), g_post_ffn.reshape(1, d))
    return out.reshape(bsz, seq, d)


def kernel(x, g_pre_mix, w_in, lam_re, lam_im, log_dt, b_re, b_im, c_re, c_im, d_skip, w_glu, b_glu,
           w_branch_a, w_branch_b, w_out, g_post_mix, g_pre_ffn, w_ff1, w_ff2, g_post_ffn):
    per_layer = (g_pre_mix, w_in, lam_re, lam_im, log_dt, b_re, b_im, c_re, c_im, d_skip, w_glu, b_glu,
                 w_branch_a, w_branch_b, w_out, g_post_mix, g_pre_ffn, w_ff1, w_ff2, g_post_ffn)
    h = x
    for l in range(g_pre_mix.shape[0]):
        h = _layer(h, *(p[l] for p in per_layer))
    return h
```

```python
import functools

import jax
import jax.numpy as jnp
import numpy as np
from jax import lax
from jax.experimental import pallas as pl
from jax.experimental.pallas import tpu as pltpu

F32 = jnp.float32
BF16 = jnp.bfloat16

RMS_EPS = 1e-6
NEG_INF = -1e30
TAKEN = -3e38

SSM_GROUP = 16
ATTN_HEADS = 8
HEAD_DIM = 64
MOBA_BLOCK = 256
MOBA_TOPK = 3
MOBA_ITEMS = 4

LOG2_E = 1.4426950408889634
MOBA_Q_SCALE = LOG2_E * HEAD_DIM ** -0.5

SUBLANES = 8
BF16_SUBLANES = 16
VMEM_LIMIT_BYTES = 56 * 1024 * 1024

INPROJ_ROWS = 1024
POST_ROWS = 512
S5_TIME_CHUNK = 64
S5_COL_GROUP = 512
FFN_CHUNK = 1024


def _params(sem):
    return pltpu.CompilerParams(dimension_semantics=sem, vmem_limit_bytes=VMEM_LIMIT_BYTES)


def _resident(shape):
    zeros = (0,) * len(shape)
    return pl.BlockSpec(shape, lambda *_: zeros, pipeline_mode=pl.Buffered(1))


def _rms(x, g):
    var = jnp.mean(x * x, axis=-1, keepdims=True)
    return (x * lax.rsqrt(var + RMS_EPS)) * g


def _inproj_body(x_ref, g_ref, w_ref, *refs, layouts, scales):
    n_out = len(layouts)
    n_cast = (len(refs) - n_out) // 2
    cast_src, out_refs, cast_dst = refs[:n_cast], refs[n_cast:n_cast + n_out], refs[n_cast + n_out:]
    for src, dst in zip(cast_src, cast_dst):
        dst[...] = src[...].astype(dst.dtype)

    bsz, tl, d = x_ref.shape
    ub = _rms(x_ref[...].reshape(bsz * tl, d), g_ref[...]).astype(BF16)
    col = 0
    for ref, layout, scale in zip(out_refs, layouts, scales):
        n = ref.shape[1] if layout == "bcl" else ref.shape[2]
        z = jnp.dot(ub, w_ref[:, col:col + n], preferred_element_type=F32).reshape(bsz, tl, n)
        if scale != 1.0:
            z = z * scale
        if layout == "lbc":
            z = jnp.swapaxes(z.astype(ref.dtype), 0, 1)
        elif layout == "bcl":
            z = jnp.swapaxes(z, 1, 2)
        ref[...] = z.astype(ref.dtype)
        col += n


def _inproj(x, g, w, widths, layouts, scales, to_bf16):
    bsz, seq, d = x.shape
    tl = min(INPROJ_ROWS // bsz, seq)
    steps = seq // tl
    specs = {"blc": lambda n: (pl.BlockSpec((bsz, tl, n), lambda i: (0, i, 0)), (bsz, seq, n)),
             "lbc": lambda n: (pl.BlockSpec((tl, bsz, n), lambda i: (i, 0, 0)), (seq, bsz, n)),
             "bcl": lambda n: (pl.BlockSpec((bsz, n, tl), lambda i: (0, 0, i)), (bsz, n, seq))}
    outs = [specs[layout](n) for n, layout in zip(widths, layouts)]
    slabs = [pl.BlockSpec((a.shape[0] // steps, a.shape[1]), lambda i: (i, 0)) for a in to_bf16]
    assert all(a.shape[0] % (steps * BF16_SUBLANES) == 0 for a in to_bf16)
    res = pl.pallas_call(
        functools.partial(_inproj_body, layouts=layouts, scales=scales),
        grid=(steps,),
        in_specs=[pl.BlockSpec((bsz, tl, d), lambda i: (0, i, 0)), _resident(g.shape), _resident(w.shape)]
        + slabs,
        out_specs=[spec for spec, _ in outs] + slabs,
        out_shape=[jax.ShapeDtypeStruct(shape, BF16) for _, shape in outs]
        + [jax.ShapeDtypeStruct(a.shape, BF16) for a in to_bf16],
        compiler_params=_params(("arbitrary",)),
        name="inproj",
    )(x, g, w, *to_bf16)
    return res[:len(outs)], res[len(outs):]


def _s5_prep_body(lr_ref, li_ref, ldt_ref, br_ref, bi_ref, ar_ref, ai_ref, bbr_ref, bbi_ref):
    lr = lr_ref[...]
    li = li_ref[...]
    dt = jnp.exp(ldt_ref[...])
    mag = jnp.exp(lr * dt)
    ab_re = mag * jnp.cos(li * dt)
    ab_im = mag * jnp.sin(li * dt)
    nr = ab_re - 1.0
    ni = ab_im
    den = lr * lr + li * li
    coef_re = (nr * lr + ni * li) / den
    coef_im = (ni * lr - nr * li) / den
    br = br_ref[...]
    bi = bi_ref[...]
    ar_ref[...] = ab_re
    ai_ref[...] = ab_im
    bbr_ref[...] = coef_re * br - coef_im * bi
    bbi_ref[...] = coef_re * bi + coef_im * br


def _s5_prep(lam_re, lam_im, log_dt, b_re, b_im):
    g, p = lam_re.shape
    h = b_re.shape[-1]
    lr = lam_re.reshape(g, 1, p)
    li = lam_im.reshape(g, 1, p)
    ldt = jnp.broadcast_to(log_dt.reshape(g, 1, 1), (g, 1, p))
    br = jnp.transpose(b_re, (0, 2, 1))
    bi = jnp.transpose(b_im, (0, 2, 1))
    small = jax.ShapeDtypeStruct((g, 1, p), F32)
    big = jax.ShapeDtypeStruct((g, h, p), F32)
    return pl.pallas_call(_s5_prep_body, out_shape=[small, small, big, big], name="s5_prep")(lr, li, ldt, br, bi)


def _block_diag(m):
    g, r, c = m.shape
    on_diag = jnp.eye(g, dtype=bool)[:, None, :, None]
    return jnp.where(on_diag, m[:, :, None, :], jnp.zeros((), m.dtype)).reshape(g * r, g * c)


def _s5_body(u_ref, b_ref, ar_ref, ai_ref, cre_ref, cim_ref, dsk_ref, wg_ref, bg_ref, y_ref,
             bu_ref, st_ref, *, steps):
    half_in = u_ref.shape[1] // 2
    half_st = st_ref.shape[1] // 4

    @pl.when(pl.program_id(0) == 0)
    def _():
        st_ref[...] = jnp.zeros_like(st_ref)

    for c in range(2):
        bu_ref[:, 2 * c * half_st:2 * (c + 1) * half_st] = jnp.dot(
            u_ref[:, c * half_in:(c + 1) * half_in], b_ref[c], preferred_element_type=F32)

    w = S5_COL_GROUP
    for c in range(2):
        for s in range(half_st // w):
            re0 = 2 * c * half_st + s * w
            im0 = re0 + half_st
            a0 = c * half_st + s * w
            ar = ar_ref[:, a0:a0 + w]
            ai = ai_ref[:, a0:a0 + w]
            xr = st_ref[:, re0:re0 + w]
            xi = st_ref[:, im0:im0 + w]
            for t in range(steps):
                rows = slice(t * SUBLANES, (t + 1) * SUBLANES)
                xr, xi = (ar * xr - ai * xi + bu_ref[rows, re0:re0 + w],
                          ar * xi + ai * xr + bu_ref[rows, im0:im0 + w])
                bu_ref[rows, re0:re0 + w] = xr
                bu_ref[rows, im0:im0 + w] = xi
            st_ref[:, re0:re0 + w] = xr
            st_ref[:, im0:im0 + w] = xi

    ys = []
    for c in range(2):
        xre = bu_ref[:, 2 * c * half_st:(2 * c + 1) * half_st].astype(BF16)
        xim = bu_ref[:, (2 * c + 1) * half_st:(2 * c + 2) * half_st].astype(BF16)
        ys.append(jnp.dot(xre, cre_ref[c], preferred_element_type=F32)
                  - jnp.dot(xim, cim_ref[c], preferred_element_type=F32))
    y = jnp.concatenate(ys, axis=1) + dsk_ref[...] * u_ref[...].astype(F32)
    v = jax.nn.gelu(y)
    gate = jnp.dot(v.astype(BF16), wg_ref[...], preferred_element_type=F32) + bg_ref[...]
    out = (v * jax.nn.sigmoid(gate)).astype(y_ref.dtype)
    bsz = y_ref.shape[0]
    y_ref[...] = jnp.swapaxes(out.reshape(steps, bsz, out.shape[1]), 0, 1)


def _s5_mixer(u_tm, bsz, bmat, ar, ai, cre, cim, d_skip, w_glu, b_glu):
    rows, width = u_tm.shape
    seq = rows // bsz
    steps = min(S5_TIME_CHUNK, seq)
    r = steps * bsz
    n_state = ar.shape[1] * 2
    return pl.pallas_call(
        functools.partial(_s5_body, steps=steps),
        grid=(seq // steps,),
        in_specs=[pl.BlockSpec((r, width), lambda i: (i, 0)),
                  _resident(bmat.shape), _resident(ar.shape), _resident(ai.shape),
                  _resident(cre.shape), _resident(cim.shape), _resident(d_skip.shape),
                  _resident(w_glu.shape), _resident(b_glu.shape)],
        out_specs=pl.BlockSpec((bsz, steps, width), lambda i: (0, i, 0)),
        out_shape=jax.ShapeDtypeStruct((bsz, seq, width), BF16),
        scratch_shapes=[pltpu.VMEM((r, n_state), F32), pltpu.VMEM((bsz, n_state), F32)],
        compiler_params=_params(("arbitrary",)),
        name="s5_mixer",
    )(u_tm, bmat, ar, ai, cre, cim, d_skip, w_glu, b_glu)


def _moba_tables(nb):
    per = MOBA_ITEMS
    items = [(i, jp, 0) for i in range(nb) for jp in range(i // 2)]
    n_past, rest = divmod(len(items), per)
    items += [(i, i // 2, 0 if i % 2 else 1) for i in range(nb)]
    n_groups, rest2 = divmod(len(items), per)
    assert nb % 2 == 0 and not rest and not rest2 and 2 <= n_past < n_groups
    tab = np.zeros((6, per * (n_groups + 2)), np.int32)
    for c in range(tab.shape[1]):
        if c < len(items):
            tab[0:3, c] = items[c]
        if 0 <= c - per < len(items):
            tab[3, c] = items[c - per][0]
        if 0 <= c - 2 * per < len(items):
            tab[4:6, c] = items[c - 2 * per][0:2]
    return tab, n_past, n_groups


def _moba_body(tab_ref, q_ref, k_ref, v_ref, o_ref, kx_ref, vt_ref, cb_ref, qx_ref,
               m_ref, acc_ref, s_ring, cm_ring, p_ring, al_ring, *, nb, tab, n_past, n_groups):
    blk = MOBA_BLOCK
    pair = 2 * HEAD_DIM
    seq = nb * blk
    two = 2 * blk
    vrows = vt_ref.shape[2]

    @pl.when((pl.program_id(0) == 0) & (pl.program_id(1) == 0))
    def _():
        row = lax.broadcasted_iota(jnp.int32, (two, pair), 0)
        lane = lax.broadcasted_iota(jnp.int32, (two, pair), 1)
        ones_row = lax.broadcasted_iota(jnp.int32, (vrows - HEAD_DIM, two), 0) == 0
        for jp in range(nb // 2):
            block = jnp.where(row < blk, 2 * jp, 2 * jp + 1)
            kx_ref[jp, :, pair:] = jnp.where(lane == block, 1.0, 0.0).astype(BF16)
            for h in range(2):
                vt_ref[jp, h, HEAD_DIM:, :] = jnp.where(ones_row, 1.0, 0.0).astype(BF16)
        kpos = lax.broadcasted_iota(jnp.int32, (blk, blk), 0)
        qpos = lax.broadcasted_iota(jnp.int32, (blk, blk), 1)
        tri = jnp.where(kpos <= qpos, 0.0, NEG_INF)
        for i in range(nb):
            for h in range(2):
                other = slice((1 - h) * HEAD_DIM, (2 - h) * HEAD_DIM)
                qx_ref[i, other, h * blk:(h + 1) * blk] = jnp.zeros((HEAD_DIM, blk), BF16)
        cb_ref[0] = jnp.concatenate([jnp.zeros((blk, blk), F32), tri], axis=0)
        cb_ref[1] = jnp.concatenate([tri, jnp.full((blk, blk), NEG_INF, F32)], axis=0)

    for jp in range(nb // 2):
        cols = slice(jp * two, (jp + 1) * two)
        kx_ref[jp, :, 0:pair] = k_ref[0, cols, :]
        for h in range(2):
            vt_ref[jp, h, 0:HEAD_DIM, :] = v_ref[0, h * HEAD_DIM:(h + 1) * HEAD_DIM, cols]
    km = jnp.mean(k_ref[0].astype(F32).reshape(nb, blk, pair), axis=1)

    q2 = q_ref[0]
    kdim = lax.broadcasted_iota(jnp.int32, km.shape, 1)
    nidx = lax.broadcasted_iota(jnp.int32, (nb, seq), 0)
    qblk = lax.shift_right_logical(lax.broadcasted_iota(jnp.int32, (nb, seq), 1), blk.bit_length() - 1)
    for h in range(2):
        dims = slice(h * HEAD_DIM, (h + 1) * HEAD_DIM)
        km_h = jnp.where((kdim < HEAD_DIM) if h == 0 else (kdim >= HEAD_DIM), km, 0.0).astype(BF16)
        g = jnp.dot(km_h, q2, preferred_element_type=F32)
        g = jnp.where(nidx < qblk, g, NEG_INF)
        picked = jnp.zeros((nb, seq), jnp.int32)
        for _ in range(min(MOBA_TOPK, nb - 1)):
            top = jnp.max(g, axis=0, keepdims=True)
            first = jnp.min(jnp.where(g == top, nidx, nb), axis=0, keepdims=True)
            hit = nidx == first
            picked = jnp.where(hit, 1, picked)
            g = jnp.where(hit, TAKEN, g)
        allow = ((picked == 1) & (nidx < qblk)) | (nidx == qblk)
        bias = jnp.where(allow, 0.0, NEG_INF)
        bias = jnp.concatenate([bias, jnp.zeros((pair - nb, seq), F32)], axis=0).astype(BF16)
        for i in range(nb):
            qx_ref[i, dims, h * blk:(h + 1) * blk] = q_ref[0, dims, i * blk:(i + 1) * blk]
            qx_ref[i, pair:, h * blk:(h + 1) * blk] = bias[:, i * blk:(i + 1) * blk]

    m_ref[...] = jnp.full(m_ref.shape, NEG_INF, F32)
    acc_ref[...] = jnp.zeros_like(acc_ref)

    def entry(row, c):
        return int(tab[row, c]) if isinstance(c, int) else tab_ref[row, c]

    def step(k, carry, own_block, stages=(1, 2, 3)):
        for e in range(MOBA_ITEMS):
            c = MOBA_ITEMS * k + e
            for h in range(2):
                r = 2 * e + h
                if 3 in stages:
                    a3 = 2 * entry(4, c) + h
                    pv = jnp.dot(vt_ref[entry(5, c), h], p_ring[r], preferred_element_type=F32)
                    acc_ref[a3] = al_ring[r] * acc_ref[a3] + pv
                if 2 in stages:
                    a2 = 2 * entry(3, c) + h
                    m_old = m_ref[a2]
                    m_new = jnp.maximum(m_old, cm_ring[r])
                    m_ref[a2] = m_new
                    al_ring[r] = jnp.exp2(m_old - m_new)
                    p_ring[r] = jnp.exp2(s_ring[r] - m_new).astype(BF16)
            if 1 in stages:
                s2 = jnp.dot(kx_ref[entry(1, c)], qx_ref[entry(0, c)], preferred_element_type=F32)
                for h in range(2):
                    s = s2[:, h * blk:(h + 1) * blk]
                    if own_block:
                        s = s + cb_ref[entry(2, c)]
                    s_ring[2 * e + h] = s
                    cm_ring[2 * e + h] = jnp.max(s, axis=0, keepdims=True)
        return carry

    step(0, 0, own_block=False, stages=(1,))
    step(1, 0, own_block=False, stages=(1, 2))
    lax.fori_loop(2, n_past, functools.partial(step, own_block=False), 0)
    lax.fori_loop(n_past, n_groups, functools.partial(step, own_block=True), 0)
    step(n_groups, 0, own_block=False, stages=(2, 3))
    step(n_groups + 1, 0, own_block=False, stages=(3,))

    for i in range(nb):
        o = jnp.concatenate([acc_ref[2 * i + h, 0:HEAD_DIM, :] / acc_ref[2 * i + h, HEAD_DIM:HEAD_DIM + 1, :]
                             for h in range(2)], axis=0)
        o_ref[0, i * blk:(i + 1) * blk, :] = o.T.astype(o_ref.dtype)


def _moba(qt, k, vt):
    bsz, seq, width = k.shape
    blk = MOBA_BLOCK
    nb = seq // blk
    pair = 2 * HEAD_DIM
    tab, n_past, n_groups = _moba_tables(nb)
    spec = pl.BlockSpec((1, seq, pair), lambda b, hp, tab_ref: (b, 0, hp))
    spec_t = pl.BlockSpec((1, pair, seq), lambda b, hp, tab_ref: (b, hp, 0))
    n_state = 2 * nb
    n_ring = 2 * MOBA_ITEMS
    vrows = HEAD_DIM + BF16_SUBLANES
    return pl.pallas_call(
        functools.partial(_moba_body, nb=nb, tab=tab, n_past=n_past, n_groups=n_groups),
        grid_spec=pltpu.PrefetchScalarGridSpec(
            num_scalar_prefetch=1,
            grid=(bsz, width // pair),
            in_specs=[spec_t, spec, spec_t],
            out_specs=spec,
            scratch_shapes=[
                pltpu.VMEM((nb // 2, 2 * blk, 2 * pair), BF16),
                pltpu.VMEM((nb // 2, 2, vrows, 2 * blk), BF16),
                pltpu.VMEM((2, 2 * blk, blk), F32),
                pltpu.VMEM((nb, 2 * pair, 2 * blk), BF16),
                pltpu.VMEM((n_state, 1, blk), F32),
                pltpu.VMEM((n_state, vrows, blk), F32),
                pltpu.VMEM((n_ring, 2 * blk, blk), F32),
                pltpu.VMEM((n_ring, 1, blk), F32),
                pltpu.VMEM((n_ring, 2 * blk, blk), BF16),
                pltpu.VMEM((n_ring, 1, blk), F32),
            ]),
        out_shape=jax.ShapeDtypeStruct((bsz, seq, width), BF16),
        compiler_params=_params(("arbitrary", "arbitrary")),
        name="moba",
    )(jnp.asarray(tab), qt, k, vt)


def _post_body(x_ref, ya_ref, yb_ref, ga_ref, gb_ref, wa_ref, wb_ref, wo_ref, gpm_ref, gpf_ref,
               w1_ref, w2_ref, gpo_ref, o_ref):
    a = jnp.dot(ya_ref[...], wa_ref[...], preferred_element_type=F32)
    b = jnp.dot(yb_ref[...], wb_ref[...], preferred_element_type=F32)
    merged = (jax.nn.sigmoid(ga_ref[...].astype(F32)) * a
              + jax.nn.sigmoid(gb_ref[...].astype(F32)) * b)
    mix = jnp.dot(merged.astype(BF16), wo_ref[...], preferred_element_type=F32)
    h = x_ref[...] + _rms(mix, gpm_ref[...])
    f = _rms(h, gpf_ref[...]).astype(BF16)
    d_ff = w1_ref.shape[1]
    ck = min(FFN_CHUNK, d_ff)
    acc = jnp.zeros(h.shape, F32)
    for c in range(d_ff // ck):
        t = jnp.dot(f, w1_ref[:, c * ck:(c + 1) * ck], preferred_element_type=F32)
        t = jnp.square(jnp.maximum(t, 0.0)).astype(BF16)
        acc = acc + jnp.dot(t, w2_ref[c * ck:(c + 1) * ck, :], preferred_element_type=F32)
    o_ref[...] = h + _rms(acc, gpo_ref[...])


def _post(x2, ya, yb, ga, gb, wa, wb, wo, gpm, gpf, w1, w2, gpo):
    t, d = x2.shape
    tm = min(POST_ROWS, t)

    def rows(a):
        return pl.BlockSpec((tm, a.shape[1]), lambda i: (i, 0))

    consts = (wa, wb, wo, gpm, gpf, w1, w2, gpo)
    return pl.pallas_call(
        _post_body,
        grid=(t // tm,),
        in_specs=[rows(a) for a in (x2, ya, yb, ga, gb)] + [_resident(c.shape) for c in consts],
        out_specs=rows(x2),
        out_shape=jax.ShapeDtypeStruct((t, d), x2.dtype),
        compiler_params=_params(("parallel",)),
        name="post",
    )(x2, ya, yb, ga, gb, *consts)


def _layer(h, g_pre_mix, w_in, lam_re, lam_im, log_dt, b_re, b_im, c_re, c_im, d_skip, w_glu, b_glu,
           w_branch_a, w_branch_b, w_out, g_post_mix, g_pre_ffn, w_ff1, w_ff2, g_post_ffn):
    bsz, seq, d = h.shape
    ssm_w = lam_re.shape[0] * SSM_GROUP
    attn_w = ATTN_HEADS * HEAD_DIM
    x2 = h.reshape(bsz * seq, d)

    (zs, qt, k, vt, ga, gb), (w_glu, w_branch_a, w_branch_b, w_out, w_ff1, w_ff2) = _inproj(
        h, g_pre_mix.reshape(1, d), w_in.astype(BF16),
        (ssm_w, attn_w, attn_w, attn_w, d, d),
        ("lbc", "bcl", "blc", "bcl", "blc", "blc"),
        (1.0, MOBA_Q_SCALE, 1.0, 1.0, 1.0, 1.0),
        (w_glu, w_branch_a, w_branch_b, w_out, w_ff1, w_ff2))

    ar, ai, bbr, bbi = _s5_prep(lam_re, lam_im, log_dt, b_re, b_im)
    g, p = lam_re.shape
    hg = g // 2
    n_half = hg * p

    def halves(m):
        return jnp.stack([_block_diag(m[:hg]), _block_diag(m[hg:])])

    bmat = jnp.concatenate([halves(bbr), halves(bbi)], axis=2).astype(BF16)
    cre = halves(jnp.transpose(c_re, (0, 2, 1))).astype(BF16)
    cim = halves(jnp.transpose(c_im, (0, 2, 1))).astype(BF16)
    ar8 = jnp.broadcast_to(ar.reshape(1, 2 * n_half), (bsz, 2 * n_half))
    ai8 = jnp.broadcast_to(ai.reshape(1, 2 * n_half), (bsz, 2 * n_half))

    ya = _s5_mixer(zs.reshape(seq * bsz, ssm_w), bsz, bmat, ar8, ai8, cre, cim, d_skip.reshape(1, ssm_w),
                   w_glu, b_glu.reshape(1, ssm_w)).reshape(bsz * seq, ssm_w)
    yb = _moba(qt, k, vt).reshape(bsz * seq, attn_w)
    ga, gb = ga.reshape(bsz * seq, d), gb.reshape(bsz * seq, d)

    out = _post(x2, ya, yb, ga, gb, w_branch_a, w_branch_b, w_out, g_post_mix.reshape(1, d),
                g_pre_ffn.reshape(1, d), w_ff1, w_ff2, g_post_ffn.reshape(1, d))
    return out.reshape(bsz, seq, d)


def kernel(x, g_pre_mix, w_in, lam_re, lam_im, log_dt, b_re, b_im, c_re, c_im, d_skip, w_glu, b_glu,
           w_branch_a, w_branch_b, w_out, g_post_mix, g_pre_ffn, w_ff1, w_ff2, g_post_ffn):
    per_layer = (g_pre_mix, w_in, lam_re, lam_im, log_dt, b_re, b_im, c_re, c_im, d_skip, w_glu, b_glu,
                 w_branch_a, w_branch_b, w_out, g_post_mix, g_pre_ffn, w_ff1, w_ff2, g_post_ffn)
    h = x
    for l in range(g_pre_mix.shape[0]):
        h = _layer(h, *(p[l] for p in per_layer))
    return h
```

```python
import functools

import jax
import jax.numpy as jnp
import numpy as np
from jax import lax
from jax.experimental import pallas as pl
from jax.experimental.pallas import tpu as pltpu

F32 = jnp.float32
BF16 = jnp.bfloat16

RMS_EPS = 1e-6
NEG_INF = -1e30
TAKEN = -3e38

SSM_GROUP = 16
ATTN_HEADS = 8
HEAD_DIM = 64
MOBA_BLOCK = 256
MOBA_TOPK = 3
MOBA_ITEMS = 4

LOG2_E = 1.4426950408889634
MOBA_Q_SCALE = LOG2_E * HEAD_DIM ** -0.5

SUBLANES = 8
BF16_SUBLANES = 16
VMEM_LIMIT_BYTES = 56 * 1024 * 1024

INPROJ_ROWS = 1024
POST_ROWS = 512
S5_TIME_CHUNK = 64
S5_COL_GROUP = 512
FFN_CHUNK = 1024


def _params(sem):
    return pltpu.CompilerParams(dimension_semantics=sem, vmem_limit_bytes=VMEM_LIMIT_BYTES)


def _resident(shape):
    zeros = (0,) * len(shape)
    return pl.BlockSpec(shape, lambda *_: zeros, pipeline_mode=pl.Buffered(1))


def _rms(x, g):
    var = jnp.mean(x * x, axis=-1, keepdims=True)
    return (x * lax.rsqrt(var + RMS_EPS)) * g


def _inproj_body(x_ref, g_ref, w_ref, *refs, layouts, scales):
    n_out = len(layouts)
    n_cast = (len(refs) - n_out) // 2
    cast_src, out_refs, cast_dst = refs[:n_cast], refs[n_cast:n_cast + n_out], refs[n_cast + n_out:]
    for src, dst in zip(cast_src, cast_dst):
        dst[...] = src[...].astype(dst.dtype)

    bsz, tl, d = x_ref.shape
    ub = _rms(x_ref[...].reshape(bsz * tl, d), g_ref[...]).astype(BF16)
    col = 0
    for ref, layout, scale in zip(out_refs, layouts, scales):
        n = ref.shape[1] if layout == "bcl" else ref.shape[2]
        z = jnp.dot(ub, w_ref[:, col:col + n], preferred_element_type=F32).reshape(bsz, tl, n)
        if scale != 1.0:
            z = z * scale
        if layout == "lbc":
            z = jnp.swapaxes(z.astype(ref.dtype), 0, 1)
        elif layout == "bcl":
            z = jnp.swapaxes(z, 1, 2)
        ref[...] = z.astype(ref.dtype)
        col += n


def _inproj(x, g, w, widths, layouts, scales, to_bf16):
    bsz, seq, d = x.shape
    tl = min(INPROJ_ROWS // bsz, seq)
    steps = seq // tl
    specs = {"blc": lambda n: (pl.BlockSpec((bsz, tl, n), lambda i: (0, i, 0)), (bsz, seq, n)),
             "lbc": lambda n: (pl.BlockSpec((tl, bsz, n), lambda i: (i, 0, 0)), (seq, bsz, n)),
             "bcl": lambda n: (pl.BlockSpec((bsz, n, tl), lambda i: (0, 0, i)), (bsz, n, seq))}
    outs = [specs[layout](n) for n, layout in zip(widths, layouts)]
    slabs = [pl.BlockSpec((a.shape[0] // steps, a.shape[1]), lambda i: (i, 0)) for a in to_bf16]
    assert all(a.shape[0] % (steps * BF16_SUBLANES) == 0 for a in to_bf16)
    res = pl.pallas_call(
        functools.partial(_inproj_body, layouts=layouts, scales=scales),
        grid=(steps,),
        in_specs=[pl.BlockSpec((bsz, tl, d), lambda i: (0, i, 0)), _resident(g.shape), _resident(w.shape)]
        + slabs,
        out_specs=[spec for spec, _ in outs] + slabs,
        out_shape=[jax.ShapeDtypeStruct(shape, BF16) for _, shape in outs]
        + [jax.ShapeDtypeStruct(a.shape, BF16) for a in to_bf16],
        compiler_params=_params(("arbitrary",)),
        name="inproj",
    )(x, g, w, *to_bf16)
    return res[:len(outs)], res[len(outs):]


def _s5_prep_body(lr_ref, li_ref, ldt_ref, br_ref, bi_ref, crt_ref, cit_ref,
                  ar_ref, ai_ref, bmat_ref, cre_ref, cim_ref):
    lr = lr_ref[...]
    li = li_ref[...]
    dt = jnp.exp(ldt_ref[...])
    mag = jnp.exp(lr * dt)
    ab_re = mag * jnp.cos(li * dt)
    ab_im = mag * jnp.sin(li * dt)
    nr = ab_re - 1.0
    ni = ab_im
    den = lr * lr + li * li
    coef_re = (nr * lr + ni * li) / den
    coef_im = (ni * lr - nr * li) / den
    br = br_ref[...]
    bi = bi_ref[...]
    bbr = coef_re * br - coef_im * bi
    bbi = coef_re * bi + coef_im * br

    g, h, p = br.shape
    hg = g // 2
    bmat_ref[...] = jnp.zeros_like(bmat_ref)
    cre_ref[...] = jnp.zeros_like(cre_ref)
    cim_ref[...] = jnp.zeros_like(cim_ref)
    for gi in range(g):
        c, gl = divmod(gi, hg)
        rows, cols = slice(gl * h, (gl + 1) * h), slice(gl * p, (gl + 1) * p)
        bmat_ref[c, rows, cols] = bbr[gi].astype(BF16)
        bmat_ref[c, rows, hg * p + gl * p:hg * p + (gl + 1) * p] = bbi[gi].astype(BF16)
        cre_ref[c, cols, rows] = crt_ref[gi].astype(BF16)
        cim_ref[c, cols, rows] = cit_ref[gi].astype(BF16)
        ar_ref[:, gi * p:(gi + 1) * p] = jnp.broadcast_to(ab_re[gi], (ar_ref.shape[0], p))
        ai_ref[:, gi * p:(gi + 1) * p] = jnp.broadcast_to(ab_im[gi], (ai_ref.shape[0], p))


def _s5_prep(lam_re, lam_im, log_dt, b_re, b_im, c_re, c_im, bsz):
    g, p = lam_re.shape
    h = b_re.shape[-1]
    lr = lam_re.reshape(g, 1, p)
    li = lam_im.reshape(g, 1, p)
    ldt = jnp.broadcast_to(log_dt.reshape(g, 1, 1), (g, 1, p))
    br = jnp.transpose(b_re, (0, 2, 1))
    bi = jnp.transpose(b_im, (0, 2, 1))
    crt = jnp.transpose(c_re, (0, 2, 1))
    cit = jnp.transpose(c_im, (0, 2, 1))
    hg = g // 2
    coef = jax.ShapeDtypeStruct((bsz, g * p), F32)
    bmat = jax.ShapeDtypeStruct((2, hg * h, 2 * hg * p), BF16)
    cmat = jax.ShapeDtypeStruct((2, hg * p, hg * h), BF16)
    return pl.pallas_call(_s5_prep_body, out_shape=[coef, coef, bmat, cmat, cmat],
                          name="s5_prep")(lr, li, ldt, br, bi, crt, cit)


def _s5_body(u_ref, b_ref, ar_ref, ai_ref, cre_ref, cim_ref, dsk_ref, wg_ref, bg_ref, y_ref,
             bu_ref, st_ref, *, steps):
    half_in = u_ref.shape[1] // 2
    half_st = st_ref.shape[1] // 4

    @pl.when(pl.program_id(0) == 0)
    def _():
        st_ref[...] = jnp.zeros_like(st_ref)

    for c in range(2):
        bu_ref[:, 2 * c * half_st:2 * (c + 1) * half_st] = jnp.dot(
            u_ref[:, c * half_in:(c + 1) * half_in], b_ref[c], preferred_element_type=F32)

    w = S5_COL_GROUP
    for c in range(2):
        for s in range(half_st // w):
            re0 = 2 * c * half_st + s * w
            im0 = re0 + half_st
            a0 = c * half_st + s * w
            ar = ar_ref[:, a0:a0 + w]
            ai = ai_ref[:, a0:a0 + w]
            xr = st_ref[:, re0:re0 + w]
            xi = st_ref[:, im0:im0 + w]
            for t in range(steps):
                rows = slice(t * SUBLANES, (t + 1) * SUBLANES)
                xr, xi = (ar * xr - ai * xi + bu_ref[rows, re0:re0 + w],
                          ar * xi + ai * xr + bu_ref[rows, im0:im0 + w])
                bu_ref[rows, re0:re0 + w] = xr
                bu_ref[rows, im0:im0 + w] = xi
            st_ref[:, re0:re0 + w] = xr
            st_ref[:, im0:im0 + w] = xi

    ys = []
    for c in range(2):
        xre = bu_ref[:, 2 * c * half_st:(2 * c + 1) * half_st].astype(BF16)
        xim = bu_ref[:, (2 * c + 1) * half_st:(2 * c + 2) * half_st].astype(BF16)
        ys.append(jnp.dot(xre, cre_ref[c], preferred_element_type=F32)
                  - jnp.dot(xim, cim_ref[c], preferred_element_type=F32))
    y = jnp.concatenate(ys, axis=1) + dsk_ref[...] * u_ref[...].astype(F32)
    v = jax.nn.gelu(y)
    gate = jnp.dot(v.astype(BF16), wg_ref[...], preferred_element_type=F32) + bg_ref[...]
    out = (v * jax.nn.sigmoid(gate)).astype(y_ref.dtype)
    bsz = y_ref.shape[0]
    y_ref[...] = jnp.swapaxes(out.reshape(steps, bsz, out.shape[1]), 0, 1)


def _s5_mixer(u_tm, bsz, bmat, ar, ai, cre, cim, d_skip, w_glu, b_glu):
    rows, width = u_tm.shape
    seq = rows // bsz
    steps = min(S5_TIME_CHUNK, seq)
    r = steps * bsz
    n_state = ar.shape[1] * 2
    return pl.pallas_call(
        functools.partial(_s5_body, steps=steps),
        grid=(seq // steps,),
        in_specs=[pl.BlockSpec((r, width), lambda i: (i, 0)),
                  _resident(bmat.shape), _resident(ar.shape), _resident(ai.shape),
                  _resident(cre.shape), _resident(cim.shape), _resident(d_skip.shape),
                  _resident(w_glu.shape), _resident(b_glu.shape)],
        out_specs=pl.BlockSpec((bsz, steps, width), lambda i: (0, i, 0)),
        out_shape=jax.ShapeDtypeStruct((bsz, seq, width), BF16),
        scratch_shapes=[pltpu.VMEM((r, n_state), F32), pltpu.VMEM((bsz, n_state), F32)],
        compiler_params=_params(("arbitrary",)),
        name="s5_mixer",
    )(u_tm, bmat, ar, ai, cre, cim, d_skip, w_glu, b_glu)


def _moba_tables(nb):
    per = MOBA_ITEMS
    items = [(i, jp, 0) for i in range(nb) for jp in range(i // 2)]
    n_past, rest = divmod(len(items), per)
    items += [(i, i // 2, 0 if i % 2 else 1) for i in range(nb)]
    n_groups, rest2 = divmod(len(items), per)
    assert nb % 2 == 0 and not rest and not rest2 and 2 <= n_past < n_groups
    tab = np.zeros((6, per * (n_groups + 2)), np.int32)
    for c in range(tab.shape[1]):
        if c < len(items):
            tab[0:3, c] = items[c]
        if 0 <= c - per < len(items):
            tab[3, c] = items[c - per][0]
        if 0 <= c - 2 * per < len(items):
            tab[4:6, c] = items[c - 2 * per][0:2]
    return tab, n_past, n_groups


def _moba_body(tab_ref, q_ref, k_ref, v_ref, o_ref, kx_ref, vt_ref, cb_ref, qx_ref,
               m_ref, acc_ref, s_ring, cm_ring, p_ring, al_ring, *, nb, tab, n_past, n_groups):
    blk = MOBA_BLOCK
    pair = 2 * HEAD_DIM
    seq = nb * blk
    two = 2 * blk
    vrows = vt_ref.shape[2]

    @pl.when((pl.program_id(0) == 0) & (pl.program_id(1) == 0))
    def _():
        row = lax.broadcasted_iota(jnp.int32, (two, pair), 0)
        lane = lax.broadcasted_iota(jnp.int32, (two, pair), 1)
        ones_row = lax.broadcasted_iota(jnp.int32, (vrows - HEAD_DIM, two), 0) == 0
        for jp in range(nb // 2):
            block = jnp.where(row < blk, 2 * jp, 2 * jp + 1)
            kx_ref[jp, :, pair:] = jnp.where(lane == block, 1.0, 0.0).astype(BF16)
            for h in range(2):
                vt_ref[jp, h, HEAD_DIM:, :] = jnp.where(ones_row, 1.0, 0.0).astype(BF16)
        kpos = lax.broadcasted_iota(jnp.int32, (blk, blk), 0)
        qpos = lax.broadcasted_iota(jnp.int32, (blk, blk), 1)
        tri = jnp.where(kpos <= qpos, 0.0, NEG_INF)
        for i in range(nb):
            for h in range(2):
                other = slice((1 - h) * HEAD_DIM, (2 - h) * HEAD_DIM)
                qx_ref[i, other, h * blk:(h + 1) * blk] = jnp.zeros((HEAD_DIM, blk), BF16)
        cb_ref[0] = jnp.concatenate([jnp.zeros((blk, blk), F32), tri], axis=0)
        cb_ref[1] = jnp.concatenate([tri, jnp.full((blk, blk), NEG_INF, F32)], axis=0)

    for jp in range(nb // 2):
        cols = slice(jp * two, (jp + 1) * two)
        kx_ref[jp, :, 0:pair] = k_ref[0, cols, :]
        for h in range(2):
            vt_ref[jp, h, 0:HEAD_DIM, :] = v_ref[0, h * HEAD_DIM:(h + 1) * HEAD_DIM, cols]
    km = jnp.mean(k_ref[0].astype(F32).reshape(nb, blk, pair), axis=1)

    q2 = q_ref[0]
    kdim = lax.broadcasted_iota(jnp.int32, km.shape, 1)
    nidx = lax.broadcasted_iota(jnp.int32, (nb, seq), 0)
    qblk = lax.shift_right_logical(lax.broadcasted_iota(jnp.int32, (nb, seq), 1), blk.bit_length() - 1)
    for h in range(2):
        dims = slice(h * HEAD_DIM, (h + 1) * HEAD_DIM)
        km_h = jnp.where((kdim < HEAD_DIM) if h == 0 else (kdim >= HEAD_DIM), km, 0.0).astype(BF16)
        g = jnp.dot(km_h, q2, preferred_element_type=F32)
        g = jnp.where(nidx < qblk, g, NEG_INF)
        picked = jnp.zeros((nb, seq), jnp.int32)
        for _ in range(min(MOBA_TOPK, nb - 1)):
            top = jnp.max(g, axis=0, keepdims=True)
            first = jnp.min(jnp.where(g == top, nidx, nb), axis=0, keepdims=True)
            hit = nidx == first
            picked = jnp.where(hit, 1, picked)
            g = jnp.where(hit, TAKEN, g)
        allow = ((picked == 1) & (nidx < qblk)) | (nidx == qblk)
        bias = jnp.where(allow, 0.0, NEG_INF)
        bias = jnp.concatenate([bias, jnp.zeros((pair - nb, seq), F32)], axis=0).astype(BF16)
        for i in range(nb):
            qx_ref[i, dims, h * blk:(h + 1) * blk] = q_ref[0, dims, i * blk:(i + 1) * blk]
            qx_ref[i, pair:, h * blk:(h + 1) * blk] = bias[:, i * blk:(i + 1) * blk]

    m_ref[...] = jnp.full(m_ref.shape, NEG_INF, F32)
    acc_ref[...] = jnp.zeros_like(acc_ref)

    def entry(row, c):
        return int(tab[row, c]) if isinstance(c, int) else tab_ref[row, c]

    def step(k, carry, own_block, stages=(1, 2, 3)):
        for e in range(MOBA_ITEMS):
            c = MOBA_ITEMS * k + e
            for h in range(2):
                r = 2 * e + h
                if 3 in stages:
                    a3 = 2 * entry(4, c) + h
                    pv = jnp.dot(vt_ref[entry(5, c), h], p_ring[r], preferred_element_type=F32)
                    acc_ref[a3] = al_ring[r] * acc_ref[a3] + pv
                if 2 in stages:
                    a2 = 2 * entry(3, c) + h
                    m_old = m_ref[a2]
                    m_new = jnp.maximum(m_old, cm_ring[r])
                    m_ref[a2] = m_new
                    al_ring[r] = jnp.exp2(m_old - m_new)
                    p_ring[r] = jnp.exp2(s_ring[r] - m_new).astype(BF16)
            if 1 in stages:
                s2 = jnp.dot(kx_ref[entry(1, c)], qx_ref[entry(0, c)], preferred_element_type=F32)
                for h in range(2):
                    s = s2[:, h * blk:(h + 1) * blk]
                    if own_block:
                        s = s + cb_ref[entry(2, c)]
                    s_ring[2 * e + h] = s
                    cm_ring[2 * e + h] = jnp.max(s, axis=0, keepdims=True)
        return carry

    step(0, 0, own_block=False, stages=(1,))
    step(1, 0, own_block=False, stages=(1, 2))
    lax.fori_loop(2, n_past, functools.partial(step, own_block=False), 0)
    lax.fori_loop(n_past, n_groups, functools.partial(step, own_block=True), 0)
    step(n_groups, 0, own_block=False, stages=(2, 3))
    step(n_groups + 1, 0, own_block=False, stages=(3,))

    for i in range(nb):
        o = jnp.concatenate([acc_ref[2 * i + h, 0:HEAD_DIM, :] / acc_ref[2 * i + h, HEAD_DIM:HEAD_DIM + 1, :]
                             for h in range(2)], axis=0)
        o_ref[0, i * blk:(i + 1) * blk, :] = o.T.astype(o_ref.dtype)


def _moba(qt, k, vt):
    bsz, seq, width = k.shape
    blk = MOBA_BLOCK
    nb = seq // blk
    pair = 2 * HEAD_DIM
    tab, n_past, n_groups = _moba_tables(nb)
    spec = pl.BlockSpec((1, seq, pair), lambda b, hp, tab_ref: (b, 0, hp))
    spec_t = pl.BlockSpec((1, pair, seq), lambda b, hp, tab_ref: (b, hp, 0))
    n_state = 2 * nb
    n_ring = 2 * MOBA_ITEMS
    vrows = HEAD_DIM + BF16_SUBLANES
    return pl.pallas_call(
        functools.partial(_moba_body, nb=nb, tab=tab, n_past=n_past, n_groups=n_groups),
        grid_spec=pltpu.PrefetchScalarGridSpec(
            num_scalar_prefetch=1,
            grid=(bsz, width // pair),
            in_specs=[spec_t, spec, spec_t],
            out_specs=spec,
            scratch_shapes=[
                pltpu.VMEM((nb // 2, 2 * blk, 2 * pair), BF16),
                pltpu.VMEM((nb // 2, 2, vrows, 2 * blk), BF16),
                pltpu.VMEM((2, 2 * blk, blk), F32),
                pltpu.VMEM((nb, 2 * pair, 2 * blk), BF16),
                pltpu.VMEM((n_state, 1, blk), F32),
                pltpu.VMEM((n_state, vrows, blk), F32),
                pltpu.VMEM((n_ring, 2 * blk, blk), F32),
                pltpu.VMEM((n_ring, 1, blk), F32),
                pltpu.VMEM((n_ring, 2 * blk, blk), BF16),
                pltpu.VMEM((n_ring, 1, blk), F32),
            ]),
        out_shape=jax.ShapeDtypeStruct((bsz, seq, width), BF16),
        compiler_params=_params(("arbitrary", "arbitrary")),
        name="moba",
    )(jnp.asarray(tab), qt, k, vt)


def _post_body(x_ref, ya_ref, yb_ref, ga_ref, gb_ref, wa_ref, wb_ref, wo_ref, gpm_ref, gpf_ref,
               w1_ref, w2_ref, gpo_ref, o_ref):
    a = jnp.dot(ya_ref[...], wa_ref[...], preferred_element_type=F32)
    b = jnp.dot(yb_ref[...], wb_ref[...], preferred_element_type=F32)
    merged = (jax.nn.sigmoid(ga_ref[...].astype(F32)) * a
              + jax.nn.sigmoid(gb_ref[...].astype(F32)) * b)
    mix = jnp.dot(merged.astype(BF16), wo_ref[...], preferred_element_type=F32)
    h = x_ref[...] + _rms(mix, gpm_ref[...])
    f = _rms(h, gpf_ref[...]).astype(BF16)
    d_ff = w1_ref.shape[1]
    ck = min(FFN_CHUNK, d_ff)
    acc = jnp.zeros(h.shape, F32)
    for c in range(d_ff // ck):
        t = jnp.dot(f, w1_ref[:, c * ck:(c + 1) * ck], preferred_element_type=F32)
        t = jnp.square(jnp.maximum(t, 0.0)).astype(BF16)
        acc = acc + jnp.dot(t, w2_ref[c * ck:(c + 1) * ck, :], preferred_element_type=F32)
    o_ref[...] = h + _rms(acc, gpo_ref[...])


def _post(x2, ya, yb, ga, gb, wa, wb, wo, gpm, gpf, w1, w2, gpo):
    t, d = x2.shape
    tm = min(POST_ROWS, t)

    def rows(a):
        return pl.BlockSpec((tm, a.shape[1]), lambda i: (i, 0))

    consts = (wa, wb, wo, gpm, gpf, w1, w2, gpo)
    return pl.pallas_call(
        _post_body,
        grid=(t // tm,),
        in_specs=[rows(a) for a in (x2, ya, yb, ga, gb)] + [_resident(c.shape) for c in consts],
        out_specs=rows(x2),
        out_shape=jax.ShapeDtypeStruct((t, d), x2.dtype),
        compiler_params=_params(("parallel",)),
        name="post",
    )(x2, ya, yb, ga, gb, *consts)


def _layer(h, g_pre_mix, w_in, lam_re, lam_im, log_dt, b_re, b_im, c_re, c_im, d_skip, w_glu, b_glu,
           w_branch_a, w_branch_b, w_out, g_post_mix, g_pre_ffn, w_ff1, w_ff2, g_post_ffn):
    bsz, seq, d = h.shape
    ssm_w = lam_re.shape[0] * SSM_GROUP
    attn_w = ATTN_HEADS * HEAD_DIM
    x2 = h.reshape(bsz * seq, d)

    (zs, qt, k, vt, ga, gb), (w_glu, w_branch_a, w_branch_b, w_out, w_ff1, w_ff2) = _inproj(
        h, g_pre_mix.reshape(1, d), w_in.astype(BF16),
        (ssm_w, attn_w, attn_w, attn_w, d, d),
        ("lbc", "bcl", "blc", "bcl", "blc", "blc"),
        (1.0, MOBA_Q_SCALE, 1.0, 1.0, 1.0, 1.0),
        (w_glu, w_branch_a, w_branch_b, w_out, w_ff1, w_ff2))

    ar8, ai8, bmat, cre, cim = _s5_prep(lam_re, lam_im, log_dt, b_re, b_im, c_re, c_im, bsz)

    ya = _s5_mixer(zs.reshape(seq * bsz, ssm_w), bsz, bmat, ar8, ai8, cre, cim, d_skip.reshape(1, ssm_w),
                   w_glu, b_glu.reshape(1, ssm_w)).reshape(bsz * seq, ssm_w)
    yb = _moba(qt, k, vt).reshape(bsz * seq, attn_w)
    ga, gb = ga.reshape(bsz * seq, d), gb.reshape(bsz * seq, d)

    out = _post(x2, ya, yb, ga, gb, w_branch_a, w_branch_b, w_out, g_post_mix.reshape(1, d),
                g_pre_ffn.reshape(1, d), w_ff1, w_ff2, g_post_ffn.reshape(1, d))
    return out.reshape(bsz, seq, d)


def kernel(x, g_pre_mix, w_in, lam_re, lam_im, log_dt, b_re, b_im, c_re, c_im, d_skip, w_glu, b_glu,
           w_branch_a, w_branch_b, w_out, g_post_mix, g_pre_ffn, w_ff1, w_ff2, g_post_ffn):
    per_layer = (g_pre_mix, w_in, lam_re, lam_im, log_dt, b_re, b_im, c_re, c_im, d_skip, w_glu, b_glu,
                 w_branch_a, w_branch_b, w_out, g_post_mix, g_pre_ffn, w_ff1, w_ff2, g_post_ffn)
    h = x
    for l in range(g_pre_mix.shape[0]):
        h = _layer(h, *(p[l] for p in per_layer))
    return h
```

```python
import functools

import jax
import jax.numpy as jnp
import numpy as np
from jax import lax
from jax.experimental import pallas as pl
from jax.experimental.pallas import tpu as pltpu

F32 = jnp.float32
BF16 = jnp.bfloat16

RMS_EPS = 1e-6
NEG_INF = -1e30
TAKEN = -3e38

SSM_GROUP = 16
ATTN_HEADS = 8
HEAD_DIM = 64
MOBA_BLOCK = 256
MOBA_TOPK = 3
MOBA_ITEMS = 4

LOG2_E = 1.4426950408889634
MOBA_Q_SCALE = LOG2_E * HEAD_DIM ** -0.5

SUBLANES = 8
BF16_SUBLANES = 16
VMEM_LIMIT_BYTES = 56 * 1024 * 1024

INPROJ_ROWS = 1024
POST_ROWS = 512
S5_TIME_CHUNK = 64
S5_COL_GROUP = 512
FFN_CHUNK = 1024


def _params(sem):
    return pltpu.CompilerParams(dimension_semantics=sem, vmem_limit_bytes=VMEM_LIMIT_BYTES)


def _resident(shape):
    zeros = (0,) * len(shape)
    return pl.BlockSpec(shape, lambda *_: zeros, pipeline_mode=pl.Buffered(1))


def _rms(x, g):
    var = jnp.mean(x * x, axis=-1, keepdims=True)
    return (x * lax.rsqrt(var + RMS_EPS)) * g


def _inproj_body(x_ref, g_ref, w_ref, *refs, layouts, scales):
    n_out = len(layouts)
    n_cast = (len(refs) - n_out) // 2
    cast_src, out_refs, cast_dst = refs[:n_cast], refs[n_cast:n_cast + n_out], refs[n_cast + n_out:]
    for src, dst in zip(cast_src, cast_dst):
        dst[...] = src[...].astype(dst.dtype)

    bsz, tl, d = x_ref.shape
    ub = _rms(x_ref[...].reshape(bsz * tl, d), g_ref[...]).astype(BF16)
    col = 0
    for ref, layout, scale in zip(out_refs, layouts, scales):
        n = ref.shape[1] if layout == "bcl" else ref.shape[2]
        z = jnp.dot(ub, w_ref[:, col:col + n], preferred_element_type=F32).reshape(bsz, tl, n)
        if scale != 1.0:
            z = z * scale
        if layout == "lbc":
            z = jnp.swapaxes(z.astype(ref.dtype), 0, 1)
        elif layout == "bcl":
            z = jnp.swapaxes(z, 1, 2)
        ref[...] = z.astype(ref.dtype)
        col += n


def _inproj(x, g, w, widths, layouts, scales, to_bf16):
    bsz, seq, d = x.shape
    tl = min(INPROJ_ROWS // bsz, seq)
    steps = seq // tl
    specs = {"blc": lambda n: (pl.BlockSpec((bsz, tl, n), lambda i: (0, i, 0)), (bsz, seq, n)),
             "lbc": lambda n: (pl.BlockSpec((tl, bsz, n), lambda i: (i, 0, 0)), (seq, bsz, n)),
             "bcl": lambda n: (pl.BlockSpec((bsz, n, tl), lambda i: (0, 0, i)), (bsz, n, seq))}
    outs = [specs[layout](n) for n, layout in zip(widths, layouts)]
    slabs = [pl.BlockSpec((a.shape[0] // steps, a.shape[1]), lambda i: (i, 0)) for a in to_bf16]
    assert all(a.shape[0] % (steps * BF16_SUBLANES) == 0 for a in to_bf16)
    res = pl.pallas_call(
        functools.partial(_inproj_body, layouts=layouts, scales=scales),
        grid=(steps,),
        in_specs=[pl.BlockSpec((bsz, tl, d), lambda i: (0, i, 0)), _resident(g.shape), _resident(w.shape)]
        + slabs,
        out_specs=[spec for spec, _ in outs] + slabs,
        out_shape=[jax.ShapeDtypeStruct(shape, BF16) for _, shape in outs]
        + [jax.ShapeDtypeStruct(a.shape, BF16) for a in to_bf16],
        compiler_params=pltpu.CompilerParams(
            dimension_semantics=("arbitrary",), vmem_limit_bytes=VMEM_LIMIT_BYTES,
            allow_input_fusion=[False, False, True] + [False] * len(to_bf16)),
        name="inproj",
    )(x, g, w, *to_bf16)
    return res[:len(outs)], res[len(outs):]


def _s5_prep_body(lr_ref, li_ref, ldt_ref, br_ref, bi_ref, crt_ref, cit_ref,
                  ar_ref, ai_ref, bmat_ref, cre_ref, cim_ref):
    lr = lr_ref[...]
    li = li_ref[...]
    dt = jnp.exp(ldt_ref[...])
    mag = jnp.exp(lr * dt)
    ab_re = mag * jnp.cos(li * dt)
    ab_im = mag * jnp.sin(li * dt)
    nr = ab_re - 1.0
    ni = ab_im
    den = lr * lr + li * li
    coef_re = (nr * lr + ni * li) / den
    coef_im = (ni * lr - nr * li) / den
    br = br_ref[...]
    bi = bi_ref[...]
    bbr = coef_re * br - coef_im * bi
    bbi = coef_re * bi + coef_im * br

    g, h, p = br.shape
    hg = g // 2
    bmat_ref[...] = jnp.zeros_like(bmat_ref)
    cre_ref[...] = jnp.zeros_like(cre_ref)
    cim_ref[...] = jnp.zeros_like(cim_ref)
    for gi in range(g):
        c, gl = divmod(gi, hg)
        rows, cols = slice(gl * h, (gl + 1) * h), slice(gl * p, (gl + 1) * p)
        bmat_ref[c, rows, cols] = bbr[gi].astype(BF16)
        bmat_ref[c, rows, hg * p + gl * p:hg * p + (gl + 1) * p] = bbi[gi].astype(BF16)
        cre_ref[c, cols, rows] = crt_ref[gi].astype(BF16)
        cim_ref[c, cols, rows] = cit_ref[gi].astype(BF16)
        ar_ref[:, gi * p:(gi + 1) * p] = jnp.broadcast_to(ab_re[gi], (ar_ref.shape[0], p))
        ai_ref[:, gi * p:(gi + 1) * p] = jnp.broadcast_to(ab_im[gi], (ai_ref.shape[0], p))


def _s5_prep(lam_re, lam_im, log_dt, b_re, b_im, c_re, c_im, bsz):
    g, p = lam_re.shape
    h = b_re.shape[-1]
    lr = lam_re.reshape(g, 1, p)
    li = lam_im.reshape(g, 1, p)
    ldt = jnp.broadcast_to(log_dt.reshape(g, 1, 1), (g, 1, p))
    br = jnp.transpose(b_re, (0, 2, 1))
    bi = jnp.transpose(b_im, (0, 2, 1))
    crt = jnp.transpose(c_re, (0, 2, 1))
    cit = jnp.transpose(c_im, (0, 2, 1))
    hg = g // 2
    coef = jax.ShapeDtypeStruct((bsz, g * p), F32)
    bmat = jax.ShapeDtypeStruct((2, hg * h, 2 * hg * p), BF16)
    cmat = jax.ShapeDtypeStruct((2, hg * p, hg * h), BF16)
    return pl.pallas_call(_s5_prep_body, out_shape=[coef, coef, bmat, cmat, cmat],
                          name="s5_prep")(lr, li, ldt, br, bi, crt, cit)


def _s5_body(u_ref, b_ref, ar_ref, ai_ref, cre_ref, cim_ref, dsk_ref, wg_ref, bg_ref, y_ref,
             bu_ref, st_ref, *, steps):
    half_in = u_ref.shape[1] // 2
    half_st = st_ref.shape[1] // 4

    @pl.when(pl.program_id(0) == 0)
    def _():
        st_ref[...] = jnp.zeros_like(st_ref)

    for c in range(2):
        bu_ref[:, 2 * c * half_st:2 * (c + 1) * half_st] = jnp.dot(
            u_ref[:, c * half_in:(c + 1) * half_in], b_ref[c], preferred_element_type=F32)

    w = S5_COL_GROUP
    for c in range(2):
        for s in range(half_st // w):
            re0 = 2 * c * half_st + s * w
            im0 = re0 + half_st
            a0 = c * half_st + s * w
            ar = ar_ref[:, a0:a0 + w]
            ai = ai_ref[:, a0:a0 + w]
            xr = st_ref[:, re0:re0 + w]
            xi = st_ref[:, im0:im0 + w]
            for t in range(steps):
                rows = slice(t * SUBLANES, (t + 1) * SUBLANES)
                xr, xi = (ar * xr - ai * xi + bu_ref[rows, re0:re0 + w],
                          ar * xi + ai * xr + bu_ref[rows, im0:im0 + w])
                bu_ref[rows, re0:re0 + w] = xr
                bu_ref[rows, im0:im0 + w] = xi
            st_ref[:, re0:re0 + w] = xr
            st_ref[:, im0:im0 + w] = xi

    ys = []
    for c in range(2):
        xre = bu_ref[:, 2 * c * half_st:(2 * c + 1) * half_st].astype(BF16)
        xim = bu_ref[:, (2 * c + 1) * half_st:(2 * c + 2) * half_st].astype(BF16)
        ys.append(jnp.dot(xre, cre_ref[c], preferred_element_type=F32)
                  - jnp.dot(xim, cim_ref[c], preferred_element_type=F32))
    y = jnp.concatenate(ys, axis=1) + dsk_ref[...] * u_ref[...].astype(F32)
    v = jax.nn.gelu(y)
    gate = jnp.dot(v.astype(BF16), wg_ref[...], preferred_element_type=F32) + bg_ref[...]
    out = (v * jax.nn.sigmoid(gate)).astype(y_ref.dtype)
    bsz = y_ref.shape[0]
    y_ref[...] = jnp.swapaxes(out.reshape(steps, bsz, out.shape[1]), 0, 1)


def _s5_mixer(u_tm, bsz, bmat, ar, ai, cre, cim, d_skip, w_glu, b_glu):
    rows, width = u_tm.shape
    seq = rows // bsz
    steps = min(S5_TIME_CHUNK, seq)
    r = steps * bsz
    n_state = ar.shape[1] * 2
    return pl.pallas_call(
        functools.partial(_s5_body, steps=steps),
        grid=(seq // steps,),
        in_specs=[pl.BlockSpec((r, width), lambda i: (i, 0)),
                  _resident(bmat.shape), _resident(ar.shape), _resident(ai.shape),
                  _resident(cre.shape), _resident(cim.shape), _resident(d_skip.shape),
                  _resident(w_glu.shape), _resident(b_glu.shape)],
        out_specs=pl.BlockSpec((bsz, steps, width), lambda i: (0, i, 0)),
        out_shape=jax.ShapeDtypeStruct((bsz, seq, width), BF16),
        scratch_shapes=[pltpu.VMEM((r, n_state), F32), pltpu.VMEM((bsz, n_state), F32)],
        compiler_params=_params(("arbitrary",)),
        name="s5_mixer",
    )(u_tm, bmat, ar, ai, cre, cim, d_skip, w_glu, b_glu)


def _moba_tables(nb):
    per = MOBA_ITEMS
    items = [(i, jp, 0) for i in range(nb) for jp in range(i // 2)]
    n_past, rest = divmod(len(items), per)
    items += [(i, i // 2, 0 if i % 2 else 1) for i in range(nb)]
    n_groups, rest2 = divmod(len(items), per)
    assert nb % 2 == 0 and not rest and not rest2 and 2 <= n_past < n_groups
    tab = np.zeros((6, per * (n_groups + 2)), np.int32)
    for c in range(tab.shape[1]):
        if c < len(items):
            tab[0:3, c] = items[c]
        if 0 <= c - per < len(items):
            tab[3, c] = items[c - per][0]
        if 0 <= c - 2 * per < len(items):
            tab[4:6, c] = items[c - 2 * per][0:2]
    return tab, n_past, n_groups


def _moba_body(tab_ref, q_ref, k_ref, v_ref, o_ref, kx_ref, vt_ref, cb_ref, qx_ref,
               m_ref, acc_ref, s_ring, cm_ring, p_ring, al_ring, *, nb, tab, n_past, n_groups):
    blk = MOBA_BLOCK
    pair = 2 * HEAD_DIM
    seq = nb * blk
    two = 2 * blk
    vrows = vt_ref.shape[2]

    @pl.when((pl.program_id(0) == 0) & (pl.program_id(1) == 0))
    def _():
        row = lax.broadcasted_iota(jnp.int32, (two, pair), 0)
        lane = lax.broadcasted_iota(jnp.int32, (two, pair), 1)
        ones_row = lax.broadcasted_iota(jnp.int32, (vrows - HEAD_DIM, two), 0) == 0
        for jp in range(nb // 2):
            block = jnp.where(row < blk, 2 * jp, 2 * jp + 1)
            kx_ref[jp, :, pair:] = jnp.where(lane == block, 1.0, 0.0).astype(BF16)
            for h in range(2):
                vt_ref[jp, h, HEAD_DIM:, :] = jnp.where(ones_row, 1.0, 0.0).astype(BF16)
        kpos = lax.broadcasted_iota(jnp.int32, (blk, blk), 0)
        qpos = lax.broadcasted_iota(jnp.int32, (blk, blk), 1)
        tri = jnp.where(kpos <= qpos, 0.0, NEG_INF)
        for i in range(nb):
            for h in range(2):
                other = slice((1 - h) * HEAD_DIM, (2 - h) * HEAD_DIM)
                qx_ref[i, other, h * blk:(h + 1) * blk] = jnp.zeros((HEAD_DIM, blk), BF16)
        cb_ref[0] = jnp.concatenate([jnp.zeros((blk, blk), F32), tri], axis=0)
        cb_ref[1] = jnp.concatenate([tri, jnp.full((blk, blk), NEG_INF, F32)], axis=0)

    for jp in range(nb // 2):
        cols = slice(jp * two, (jp + 1) * two)
        kx_ref[jp, :, 0:pair] = k_ref[0, cols, :]
        for h in range(2):
            vt_ref[jp, h, 0:HEAD_DIM, :] = v_ref[0, h * HEAD_DIM:(h + 1) * HEAD_DIM, cols]
    km = jnp.mean(k_ref[0].astype(F32).reshape(nb, blk, pair), axis=1)

    q2 = q_ref[0]
    kdim = lax.broadcasted_iota(jnp.int32, km.shape, 1)
    nidx = lax.broadcasted_iota(jnp.int32, (nb, seq), 0)
    qblk = lax.shift_right_logical(lax.broadcasted_iota(jnp.int32, (nb, seq), 1), blk.bit_length() - 1)
    for h in range(2):
        dims = slice(h * HEAD_DIM, (h + 1) * HEAD_DIM)
        km_h = jnp.where((kdim < HEAD_DIM) if h == 0 else (kdim >= HEAD_DIM), km, 0.0).astype(BF16)
        g = jnp.dot(km_h, q2, preferred_element_type=F32)
        g = jnp.where(nidx < qblk, g, NEG_INF)
        picked = jnp.zeros((nb, seq), jnp.int32)
        for _ in range(min(MOBA_TOPK, nb - 1)):
            top = jnp.max(g, axis=0, keepdims=True)
            first = jnp.min(jnp.where(g == top, nidx, nb), axis=0, keepdims=True)
            hit = nidx == first
            picked = jnp.where(hit, 1, picked)
            g = jnp.where(hit, TAKEN, g)
        allow = ((picked == 1) & (nidx < qblk)) | (nidx == qblk)
        bias = jnp.where(allow, 0.0, NEG_INF)
        bias = jnp.concatenate([bias, jnp.zeros((pair - nb, seq), F32)], axis=0).astype(BF16)
        for i in range(nb):
            qx_ref[i, dims, h * blk:(h + 1) * blk] = q_ref[0, dims, i * blk:(i + 1) * blk]
            qx_ref[i, pair:, h * blk:(h + 1) * blk] = bias[:, i * blk:(i + 1) * blk]

    m_ref[...] = jnp.full(m_ref.shape, NEG_INF, F32)
    acc_ref[...] = jnp.zeros_like(acc_ref)

    def entry(row, c):
        return int(tab[row, c]) if isinstance(c, int) else tab_ref[row, c]

    def step(k, carry, own_block, stages=(1, 2, 3)):
        for e in range(MOBA_ITEMS):
            c = MOBA_ITEMS * k + e
            for h in range(2):
                r = 2 * e + h
                if 3 in stages:
                    a3 = 2 * entry(4, c) + h
                    pv = jnp.dot(vt_ref[entry(5, c), h], p_ring[r], preferred_element_type=F32)
                    acc_ref[a3] = al_ring[r] * acc_ref[a3] + pv
                if 2 in stages:
                    a2 = 2 * entry(3, c) + h
                    m_old = m_ref[a2]
                    m_new = jnp.maximum(m_old, cm_ring[r])
                    m_ref[a2] = m_new
                    al_ring[r] = jnp.exp2(m_old - m_new)
                    p_ring[r] = jnp.exp2(s_ring[r] - m_new).astype(BF16)
            if 1 in stages:
                s2 = jnp.dot(kx_ref[entry(1, c)], qx_ref[entry(0, c)], preferred_element_type=F32)
                for h in range(2):
                    s = s2[:, h * blk:(h + 1) * blk]
                    if own_block:
                        s = s + cb_ref[entry(2, c)]
                    s_ring[2 * e + h] = s
                    cm_ring[2 * e + h] = jnp.max(s, axis=0, keepdims=True)
        return carry

    step(0, 0, own_block=False, stages=(1,))
    step(1, 0, own_block=False, stages=(1, 2))
    lax.fori_loop(2, n_past, functools.partial(step, own_block=False), 0)
    lax.fori_loop(n_past, n_groups, functools.partial(step, own_block=True), 0)
    step(n_groups, 0, own_block=False, stages=(2, 3))
    step(n_groups + 1, 0, own_block=False, stages=(3,))

    for i in range(nb):
        o = jnp.concatenate([acc_ref[2 * i + h, 0:HEAD_DIM, :] / acc_ref[2 * i + h, HEAD_DIM:HEAD_DIM + 1, :]
                             for h in range(2)], axis=0)
        o_ref[0, i * blk:(i + 1) * blk, :] = o.T.astype(o_ref.dtype)


def _moba(qt, k, vt):
    bsz, seq, width = k.shape
    blk = MOBA_BLOCK
    nb = seq // blk
    pair = 2 * HEAD_DIM
    tab, n_past, n_groups = _moba_tables(nb)
    spec = pl.BlockSpec((1, seq, pair), lambda b, hp, tab_ref: (b, 0, hp))
    spec_t = pl.BlockSpec((1, pair, seq), lambda b, hp, tab_ref: (b, hp, 0))
    n_state = 2 * nb
    n_ring = 2 * MOBA_ITEMS
    vrows = HEAD_DIM + BF16_SUBLANES
    return pl.pallas_call(
        functools.partial(_moba_body, nb=nb, tab=tab, n_past=n_past, n_groups=n_groups),
        grid_spec=pltpu.PrefetchScalarGridSpec(
            num_scalar_prefetch=1,
            grid=(bsz, width // pair),
            in_specs=[spec_t, spec, spec_t],
            out_specs=spec,
            scratch_shapes=[
                pltpu.VMEM((nb // 2, 2 * blk, 2 * pair), BF16),
                pltpu.VMEM((nb // 2, 2, vrows, 2 * blk), BF16),
                pltpu.VMEM((2, 2 * blk, blk), F32),
                pltpu.VMEM((nb, 2 * pair, 2 * blk), BF16),
                pltpu.VMEM((n_state, 1, blk), F32),
                pltpu.VMEM((n_state, vrows, blk), F32),
                pltpu.VMEM((n_ring, 2 * blk, blk), F32),
                pltpu.VMEM((n_ring, 1, blk), F32),
                pltpu.VMEM((n_ring, 2 * blk, blk), BF16),
                pltpu.VMEM((n_ring, 1, blk), F32),
            ]),
        out_shape=jax.ShapeDtypeStruct((bsz, seq, width), BF16),
        compiler_params=_params(("arbitrary", "arbitrary")),
        name="moba",
    )(jnp.asarray(tab), qt, k, vt)


def _post_body(x_ref, ya_ref, yb_ref, ga_ref, gb_ref, wa_ref, wb_ref, wo_ref, gpm_ref, gpf_ref,
               w1_ref, w2_ref, gpo_ref, o_ref):
    a = jnp.dot(ya_ref[...], wa_ref[...], preferred_element_type=F32)
    b = jnp.dot(yb_ref[...], wb_ref[...], preferred_element_type=F32)
    merged = (jax.nn.sigmoid(ga_ref[...].astype(F32)) * a
              + jax.nn.sigmoid(gb_ref[...].astype(F32)) * b)
    mix = jnp.dot(merged.astype(BF16), wo_ref[...], preferred_element_type=F32)
    h = x_ref[...] + _rms(mix, gpm_ref[...])
    f = _rms(h, gpf_ref[...]).astype(BF16)
    d_ff = w1_ref.shape[1]
    ck = min(FFN_CHUNK, d_ff)
    acc = jnp.zeros(h.shape, F32)
    for c in range(d_ff // ck):
        t = jnp.dot(f, w1_ref[:, c * ck:(c + 1) * ck], preferred_element_type=F32)
        t = jnp.square(jnp.maximum(t, 0.0)).astype(BF16)
        acc = acc + jnp.dot(t, w2_ref[c * ck:(c + 1) * ck, :], preferred_element_type=F32)
    o_ref[...] = h + _rms(acc, gpo_ref[...])


def _post(x2, ya, yb, ga, gb, wa, wb, wo, gpm, gpf, w1, w2, gpo):
    t, d = x2.shape
    tm = min(POST_ROWS, t)

    def rows(a):
        return pl.BlockSpec((tm, a.shape[1]), lambda i: (i, 0))

    consts = (wa, wb, wo, gpm, gpf, w1, w2, gpo)
    return pl.pallas_call(
        _post_body,
        grid=(t // tm,),
        in_specs=[rows(a) for a in (x2, ya, yb, ga, gb)] + [_resident(c.shape) for c in consts],
        out_specs=rows(x2),
        out_shape=jax.ShapeDtypeStruct((t, d), x2.dtype),
        compiler_params=_params(("parallel",)),
        name="post",
    )(x2, ya, yb, ga, gb, *consts)


def _layer(h, g_pre_mix, w_in, lam_re, lam_im, log_dt, b_re, b_im, c_re, c_im, d_skip, w_glu, b_glu,
           w_branch_a, w_branch_b, w_out, g_post_mix, g_pre_ffn, w_ff1, w_ff2, g_post_ffn):
    bsz, seq, d = h.shape
    ssm_w = lam_re.shape[0] * SSM_GROUP
    attn_w = ATTN_HEADS * HEAD_DIM
    x2 = h.reshape(bsz * seq, d)

    (zs, qt, k, vt, ga, gb), (w_glu, w_branch_a, w_branch_b, w_out, w_ff1, w_ff2) = _inproj(
        h, g_pre_mix.reshape(1, d), w_in.astype(BF16),
        (ssm_w, attn_w, attn_w, attn_w, d, d),
        ("lbc", "bcl", "blc", "bcl", "blc", "blc"),
        (1.0, MOBA_Q_SCALE, 1.0, 1.0, 1.0, 1.0),
        (w_glu, w_branch_a, w_branch_b, w_out, w_ff1, w_ff2))

    ar8, ai8, bmat, cre, cim = _s5_prep(lam_re, lam_im, log_dt, b_re, b_im, c_re, c_im, bsz)

    ya = _s5_mixer(zs.reshape(seq * bsz, ssm_w), bsz, bmat, ar8, ai8, cre, cim, d_skip.reshape(1, ssm_w),
                   w_glu, b_glu.reshape(1, ssm_w)).reshape(bsz * seq, ssm_w)
    yb = _moba(qt, k, vt).reshape(bsz * seq, attn_w)
    ga, gb = ga.reshape(bsz * seq, d), gb.reshape(bsz * seq, d)

    out = _post(x2, ya, yb, ga, gb, w_branch_a, w_branch_b, w_out, g_post_mix.reshape(1, d),
                g_pre_ffn.reshape(1, d), w_ff1, w_ff2, g_post_ffn.reshape(1, d))
    return out.reshape(bsz, seq, d)


def kernel(x, g_pre_mix, w_in, lam_re, lam_im, log_dt, b_re, b_im, c_re, c_im, d_skip, w_glu, b_glu,
           w_branch_a, w_branch_b, w_out, g_post_mix, g_pre_ffn, w_ff1, w_ff2, g_post_ffn):
    per_layer = (g_pre_mix, w_in, lam_re, lam_im, log_dt, b_re, b_im, c_re, c_im, d_skip, w_glu, b_glu,
                 w_branch_a, w_branch_b, w_out, g_post_mix, g_pre_ffn, w_ff1, w_ff2, g_post_ffn)
    h = x
    for l in range(g_pre_mix.shape[0]):
        h = _layer(h, *(p[l] for p in per_layer))
    return h
```
